```python
import jax, jax.numpy as jnp
from jax import lax
import numpy as np

D_MODEL = 1024
BATCH = 16
SEQ = 256
DEPTH = 4
DEC_BATCH = 2
DEC_SEQ = 1024
PAST_LEN = 256

GRID_W = 64
N_HEADS_A = 8
N_KV_A = 2
HD_A = 64
Q_BLOCK = 128
ROPE_THETA = 10000.0
N_HEADS_B = 4
DK_B = 64
DV_B = 128
GK_RANK = 16
GK_NORMALIZER = 16.0
GLA_CHUNK = 64
N_HEADS_C = 8
HD_C = 64
W_LORA = 64
A_LORA = 64
G_LORA = 128
GN_EPS = 64e-5
NORM_EPS = 1e-6
WIDTH_A = N_HEADS_A * HD_A
KV_A = N_KV_A * HD_A
WIDTH_BK = N_HEADS_B * DK_B
WIDTH_BV = N_HEADS_B * DV_B
WIDTH_C = N_HEADS_C * HD_C
N_BRANCH = 3
D_FF = ((8 * D_MODEL + 3 * 256 - 1) // (3 * 256)) * 256
OFF_QA = 0
OFF_KA = OFF_QA + WIDTH_A
OFF_VA = OFF_KA + KV_A
OFF_QB = OFF_VA + KV_A
OFF_KB = OFF_QB + WIDTH_BK
OFF_VB = OFF_KB + WIDTH_BK
OFF_GB = OFF_VB + WIDTH_BV
OFF_GKF = OFF_GB + WIDTH_BV
OFF_GKB = OFF_GKF + GK_RANK
OFF_C = OFF_GKB + GK_RANK
C_COLS = 3 * WIDTH_C + 2 * W_LORA + 2 * A_LORA + G_LORA
OFF_GATE = OFF_C + C_COLS
N_IN = OFF_GATE + N_BRANCH * D_MODEL
C_R = 0
C_K = C_R + WIDTH_C
C_V = C_K + WIDTH_C
C_WF = C_V + WIDTH_C
C_WB = C_WF + W_LORA
C_AF = C_WB + W_LORA
C_AB = C_AF + A_LORA
C_G = C_AB + A_LORA

kernel_name = 'hybrid_diffusion_prefix_gqa_gla_rwkv7_step'


def rms_norm(x, g, eps=NORM_EPS):
    xf = x.astype(jnp.float32)
    y = xf * lax.rsqrt(jnp.mean(xf * xf, axis=-1, keepdims=True) + eps)
    return (y * g.astype(jnp.float32)).astype(x.dtype)


def flip_t(a):
    return jnp.flip(a, axis=1)


def axial_rope(x):
    T = x.shape[1]
    rows = T // GRID_W
    t = jnp.arange(rows * GRID_W)
    row = (t // GRID_W).astype(jnp.float32)
    col = (t % GRID_W).astype(jnp.float32)
    n_pairs = HD_A // 4
    inv = ROPE_THETA ** (-jnp.arange(n_pairs, dtype=jnp.float32) / n_pairs)
    ang = jnp.concatenate([row[:, None] * inv, col[:, None] * inv], axis=-1)
    cos = jnp.cos(ang)[None, :, None, :]
    sin = jnp.sin(ang)[None, :, None, :]
    xp = x.astype(jnp.float32).reshape(x.shape[:-1] + (HD_A // 2, 2))
    x0, x1 = xp[..., 0], xp[..., 1]
    out = jnp.stack([x0 * cos - x1 * sin, x0 * sin + x1 * cos], axis=-1)
    return out.reshape(x.shape).astype(x.dtype)


def block_attention(q, k, v):
    B, Tq, H, hd = q.shape
    G = H // N_KV_A
    nb = Tq // Q_BLOCK
    qb = q.reshape(B, nb, Q_BLOCK, N_KV_A, G, hd).transpose(1, 0, 2, 3, 4, 5)
    scale = hd ** -0.5

    def one_block(qblk):
        s = jnp.einsum('bqkgd,bskd->bkgqs', qblk, k, preferred_element_type=jnp.float32) * scale
        p = jax.nn.softmax(s, axis=-1)
        return jnp.einsum('bkgqs,bskd->bqkgd', p.astype(v.dtype), v)

    o = lax.map(one_block, qb)
    return o.transpose(1, 0, 2, 3, 4, 5).reshape(B, Tq, H * hd)


def gla_chunked(q, k, v, log_a, s0):
    B, T, H, dk = q.shape
    dv = v.shape[-1]
    n = T // GLA_CHUNK

    def chunks(a):
        return a.astype(jnp.float32).reshape(B, n, GLA_CHUNK, H, a.shape[-1]).transpose(1, 0, 3, 2, 4)

    causal = jnp.tril(jnp.ones((GLA_CHUNK, GLA_CHUNK), dtype=bool))[:, :, None]

    def step(s, inp):
        qc, kc, vc, gc = inp
        b = jnp.cumsum(gc, axis=2)
        o_inter = jnp.einsum('bhcd,bhde->bhce', qc * jnp.exp(b), s)
        rel = jnp.where(causal, b[:, :, :, None, :] - b[:, :, None, :, :], -jnp.inf)
        att = jnp.sum(qc[:, :, :, None, :] * kc[:, :, None, :, :] * jnp.exp(rel), axis=-1)
        o_intra = jnp.einsum('bhij,bhje->bhie', att, vc)
        b_last = b[:, :, -1:, :]
        s_new = jnp.exp(b_last[:, :, 0, :])[..., None] * s + jnp.einsum('bhcd,bhce->bhde', kc * jnp.exp(b_last - b), vc)
        return s_new, o_inter + o_intra

    s_fin, o = lax.scan(step, s0.astype(jnp.float32), (chunks(q), chunks(k), chunks(v), chunks(log_a)))
    return o.transpose(1, 0, 3, 2, 4).reshape(B, T, H, dv), s_fin


def rwkv7_scan(r, decay, k, v, a_vec, b_vec, s0):
    def step(s, inp):
        rt, wt, kt, vt, at, bt = inp
        sa = jnp.einsum('bhij,bhj->bhi', s, at)
        s = s * wt[:, :, None, :] + sa[..., None] * bt[:, :, None, :] + vt[..., None] * kt[:, :, None, :]
        return s, jnp.einsum('bhij,bhj->bhi', s, rt)

    seq = tuple(z.astype(jnp.float32).transpose(1, 0, 2, 3) for z in (r, decay, k, v, a_vec, b_vec))
    s_fin, y = lax.scan(step, s0.astype(jnp.float32), seq)
    return y.transpose(1, 0, 2, 3), s_fin


def centred_shift(u, mu):
    prev = jnp.pad(u[:, :-1], ((0, 0), (1, 0), (0, 0)))
    nxt = jnp.pad(u[:, 1:], ((0, 0), (0, 1), (0, 0)))
    return u + mu * (0.5 * (prev + nxt) - u)


def trunk_layer(x, cond, lp, ctx):
    (w_ada, b_ada, g_mix, g_ffn, w_in, q_gain, k_gain, gk_w2, gk_b, gla_gain,
     mu_c, w0, w2, a0, a2, g2, k_k, k_a, r_k, lnx_w, lnx_b,
     w_po_a, w_po_b, w_po_c, w_out, w_ffn_in, w_ffn_out) = lp
    B, T, _ = x.shape
    latent = ctx is not None
    mod = (jax.nn.silu(cond) @ w_ada + b_ada)[:, None, :]
    sh1, sc1, gt1, sh2, sc2, gt2 = jnp.split(mod, 6, axis=-1)
    h = rms_norm(x, g_mix) * (1 + sc1) + sh1
    p = h @ w_in

    qa = rms_norm(p[..., OFF_QA:OFF_KA].reshape(B, T, N_HEADS_A, HD_A), q_gain)
    ka = rms_norm(p[..., OFF_KA:OFF_VA].reshape(B, T, N_KV_A, HD_A), k_gain)
    va = p[..., OFF_VA:OFF_QB].reshape(B, T, N_KV_A, HD_A)
    if latent:
        keys = jnp.concatenate([ctx[0].astype(x.dtype), axial_rope(ka)], axis=1)
        vals = jnp.concatenate([ctx[1].astype(x.dtype), va], axis=1)
        o_a = block_attention(axial_rope(qa), keys, vals)
    else:
        o_a = block_attention(qa, ka, va)

    qb = p[..., OFF_QB:OFF_KB].reshape(B, T, N_HEADS_B, DK_B) * (DK_B ** -0.5)
    kb = p[..., OFF_KB:OFF_VB].reshape(B, T, N_HEADS_B, DK_B)
    vb = p[..., OFF_VB:OFF_GB].reshape(B, T, N_HEADS_B, DV_B)
    gb = p[..., OFF_GB:OFF_GKF]

    def log_decay(lr, d):
        z = (lr @ gk_w2[d] + gk_b[d]).astype(jnp.float32)
        return (jax.nn.log_sigmoid(z) / GK_NORMALIZER).reshape(B, T, N_HEADS_B, DK_B)

    lg_f = log_decay(p[..., OFF_GKF:OFF_GKB], 0)
    lg_b = log_decay(p[..., OFF_GKB:OFF_C], 1)
    if latent:
        sg_f0, sg_b0 = ctx[2][:, 0], ctx[2][:, 1]
    else:
        sg_f0 = jnp.zeros((B, N_HEADS_B, DK_B, DV_B), jnp.float32)
        sg_b0 = sg_f0
    ob_f, sg_f = gla_chunked(qb, kb, vb, lg_f, sg_f0)
    ob_b, sg_b = gla_chunked(flip_t(qb), flip_t(kb), flip_t(vb), flip_t(lg_b), sg_b0)
    o_b = (ob_f + flip_t(ob_b)).astype(x.dtype)
    o_b = rms_norm(o_b, gla_gain).reshape(B, T, WIDTH_BV) * jax.nn.silu(gb)

    pc = centred_shift(p[..., OFF_C:OFF_GATE], mu_c)
    heads = lambda z: z.reshape(B, T, N_HEADS_C, HD_C)
    r = heads(pc[..., C_R:C_K]).astype(jnp.float32)
    kc = pc[..., C_K:C_V]
    vc = heads(pc[..., C_V:C_WF]).astype(jnp.float32)
    kkf = heads(kc * k_k).astype(jnp.float32)
    kk = kkf * lax.rsqrt(jnp.sum(kkf * kkf, axis=-1, keepdims=True) + 1e-12)

    def dir_terms(d, wlr, alr):
        wl = (w0[d] + jnp.tanh(wlr) @ w2[d]).astype(jnp.float32)
        decay = jnp.exp(-jnp.exp(-jax.nn.softplus(-wl) - 0.5))
        a = jax.nn.sigmoid((a0[d] + alr @ a2[d]).astype(jnp.float32))
        kd = kc.astype(jnp.float32) * (1 + (a - 1) * k_a.astype(jnp.float32))
        return heads(decay), heads(a), heads(kd)

    dec_f, a_f, kd_f = dir_terms(0, pc[..., C_WF:C_WB], pc[..., C_AF:C_AB])
    dec_b, a_b, kd_b = dir_terms(1, pc[..., C_WB:C_AF], pc[..., C_AB:C_G])
    if latent:
        sr_f0, sr_b0 = ctx[3][:, 0], ctx[3][:, 1]
    else:
        sr_f0 = jnp.zeros((B, N_HEADS_C, HD_C, HD_C), jnp.float32)
        sr_b0 = sr_f0
    y_f, sr_f = rwkv7_scan(r, dec_f, kd_f, vc, -kk, kk * a_f, sr_f0)
    y_b, sr_b = rwkv7_scan(flip_t(r), flip_t(dec_b), flip_t(kd_b), flip_t(vc),
                           flip_t(-kk), flip_t(kk * a_b), sr_b0)
    y = y_f + flip_t(y_b)
    mean = jnp.mean(y, axis=-1, keepdims=True)
    var = jnp.var(y, axis=-1, keepdims=True)
    gn = ((y - mean) * lax.rsqrt(var + GN_EPS)).reshape(B, T, WIDTH_C) * lnx_w + lnx_b
    rkf = r_k.astype(jnp.float32)
    bonus = (jnp.sum(r * kd_f * rkf, axis=-1, keepdims=True) + jnp.sum(r * kd_b * rkf, axis=-1, keepdims=True)) * vc
    g_c = jax.nn.sigmoid(pc[..., C_G:]) @ g2
    o_c = ((gn + bonus.reshape(B, T, WIDTH_C)) * g_c).astype(x.dtype)

    gates = jax.nn.sigmoid(p[..., OFF_GATE:].astype(jnp.float32)).astype(x.dtype).reshape(B, T, N_BRANCH, D_MODEL)
    merged = (gates[..., 0, :] * (o_a @ w_po_a) + gates[..., 1, :] * (o_b @ w_po_b)
              + gates[..., 2, :] * (o_c @ w_po_c))
    x = x + gt1 * (merged @ w_out)

    h2 = rms_norm(x, g_ffn) * (1 + sc2) + sh2
    gu = h2 @ w_ffn_in
    x = x + gt2 * ((jax.nn.silu(gu[..., :D_FF]) * gu[..., D_FF:]) @ w_ffn_out)

    if latent:
        return x, None
    return x, (ka, va, jnp.stack([sg_f, sg_b], axis=1), jnp.stack([sr_f, sr_b], axis=1))


def setup_inputs(seed: int = 0) -> dict:
    key = jax.random.key(seed)
    ks = iter(jax.random.split(key, 40))
    L = DEPTH
    D = D_MODEL

    def nrm(shape, scale=1.0):
        return jax.random.normal(next(ks), shape, jnp.float32) * scale

    def uni(shape, lo, hi):
        return jax.random.uniform(next(ks), shape, jnp.float32, lo, hi)

    return {
        'x_prompt': nrm((BATCH, SEQ, D)),
        'x_sample': nrm((DEC_BATCH, DEC_SEQ, D)),
        'cache_k': nrm((DEC_BATCH, L, PAST_LEN, N_KV_A, HD_A)),
        'cache_v': nrm((DEC_BATCH, L, PAST_LEN, N_KV_A, HD_A)),
        'state_gla': nrm((DEC_BATCH, L, 2, N_HEADS_B, DK_B, DV_B), 0.5),
        'state_rwkv': nrm((DEC_BATCH, L, 2, N_HEADS_C, HD_C, HD_C), 0.5),
        'c': nrm((DEC_BATCH, D)),
        'c_ctx': nrm((D,)),
        'w_ada': nrm((L, D, 6 * D), 0.5 * D ** -0.5),
        'b_ada': nrm((L, 6 * D), 0.02),
        'g_mix': 1.0 + nrm((L, D), 0.02),
        'g_ffn': 1.0 + nrm((L, D), 0.02),
        'w_in': nrm((L, D, N_IN), D ** -0.5),
        'q_gain': 1.0 + nrm((L, HD_A), 0.02),
        'k_gain': 1.0 + nrm((L, HD_A), 0.02),
        'gk_w2': nrm((L, 2, GK_RANK, WIDTH_BK), GK_RANK ** -0.5),
        'gk_b': nrm((L, 2, WIDTH_BK), 0.1),
        'gla_gain': 1.0 + nrm((L, DV_B), 0.02),
        'mu_c': uni((L, C_COLS), 0.0, 1.0),
        'w0': uni((L, 2, WIDTH_C), -5.0, -1.0),
        'w2': nrm((L, 2, W_LORA, WIDTH_C), 0.1 * W_LORA ** -0.5),
        'a0': nrm((L, 2, WIDTH_C), 0.1),
        'a2': nrm((L, 2, A_LORA, WIDTH_C), 0.1 * A_LORA ** -0.5),
        'g2': nrm((L, G_LORA, WIDTH_C), G_LORA ** -0.5),
        'k_k': 0.85 + nrm((L, WIDTH_C), 0.02),
        'k_a': 1.0 + nrm((L, WIDTH_C), 0.02),
        'r_k': nrm((L, N_HEADS_C, HD_C), 0.1),
        'lnx_w': 1.0 + nrm((L, WIDTH_C), 0.02),
        'lnx_b': nrm((L, WIDTH_C), 0.02),
        'w_po_a': nrm((L, WIDTH_A, D), WIDTH_A ** -0.5),
        'w_po_b': nrm((L, WIDTH_BV, D), WIDTH_BV ** -0.5),
        'w_po_c': nrm((L, WIDTH_C, D), WIDTH_C ** -0.5),
        'w_out': nrm((L, D, D), D ** -0.5),
        'w_ffn_in': nrm((L, D, 2 * D_FF), D ** -0.5),
        'w_ffn_out': nrm((L, D_FF, D), D_FF ** -0.5),
    }


def reference(x_prompt, x_sample, cache_k, cache_v, state_gla, state_rwkv, c, c_ctx,
              w_ada, b_ada, g_mix, g_ffn, w_in, q_gain, k_gain, gk_w2, gk_b, gla_gain,
              mu_c, w0, w2, a0, a2, g2, k_k, k_a, r_k, lnx_w, lnx_b,
              w_po_a, w_po_b, w_po_c, w_out, w_ffn_in, w_ffn_out):
    y_prompt = x_prompt
    y_sample = x_sample
    cond_ctx = c_ctx[None, :]
    ks, vs, sgs, srs = [], [], [], []
    for li in range(DEPTH):
        lp = (w_ada[li], b_ada[li], g_mix[li], g_ffn[li], w_in[li], q_gain[li], k_gain[li],
              gk_w2[li], gk_b[li], gla_gain[li], mu_c[li], w0[li], w2[li], a0[li], a2[li], g2[li],
              k_k[li], k_a[li], r_k[li], lnx_w[li], lnx_b[li], w_po_a[li], w_po_b[li], w_po_c[li],
              w_out[li], w_ffn_in[li], w_ffn_out[li])
        y_prompt, (k_l, v_l, sg_l, sr_l) = trunk_layer(y_prompt, cond_ctx, lp, None)
        ks.append(k_l)
        vs.append(v_l)
        sgs.append(sg_l)
        srs.append(sr_l)
        y_sample, _ = trunk_layer(y_sample, c, lp,
                                  (cache_k[:, li], cache_v[:, li], state_gla[:, li], state_rwkv[:, li]))
    new_cache_k = jnp.stack(ks, axis=1)
    new_cache_v = jnp.stack(vs, axis=1)
    new_state_gla = jnp.stack(sgs, axis=1)
    new_state_rwkv = jnp.stack(srs, axis=1)
    return (y_prompt, y_sample, new_cache_k, new_cache_v, new_state_gla, new_state_rwkv)
```

```python
import functools
import math

import numpy as np
import jax
import jax.numpy as jnp
from jax import lax
from jax.experimental import pallas as pl
from jax.experimental.pallas import tpu as pltpu

D_MODEL = 1024
BATCH = 16
SEQ = 256
DEPTH = 4
DEC_BATCH = 2
DEC_SEQ = 1024
PAST_LEN = 256
GRID_W = 64
N_HEADS_A = 8
N_KV_A = 2
HD_A = 64
ROPE_THETA = 10000.0
N_HEADS_B = 4
DK_B = 64
DV_B = 128
GK_RANK = 16
GK_NORMALIZER = 16.0
N_HEADS_C = 8
HD_C = 64
W_LORA = 64
A_LORA = 64
G_LORA = 128
GN_EPS = 64e-5
NORM_EPS = 1e-6
WIDTH_A = N_HEADS_A * HD_A
KV_A = N_KV_A * HD_A
WIDTH_BK = N_HEADS_B * DK_B
WIDTH_BV = N_HEADS_B * DV_B
WIDTH_C = N_HEADS_C * HD_C
N_BRANCH = 3
D_FF = ((8 * D_MODEL + 3 * 256 - 1) // (3 * 256)) * 256
OFF_QA = 0
OFF_KA = OFF_QA + WIDTH_A
OFF_VA = OFF_KA + KV_A
OFF_QB = OFF_VA + KV_A
OFF_KB = OFF_QB + WIDTH_BK
OFF_VB = OFF_KB + WIDTH_BK
OFF_GB = OFF_VB + WIDTH_BV
OFF_GKF = OFF_GB + WIDTH_BV
OFF_GKB = OFF_GKF + GK_RANK
OFF_C = OFF_GKB + GK_RANK
C_COLS = 3 * WIDTH_C + 2 * W_LORA + 2 * A_LORA + G_LORA
OFF_GATE = OFF_C + C_COLS
N_IN = OFF_GATE + N_BRANCH * D_MODEL

F32 = jnp.float32
BF16 = jnp.bfloat16

LANES = 128
VMEM_LIMIT = 56 * 1024 * 1024

N_TOK_CTX = BATCH * SEQ
N_TOK_LAT = DEC_BATCH * DEC_SEQ
N_TOK = N_TOK_CTX + N_TOK_LAT
N_COND = 1 + DEC_BATCH
COND_ROWS = 8
SEC_A = OFF_QB
SEC_B_MAIN = OFF_GKF - OFF_QB
SEC_B = SEC_B_MAIN + LANES
SEC_C = C_COLS
SEC_G = N_BRANCH * D_MODEL
TM = 512
TF = D_FF // 2
GLA_CHUNK = 64
RW_CHUNK = 64
HI = lax.Precision.HIGHEST


def _tile_group(i, tm):
    ctx_tiles = N_TOK_CTX // tm
    lat_tiles = DEC_SEQ // tm
    return jnp.where(i < ctx_tiles, 0, 1 + (i - ctx_tiles) // lat_tiles)


def _bdot(a, b):
    return jnp.dot(a.astype(BF16), b.astype(BF16), preferred_element_type=F32)


def _dot_nt(a, b, precision=None):
    return lax.dot_general(a, b, (((1,), (1,)), ((), ())), preferred_element_type=F32, precision=precision)


def _dot_tn(a, b, precision=None):
    return lax.dot_general(a, b, (((0,), (0,)), ((), ())), preferred_element_type=F32, precision=precision)


def _sigmoid(x):
    return jax.nn.sigmoid(x)


def _modnorm(x, gain, shift, scale):
    ms = jnp.mean(x * x, axis=-1, keepdims=True)
    y = x * lax.rsqrt(ms + NORM_EPS) * gain
    return y * (1.0 + scale) + shift


def _mod_kernel(c_ref, w_ref, b_ref, o_ref):
    c = c_ref[...]
    s = c * _sigmoid(c)
    o_ref[...] = _bdot(s, w_ref[...]) + b_ref[...]


def _modulation(cond, w_ada, b_ada):
    tn = 1536
    n_out = 6 * D_MODEL
    return pl.pallas_call(
        _mod_kernel,
        grid=(DEPTH, n_out // tn),
        in_specs=[
            pl.BlockSpec((COND_ROWS, D_MODEL), lambda l, j: (0, 0)),
            pl.BlockSpec((None, D_MODEL, tn), lambda l, j: (l, 0, j)),
            pl.BlockSpec((None, 1, tn), lambda l, j: (l, 0, j)),
        ],
        out_specs=pl.BlockSpec((None, COND_ROWS, tn), lambda l, j: (l, 0, j)),
        out_shape=jax.ShapeDtypeStruct((DEPTH, COND_ROWS, n_out), F32),
        compiler_params=pltpu.CompilerParams(dimension_semantics=("arbitrary", "arbitrary")),
        name="modulation",
    )(cond, w_ada, b_ada.reshape(DEPTH, 1, n_out))


def _inproj_kernel(x_ref, g_ref, mod_ref, wa_ref, wb_ref, wc_ref, wg_ref, oa_ref, ob_ref, oc_ref, og_ref):
    h = _modnorm(x_ref[...], g_ref[...], mod_ref[0], mod_ref[1]).astype(BF16)
    oa_ref[...] = jnp.dot(h, wa_ref[...], preferred_element_type=F32)
    ob_ref[...] = jnp.dot(h, wb_ref[...], preferred_element_type=F32)
    oc_ref[...] = jnp.dot(h, wc_ref[...], preferred_element_type=F32)
    og_ref[...] = jnp.dot(h, wg_ref[...], preferred_element_type=F32)


def _resident(shape):
    nd = len(shape)
    return pl.BlockSpec(shape, lambda *_: (0,) * nd, pipeline_mode=pl.Buffered(1))


def _mod_spec(tm):
    return pl.BlockSpec((None, 6, 1, D_MODEL), lambda i, *_: (_tile_group(i, tm), 0, 0, 0))


def _inproj(x, gain, mod, wa, wb, wc, wg):
    tm = 256
    widths = (SEC_A, SEC_B, SEC_C, SEC_G)
    return pl.pallas_call(
        _inproj_kernel,
        grid=(N_TOK // tm,),
        in_specs=[
            pl.BlockSpec((tm, D_MODEL), lambda i: (i, 0)),
            _resident((1, D_MODEL)),
            _mod_spec(tm),
        ] + [_resident((D_MODEL, w)) for w in widths],
        out_specs=[pl.BlockSpec((tm, w), lambda i: (i, 0)) for w in widths],
        out_shape=[jax.ShapeDtypeStruct((N_TOK, w), F32) for w in widths],
        compiler_params=pltpu.CompilerParams(dimension_semantics=("arbitrary",), vmem_limit_bytes=VMEM_LIMIT),
        name="inproj",
    )(x, gain, mod, wa, wb, wc, wg)


def _segsum(x, width):
    n = x.shape[-1]
    gi = lax.broadcasted_iota(jnp.int32, (n, n), 0) // width
    gj = lax.broadcasted_iota(jnp.int32, (n, n), 1) // width
    e = jnp.where(gi == gj, 1.0, 0.0).astype(F32)
    return jnp.dot(x, e, preferred_element_type=F32, precision=HI)


def _swap_pairs(x):
    lane = lax.broadcasted_iota(jnp.int32, x.shape, 1)
    nxt = pltpu.roll(x, LANES - 1, 1)
    prv = pltpu.roll(x, 1, 1)
    return jnp.where(lane % 2 == 0, nxt, prv)


def _head_norm(x, gain):
    ms = _segsum(x * x, HD_A) * (1.0 / HD_A)
    return x * lax.rsqrt(ms + NORM_EPS) * gain


def _attn_kernel(*refs, t, latent):
    if latent:
        (pa_ref, qg_ref, kg_ref, ck_ref, cv_ref, cos_ref, sin_ref, o_ref, k_scr, v_scr, q_scr) = refs
    else:
        (pa_ref, qg_ref, kg_ref, o_ref, ko_ref, vo_ref, k_scr, v_scr, q_scr) = refs
    tk = k_scr.shape[0]
    past = tk - t
    kn = _head_norm(pa_ref[:, OFF_KA:OFF_VA], kg_ref[...])
    va = pa_ref[:, OFF_VA:OFF_QB]
    if latent:
        cos = cos_ref[...]
        sin = sin_ref[...]
        kn = kn * cos + _swap_pairs(kn) * sin
        k_scr[0:past, :] = ck_ref[...]
        v_scr[0:past, :] = cv_ref[...]
    else:
        ko_ref[...] = kn
        vo_ref[...] = va
    k_scr[past:tk, :] = kn
    v_scr[past:tk, :] = va
    for c in range(WIDTH_A // LANES):
        qn = _head_norm(pa_ref[:, c * LANES:(c + 1) * LANES], qg_ref[...])
        if latent:
            qn = qn * cos + _swap_pairs(qn) * sin
        q_scr[:, c * LANES:(c + 1) * LANES] = qn * (HD_A ** -0.5)

    group = N_HEADS_A // N_KV_A
    tq = min(t, 256)
    for kv in range(N_KV_A):
        kh = k_scr[:, kv * HD_A:(kv + 1) * HD_A].astype(BF16)
        vh = v_scr[:, kv * HD_A:(kv + 1) * HD_A].astype(BF16)
        for qb in range(t // tq):
            outs = []
            for g in range(group):
                h = kv * group + g
                qh = q_scr[qb * tq:(qb + 1) * tq, h * HD_A:(h + 1) * HD_A].astype(BF16)
                s = _dot_nt(qh, kh)
                m = jnp.max(s, axis=-1, keepdims=True)
                e = jnp.exp(s - m)
                p = e / jnp.sum(e, axis=-1, keepdims=True)
                outs.append(jnp.dot(p.astype(BF16), vh, preferred_element_type=F32))
            for j in range(group // 2):
                pair = jnp.concatenate([outs[2 * j], outs[2 * j + 1]], axis=1)
                col = (kv * group + 2 * j) * HD_A
                o_ref[qb * tq:(qb + 1) * tq, col:col + LANES] = pair


def _attention(pa, q_gain, k_gain, n_seq, t, row_block0, cache=None, rope=None):
    latent = cache is not None
    tk = t + (PAST_LEN if latent else 0)
    in_specs = [
        pl.BlockSpec((t, SEC_A), lambda s: (row_block0 + s, 0)),
        _resident((1, LANES)),
        _resident((1, LANES)),
    ]
    args = [pa, q_gain, k_gain]
    out_specs = [pl.BlockSpec((t, WIDTH_A), lambda s: (s, 0))]
    out_shape = [jax.ShapeDtypeStruct((n_seq * t, WIDTH_A), F32)]
    if latent:
        in_specs += [
            pl.BlockSpec((None, PAST_LEN, KV_A), lambda s: (s, 0, 0)),
            pl.BlockSpec((None, PAST_LEN, KV_A), lambda s: (s, 0, 0)),
            _resident((t, LANES)),
            _resident((t, LANES)),
        ]
        args += [cache[0], cache[1], rope[0], rope[1]]
    else:
        out_specs += [pl.BlockSpec((None, t, KV_A), lambda s: (s, 0, 0))] * 2
        out_shape += [jax.ShapeDtypeStruct((n_seq, t, KV_A), F32)] * 2
    return pl.pallas_call(
        functools.partial(_attn_kernel, t=t, latent=latent),
        grid=(n_seq,),
        in_specs=in_specs,
        out_specs=out_specs,
        out_shape=out_shape,
        scratch_shapes=[
            pltpu.VMEM((tk, KV_A), F32),
            pltpu.VMEM((tk, KV_A), F32),
            pltpu.VMEM((t, WIDTH_A), F32),
        ],
        compiler_params=pltpu.CompilerParams(dimension_semantics=("arbitrary",), vmem_limit_bytes=VMEM_LIMIT),
        name="attn_lat" if latent else "attn_ctx",
    )(*args)


def _rope_tables():
    tpos = np.arange(DEC_SEQ)
    row = (tpos // GRID_W).astype(np.float32)
    col = (tpos % GRID_W).astype(np.float32)
    n_pairs = HD_A // 4
    inv = (ROPE_THETA ** (-np.arange(n_pairs, dtype=np.float32) / n_pairs)).astype(np.float32)
    ang = np.concatenate([row[:, None] * inv, col[:, None] * inv], axis=-1)
    cos = np.repeat(np.cos(ang), 2, axis=-1)
    sin = np.repeat(np.sin(ang), 2, axis=-1)
    sign = np.tile(np.array([-1.0, 1.0], np.float32), HD_A // 2)
    reps = LANES // HD_A
    return (jnp.asarray(np.tile(cos, (1, reps)), F32), jnp.asarray(np.tile(sin * sign, (1, reps)), F32))


def _tri(n, reverse, strict):
    i = lax.broadcasted_iota(jnp.int32, (n, n), 0)
    j = lax.broadcasted_iota(jnp.int32, (n, n), 1)
    if reverse:
        keep = (j > i) if strict else (j >= i)
    else:
        keep = (j < i) if strict else (j <= i)
    return keep


def _head_masks(shape):
    lane = lax.broadcasted_iota(jnp.int32, shape, 1)
    first = lane < (LANES // 2)
    return first, jnp.logical_not(first)


def _log_sigmoid(z):
    return jnp.minimum(z, 0.0) - jnp.log1p(jnp.exp(-jnp.abs(z)))


def _gla_kernel(*refs, t, has_init, want_state):
    refs = list(refs)
    q_ref, k_ref, v_ref, g_ref, lr_ref, w2_ref, gb_ref, gain_ref = refs[:8]
    pos = 8
    if has_init:
        s0_ref = refs[pos]
        pos += 1
    o_ref = refs[pos]
    pos += 1
    if want_state:
        so_ref = refs[pos]
        pos += 1
    lg_scr, of_scr, ob_scr, st_scr = refs[pos:pos + 4]

    c = GLA_CHUNK
    n_chunks = t // c
    half = LANES // 2
    jj = lax.broadcasted_iota(jnp.int32, (half, LANES), 0)
    ll = lax.broadcasted_iota(jnp.int32, (half, LANES), 1)
    sel = [jnp.where(ll == jj + h * half, 1.0, 0.0).astype(F32) for h in range(2)]

    lr = lr_ref[...].astype(BF16)
    for d in range(2):
        z = jnp.dot(lr, w2_ref[d].astype(BF16), preferred_element_type=F32) + gb_ref[d]
        lg_scr[d] = _log_sigmoid(z) * (1.0 / GK_NORMALIZER)
        for h in range(2):
            if has_init:
                st_scr[d, h] = _dot_tn(s0_ref[d, h], sel[h], precision=HI)
            else:
                st_scr[d, h] = jnp.zeros((DV_B, LANES), F32)

    m0, m1 = _head_masks((c, LANES))
    hmask = (m0, m1)
    scale = DK_B ** -0.5

    def chunk_step(i, carry):
        for d in range(2):
            reverse = d == 1
            ci = (n_chunks - 1 - i) if reverse else i
            r0 = pl.multiple_of(ci * c, c)
            rows = pl.ds(r0, c)
            q = q_ref[rows, :]
            k = k_ref[rows, :]
            g = lg_scr[d, rows, :]
            cum = jnp.where(_tri(c, reverse, False), 1.0, 0.0).astype(F32)
            b = jnp.dot(cum, g, preferred_element_type=F32, precision=HI)
            bm = b[c // 2:c // 2 + 1, :]
            bl = b[0:1, :] if reverse else b[c - 1:c, :]
            e1 = jnp.exp(b - bm)
            e2 = jnp.exp(bm - b)
            qt = q * scale * e1
            kt = k * e2
            qh = qt * jnp.exp(bm)
            kc = kt * jnp.exp(bl - bm)
            keep = _tri(c, reverse, False)
            dec = jnp.exp(bl)
            for h in range(2):
                vh = v_ref[rows, h * DV_B:(h + 1) * DV_B]
                st = st_scr[d, h]
                att = _dot_nt(jnp.where(hmask[h], qt, 0.0), kt, precision=HI)
                att = jnp.where(keep, att, 0.0)
                o = _bdot(att, vh) + _dot_nt(jnp.where(hmask[h], qh, 0.0).astype(BF16), st.astype(BF16))
                if reverse:
                    ob_scr[h, rows, :] = o
                else:
                    of_scr[h, rows, :] = o
                upd = _dot_tn(vh.astype(BF16), jnp.where(hmask[h], kc, 0.0).astype(BF16))
                st_scr[d, h] = st * dec + upd
        return carry

    lax.fori_loop(0, n_chunks, chunk_step, 0)

    for h in range(2):
        o = of_scr[h] + ob_scr[h]
        ms = jnp.mean(o * o, axis=-1, keepdims=True)
        o = o * lax.rsqrt(ms + NORM_EPS) * gain_ref[...]
        gate = g_ref[:, h * DV_B:(h + 1) * DV_B]
        o_ref[:, h * DV_B:(h + 1) * DV_B] = o * (gate * _sigmoid(gate))
        if want_state:
            for d in range(2):
                so_ref[d, h] = _dot_nt(sel[h], st_scr[d, h], precision=HI)


def _gla(pb, w2pad, gk_b, gain, n_seq, t, row_block0, s0=None, want_state=False):
    has_init = s0 is not None
    pair_w = 2 * DV_B
    state_spec = pl.BlockSpec((None, 2, 2, DK_B, DV_B), lambda s, p: (s, 0, p, 0, 0))
    in_specs = [
        pl.BlockSpec((t, LANES), lambda s, p: (row_block0 + s, p)),
        pl.BlockSpec((t, LANES), lambda s, p: (row_block0 + s, WIDTH_BK // LANES + p)),
        pl.BlockSpec((t, pair_w), lambda s, p: (row_block0 + s, (2 * WIDTH_BK) // pair_w + p)),
        pl.BlockSpec((t, pair_w), lambda s, p: (row_block0 + s, (2 * WIDTH_BK + WIDTH_BV) // pair_w + p)),
        pl.BlockSpec((t, LANES), lambda s, p: (row_block0 + s, SEC_B_MAIN // LANES)),
        pl.BlockSpec((2, LANES, LANES), lambda s, p: (0, 0, p)),
        pl.BlockSpec((2, 1, LANES), lambda s, p: (0, 0, p)),
        pl.BlockSpec((1, DV_B), lambda s, p: (0, 0)),
    ]
    args = [pb, pb, pb, pb, pb, w2pad, gk_b, gain]
    if has_init:
        in_specs.append(state_spec)
        args.append(s0)
    out_specs = [pl.BlockSpec((t, pair_w), lambda s, p: (s, p))]
    out_shape = [jax.ShapeDtypeStruct((n_seq * t, WIDTH_BV), F32)]
    if want_state:
        out_specs.append(state_spec)
        out_shape.append(jax.ShapeDtypeStruct((n_seq, 2, N_HEADS_B, DK_B, DV_B), F32))
    return pl.pallas_call(
        functools.partial(_gla_kernel, t=t, has_init=has_init, want_state=want_state),
        grid=(n_seq, N_HEADS_B // 2),
        in_specs=in_specs,
        out_specs=out_specs,
        out_shape=out_shape,
        scratch_shapes=[
            pltpu.VMEM((2, t, LANES), F32),
            pltpu.VMEM((2, t, DV_B), F32),
            pltpu.VMEM((2, t, DV_B), F32),
            pltpu.VMEM((2, 2, DV_B, LANES), F32),
        ],
        compiler_params=pltpu.CompilerParams(dimension_semantics=("arbitrary", "arbitrary")),
        name="gla_lat" if has_init else "gla_ctx",
    )(*args)


def _shift_mix(u, mu):
    t = u.shape[0]
    row = lax.broadcasted_iota(jnp.int32, u.shape, 0)
    prev = jnp.where(row == 0, 0.0, pltpu.roll(u, 1, 0))
    nxt = jnp.where(row == t - 1, 0.0, pltpu.roll(u, t - 1, 0))
    return u + mu * (0.5 * (prev + nxt) - u)


def _unit_lower_inverse(n_mat):
    c = n_mat.shape[0]
    eye = jnp.where(lax.broadcasted_iota(jnp.int32, (c, c), 0) == lax.broadcasted_iota(jnp.int32, (c, c), 1),
                    1.0, 0.0).astype(F32)
    p = eye + n_mat
    m = n_mat
    for _ in range(int(math.log2(c)) - 1):
        m = jnp.dot(m, m, preferred_element_type=F32, precision=HI)
        p = p + jnp.dot(p, m, preferred_element_type=F32, precision=HI)
    return p


def _rwkv_kernel(*refs, t, has_init, want_state):
    refs = list(refs)
    (r_ref, k_ref, v_ref, wl_ref, al_ref, gl_ref,
     mur_ref, muk_ref, muv_ref, muw_ref, mua_ref, mug_ref,
     w0_ref, w2_ref, a0_ref, a2_ref, g2_ref, kk_ref, ka_ref, rk_ref, lw_ref, lb_ref) = refs[:22]
    pos = 22
    if has_init:
        s0_ref = refs[pos]
        pos += 1
    o_ref = refs[pos]
    pos += 1
    if want_state:
        so_ref = refs[pos]
        pos += 1
    r_scr, v_scr, a_scr, dir_scr, y_scr, st_scr = refs[pos:pos + 6]

    c = RW_CHUNK
    n_chunks = t // c
    half = LANES // 2

    r = _shift_mix(r_ref[...], mur_ref[...])
    kc = _shift_mix(k_ref[...], muk_ref[...])
    vc = _shift_mix(v_ref[...], muv_ref[...])
    wl_in = jnp.tanh(_shift_mix(wl_ref[...], muw_ref[...]))
    al_in = _shift_mix(al_ref[...], mua_ref[...])
    g_in = _sigmoid(_shift_mix(gl_ref[...], mug_ref[...]))

    kkf = kc * kk_ref[...]
    kk = kkf * lax.rsqrt(_segsum(kkf * kkf, HD_C) + 1e-12)
    r_scr[...] = r
    v_scr[...] = vc
    a_scr[...] = -kk
    bonus = jnp.zeros_like(r)
    for d in range(2):
        wl = w0_ref[d] + _bdot(wl_in, w2_ref[d])
        dir_scr[d, 0] = -math.exp(-0.5) * _sigmoid(wl)
        a = _sigmoid(a0_ref[d] + _bdot(al_in, a2_ref[d]))
        kd = kc * (1.0 + (a - 1.0) * ka_ref[...])
        dir_scr[d, 1] = kd
        dir_scr[d, 2] = kk * a
        bonus = bonus + _segsum(r * kd * rk_ref[...], HD_C)
    bonus = bonus * vc
    g_c = _bdot(g_in, g2_ref[...])

    for d in range(2):
        if has_init:
            z = jnp.zeros((half, half), F32)
            top = jnp.concatenate([s0_ref[d, 0], z], axis=1)
            bot = jnp.concatenate([z, s0_ref[d, 1]], axis=1)
            st_scr[d] = jnp.concatenate([top, bot], axis=0)
        else:
            st_scr[d] = jnp.zeros((LANES, LANES), F32)

    m0, m1 = _head_masks((c, LANES))
    hmask = (m0, m1)

    def chunk_step(i, carry):
        for d in range(2):
            reverse = d == 1
            ci = (n_chunks - 1 - i) if reverse else i
            r0 = pl.multiple_of(ci * c, c)
            rows = pl.ds(r0, c)
            rr = r_scr[rows, :]
            vv = v_scr[rows, :]
            aa = a_scr[rows, :]
            lw = dir_scr[d, 0, rows, :]
            kd = dir_scr[d, 1, rows, :]
            bb = dir_scr[d, 2, rows, :]
            st = st_scr[d]
            cum = jnp.where(_tri(c, reverse, False), 1.0, 0.0).astype(F32)
            cs = jnp.dot(cum, lw, preferred_element_type=F32, precision=HI)
            cm = cs[c // 2:c // 2 + 1, :]
            cl = cs[0:1, :] if reverse else cs[c - 1:c, :]
            e1 = jnp.exp(cs - cm)
            e2 = jnp.exp(cm - cs)
            ecm = jnp.exp(cm)
            ecl = jnp.exp(cl - cm)
            rt = rr * e1
            at = aa * e1 * jnp.exp(-lw)
            bt = bb * e2
            kt = kd * e2
            rh = rt * ecm
            ah = at * ecm
            bc = bt * ecl
            kcl = kt * ecl
            strict = _tri(c, reverse, True)
            incl = _tri(c, reverse, False)
            us = []
            ys = []
            for h in range(2):
                at_h = jnp.where(hmask[h], at, 0.0)
                rt_h = jnp.where(hmask[h], rt, 0.0)
                a_ab = jnp.where(strict, _dot_nt(at_h, bt, precision=HI), 0.0)
                a_ak = jnp.where(strict, _dot_nt(at_h, kt, precision=HI), 0.0)
                a_rb = jnp.where(incl, _dot_nt(rt_h, bt, precision=HI), 0.0)
                a_rk = jnp.where(incl, _dot_nt(rt_h, kt, precision=HI), 0.0)
                x = (jnp.dot(a_ak, vv, preferred_element_type=F32, precision=HI)
                     + _dot_nt(jnp.where(hmask[h], ah, 0.0), st, precision=HI))
                u = jnp.dot(_unit_lower_inverse(a_ab), x, preferred_element_type=F32, precision=HI)
                y = (_dot_nt(jnp.where(hmask[h], rh, 0.0), st, precision=HI)
                     + jnp.dot(a_rb, u, preferred_element_type=F32, precision=HI)
                     + jnp.dot(a_rk, vv, preferred_element_type=F32, precision=HI))
                us.append(u)
                ys.append(y)
            u = jnp.where(m0, us[0], us[1])
            y_scr[d, rows, :] = jnp.where(m0, ys[0], ys[1])
            st_scr[d] = (st * jnp.exp(cl) + _dot_tn(u, bc, precision=HI) + _dot_tn(vv, kcl, precision=HI))
        return carry

    lax.fori_loop(0, n_chunks, chunk_step, 0)

    y = y_scr[0] + y_scr[1]
    mean = _segsum(y, HD_C) * (1.0 / HD_C)
    yc = y - mean
    var = _segsum(yc * yc, HD_C) * (1.0 / HD_C)
    gn = yc * lax.rsqrt(var + GN_EPS) * lw_ref[...] + lb_ref[...]
    o_ref[...] = (gn + bonus) * g_c
    if want_state:
        for d in range(2):
            st = st_scr[d]
            so_ref[d, 0] = st[0:half, 0:half]
            so_ref[d, 1] = st[half:LANES, half:LANES]


def _rwkv(pc, mu_c, w0, w2pad, a0, a2pad, g2, k_k, k_a, r_k, lnx_w, lnx_b, n_seq, t, row_block0,
          s0=None, want_state=False):
    has_init = s0 is not None
    n_pairs = WIDTH_C // LANES
    blk_w = 3 * n_pairs
    blk_a = blk_w + 1
    blk_g = blk_w + 2
    state_spec = pl.BlockSpec((None, 2, 2, HD_C, HD_C), lambda s, p: (s, 0, p, 0, 0))

    def sec(col_fn):
        return pl.BlockSpec((t, LANES), lambda s, p: (row_block0 + s, col_fn(p)))

    def vec(col_fn):
        return pl.BlockSpec((1, LANES), lambda s, p: (0, col_fn(p)))

    col_fns = [lambda p: p, lambda p: n_pairs + p, lambda p: 2 * n_pairs + p,
               lambda p: blk_w, lambda p: blk_a, lambda p: blk_g]
    in_specs = [sec(f) for f in col_fns] + [vec(f) for f in col_fns]
    in_specs += [
        pl.BlockSpec((2, 1, LANES), lambda s, p: (0, 0, p)),
        pl.BlockSpec((2, LANES, LANES), lambda s, p: (0, 0, p)),
        pl.BlockSpec((2, 1, LANES), lambda s, p: (0, 0, p)),
        pl.BlockSpec((2, LANES, LANES), lambda s, p: (0, 0, p)),
        pl.BlockSpec((LANES, LANES), lambda s, p: (0, p)),
    ] + [vec(lambda p: p)] * 5
    args = [pc] * 6 + [mu_c] * 6 + [w0, w2pad, a0, a2pad, g2, k_k, k_a, r_k, lnx_w, lnx_b]
    if has_init:
        in_specs.append(state_spec)
        args.append(s0)
    out_specs = [pl.BlockSpec((t, LANES), lambda s, p: (s, p))]
    out_shape = [jax.ShapeDtypeStruct((n_seq * t, WIDTH_C), F32)]
    if want_state:
        out_specs.append(state_spec)
        out_shape.append(jax.ShapeDtypeStruct((n_seq, 2, N_HEADS_C, HD_C, HD_C), F32))
    return pl.pallas_call(
        functools.partial(_rwkv_kernel, t=t, has_init=has_init, want_state=want_state),
        grid=(n_seq, n_pairs),
        in_specs=in_specs,
        out_specs=out_specs,
        out_shape=out_shape,
        scratch_shapes=[
            pltpu.VMEM((t, LANES), F32),
            pltpu.VMEM((t, LANES), F32),
            pltpu.VMEM((t, LANES), F32),
            pltpu.VMEM((2, 3, t, LANES), F32),
            pltpu.VMEM((2, t, LANES), F32),
            pltpu.VMEM((2, LANES, LANES), F32),
        ],
        compiler_params=pltpu.CompilerParams(dimension_semantics=("arbitrary", "arbitrary")),
        name="rwkv_lat" if has_init else "rwkv_ctx",
    )(*args)


def _merge_kernel(x_ref, oa_ref, ob_ref, oc_ref, pg_ref, mod_ref, wa_ref, wb_ref, wc_ref, wo_ref, o_ref):
    merged = jnp.zeros(x_ref.shape, F32)
    for j, (b_ref, w_ref) in enumerate(((oa_ref, wa_ref), (ob_ref, wb_ref), (oc_ref, wc_ref))):
        gate = _sigmoid(pg_ref[:, j * D_MODEL:(j + 1) * D_MODEL])
        merged = merged + gate * jnp.dot(b_ref[...].astype(BF16), w_ref[...], preferred_element_type=F32)
    y = jnp.dot(merged.astype(BF16), wo_ref[...], preferred_element_type=F32)
    o_ref[...] = x_ref[...] + mod_ref[2] * y


def _merge(x, oa, ob, oc, pg, mod, wpa, wpb, wpc, wout):
    tm = TM
    row = lambda w: pl.BlockSpec((tm, w), lambda i: (i, 0))
    return pl.pallas_call(
        _merge_kernel,
        grid=(N_TOK // tm,),
        in_specs=[row(D_MODEL), row(WIDTH_A), row(WIDTH_BV), row(WIDTH_C), row(SEC_G), _mod_spec(tm),
                  _resident((WIDTH_A, D_MODEL)), _resident((WIDTH_BV, D_MODEL)),
                  _resident((WIDTH_C, D_MODEL)), _resident((D_MODEL, D_MODEL))],
        out_specs=row(D_MODEL),
        out_shape=jax.ShapeDtypeStruct((N_TOK, D_MODEL), F32),
        compiler_params=pltpu.CompilerParams(dimension_semantics=("arbitrary",), vmem_limit_bytes=VMEM_LIMIT),
        name="merge",
    )(x, oa, ob, oc, pg, mod, wpa, wpb, wpc, wout)


def _ffn_kernel(x_ref, g_ref, mod_ref, wg_ref, wu_ref, wo_ref, o_ref, h_scr, acc_scr):
    f = pl.program_id(1)

    @pl.when(f == 0)
    def _():
        h_scr[...] = _modnorm(x_ref[...], g_ref[...], mod_ref[3], mod_ref[4]).astype(BF16)
        acc_scr[...] = jnp.zeros_like(acc_scr)

    h = h_scr[...]
    gate = jnp.dot(h, wg_ref[...], preferred_element_type=F32)
    up = jnp.dot(h, wu_ref[...], preferred_element_type=F32)
    act = (gate * _sigmoid(gate) * up).astype(BF16)
    acc_scr[...] += jnp.dot(act, wo_ref[...], preferred_element_type=F32)

    @pl.when(f == pl.num_programs(1) - 1)
    def _():
        o_ref[...] = x_ref[...] + mod_ref[5] * acc_scr[...]


def _ffn(x, gain, mod, w_in, w_out):
    tm = TM
    nf = D_FF // TF
    return pl.pallas_call(
        _ffn_kernel,
        grid=(N_TOK // tm, nf),
        in_specs=[
            pl.BlockSpec((tm, D_MODEL), lambda i, f: (i, 0)),
            pl.BlockSpec((1, D_MODEL), lambda i, f: (0, 0)),
            _mod_spec(tm),
            pl.BlockSpec((D_MODEL, TF), lambda i, f: (0, f)),
            pl.BlockSpec((D_MODEL, TF), lambda i, f: (0, nf + f)),
            pl.BlockSpec((TF, D_MODEL), lambda i, f: (f, 0)),
        ],
        out_specs=pl.BlockSpec((tm, D_MODEL), lambda i, f: (i, 0)),
        out_shape=jax.ShapeDtypeStruct((N_TOK, D_MODEL), F32),
        scratch_shapes=[pltpu.VMEM((tm, D_MODEL), BF16), pltpu.VMEM((tm, D_MODEL), F32)],
        compiler_params=pltpu.CompilerParams(dimension_semantics=("arbitrary", "arbitrary"),
                                             vmem_limit_bytes=VMEM_LIMIT),
        name="ffn",
    )(x, gain, mod, w_in, w_in, w_out)


def _pad_rows(w, offset, total):
    pads = [(0, 0)] * w.ndim
    pads[-2] = (offset, total - offset - w.shape[-2])
    return jnp.pad(w, pads)


def kernel(x_prompt, x_sample, cache_k, cache_v, state_gla, state_rwkv, c, c_ctx, w_ada, b_ada, g_mix, g_ffn,
           w_in, q_gain, k_gain, gk_w2, gk_b, gla_gain, mu_c, w0, w2, a0, a2, g2, k_k, k_a, r_k, lnx_w, lnx_b,
           w_po_a, w_po_b, w_po_c, w_out, w_ffn_in, w_ffn_out):
    x = jnp.concatenate([x_prompt.reshape(N_TOK_CTX, D_MODEL), x_sample.reshape(N_TOK_LAT, D_MODEL)], axis=0)
    cond = jnp.concatenate([c_ctx[None, :], c, jnp.zeros((COND_ROWS - N_COND, D_MODEL), F32)], axis=0)
    mod = _modulation(cond, w_ada, b_ada)[:, :N_COND].reshape(DEPTH, N_COND, 6, 1, D_MODEL)

    wa = w_in[:, :, OFF_QA:OFF_QB].astype(BF16)
    wb = jnp.pad(w_in[:, :, OFF_QB:OFF_C], ((0, 0), (0, 0), (0, SEC_B - (OFF_C - OFF_QB)))).astype(BF16)
    wc = w_in[:, :, OFF_C:OFF_GATE].astype(BF16)
    wg = w_in[:, :, OFF_GATE:].astype(BF16)
    gk_w2p = jnp.stack([_pad_rows(gk_w2[:, 0], 0, LANES), _pad_rows(gk_w2[:, 1], GK_RANK, LANES)], axis=1)
    w2p = jnp.stack([_pad_rows(w2[:, 0], 0, LANES), _pad_rows(w2[:, 1], W_LORA, LANES)], axis=1)
    a2p = jnp.stack([_pad_rows(a2[:, 0], 0, LANES), _pad_rows(a2[:, 1], A_LORA, LANES)], axis=1)
    wpa, wpb, wpc, wout = (w.astype(BF16) for w in (w_po_a, w_po_b, w_po_c, w_out))
    wfi = w_ffn_in.astype(BF16)
    wfo = w_ffn_out.astype(BF16)
    rope = _rope_tables()
    lat_blk = N_TOK_CTX // DEC_SEQ

    ks, vs, sgs, srs = [], [], [], []
    for li in range(DEPTH):
        vec = lambda a: a[li].reshape(1, -1)
        pa, pb, pc, pg = _inproj(x, vec(g_mix), mod[li], wa[li], wb[li], wc[li], wg[li])

        qg = jnp.tile(vec(q_gain), (1, LANES // HD_A))
        kg = jnp.tile(vec(k_gain), (1, LANES // HD_A))
        oa_c, k_l, v_l = _attention(pa, qg, kg, BATCH, SEQ, 0)
        (oa_l,) = _attention(pa, qg, kg, DEC_BATCH, DEC_SEQ, lat_blk,
                             cache=(cache_k[:, li].reshape(DEC_BATCH, PAST_LEN, KV_A),
                                    cache_v[:, li].reshape(DEC_BATCH, PAST_LEN, KV_A)), rope=rope)

        gkb = gk_b[li].reshape(2, 1, WIDTH_BK)
        ob_c, sg_l = _gla(pb, gk_w2p[li], gkb, vec(gla_gain), BATCH, SEQ, 0, want_state=True)
        (ob_l,) = _gla(pb, gk_w2p[li], gkb, vec(gla_gain), DEC_BATCH, DEC_SEQ, lat_blk, s0=state_gla[:, li])

        rw_args = (pc, vec(mu_c), w0[li].reshape(2, 1, WIDTH_C), w2p[li], a0[li].reshape(2, 1, WIDTH_C), a2p[li],
                   g2[li], vec(k_k), vec(k_a), r_k[li].reshape(1, WIDTH_C), vec(lnx_w), vec(lnx_b))
        oc_c, sr_l = _rwkv(*rw_args, BATCH, SEQ, 0, want_state=True)
        (oc_l,) = _rwkv(*rw_args, DEC_BATCH, DEC_SEQ, lat_blk, s0=state_rwkv[:, li])

        oa = jnp.concatenate([oa_c, oa_l], axis=0)
        ob = jnp.concatenate([ob_c, ob_l], axis=0)
        oc = jnp.concatenate([oc_c, oc_l], axis=0)
        x = _merge(x, oa, ob, oc, pg, mod[li], wpa[li], wpb[li], wpc[li], wout[li])
        x = _ffn(x, vec(g_ffn), mod[li], wfi[li], wfo[li])

        ks.append(k_l.reshape(BATCH, SEQ, N_KV_A, HD_A))
        vs.append(v_l.reshape(BATCH, SEQ, N_KV_A, HD_A))
        sgs.append(sg_l)
        srs.append(sr_l)

    y_prompt = x[:N_TOK_CTX].reshape(BATCH, SEQ, D_MODEL)
    y_sample = x[N_TOK_CTX:].reshape(DEC_BATCH, DEC_SEQ, D_MODEL)
    return (y_prompt, y_sample, jnp.stack(ks, axis=1), jnp.stack(vs, axis=1),
            jnp.stack(sgs, axis=1), jnp.stack(srs, axis=1))
```

```python
import functools
import math

import numpy as np
import jax
import jax.numpy as jnp
from jax import lax
from jax.experimental import pallas as pl
from jax.experimental.pallas import tpu as pltpu

D_MODEL = 1024
BATCH = 16
SEQ = 256
DEPTH = 4
DEC_BATCH = 2
DEC_SEQ = 1024
PAST_LEN = 256
GRID_W = 64
N_HEADS_A = 8
N_KV_A = 2
HD_A = 64
ROPE_THETA = 10000.0
N_HEADS_B = 4
DK_B = 64
DV_B = 128
GK_RANK = 16
GK_NORMALIZER = 16.0
N_HEADS_C = 8
HD_C = 64
W_LORA = 64
A_LORA = 64
G_LORA = 128
GN_EPS = 64e-5
NORM_EPS = 1e-6
WIDTH_A = N_HEADS_A * HD_A
KV_A = N_KV_A * HD_A
WIDTH_BK = N_HEADS_B * DK_B
WIDTH_BV = N_HEADS_B * DV_B
WIDTH_C = N_HEADS_C * HD_C
N_BRANCH = 3
D_FF = ((8 * D_MODEL + 3 * 256 - 1) // (3 * 256)) * 256
OFF_QA = 0
OFF_KA = OFF_QA + WIDTH_A
OFF_VA = OFF_KA + KV_A
OFF_QB = OFF_VA + KV_A
OFF_KB = OFF_QB + WIDTH_BK
OFF_VB = OFF_KB + WIDTH_BK
OFF_GB = OFF_VB + WIDTH_BV
OFF_GKF = OFF_GB + WIDTH_BV
OFF_GKB = OFF_GKF + GK_RANK
OFF_C = OFF_GKB + GK_RANK
C_COLS = 3 * WIDTH_C + 2 * W_LORA + 2 * A_LORA + G_LORA
OFF_GATE = OFF_C + C_COLS
N_IN = OFF_GATE + N_BRANCH * D_MODEL

F32 = jnp.float32
BF16 = jnp.bfloat16

LANES = 128
VMEM_LIMIT = 56 * 1024 * 1024

N_TOK_CTX = BATCH * SEQ
N_TOK_LAT = DEC_BATCH * DEC_SEQ
N_TOK = N_TOK_CTX + N_TOK_LAT
N_COND = 1 + DEC_BATCH
COND_ROWS = 8
SEC_A = OFF_QB
SEC_B_MAIN = OFF_GKF - OFF_QB
SEC_B = SEC_B_MAIN + LANES
SEC_C = C_COLS
SEC_G = N_BRANCH * D_MODEL
TM = 512
TF = D_FF // 2
GLA_CHUNK = 64
RW_CHUNK = 64
HI = lax.Precision.HIGHEST


def _tile_group(i, tm):
    ctx_tiles = N_TOK_CTX // tm
    lat_tiles = DEC_SEQ // tm
    return jnp.where(i < ctx_tiles, 0, 1 + (i - ctx_tiles) // lat_tiles)


def _bdot(a, b):
    return jnp.dot(a.astype(BF16), b.astype(BF16), preferred_element_type=F32)


def _dot_nt(a, b, precision=None):
    return lax.dot_general(a, b, (((1,), (1,)), ((), ())), preferred_element_type=F32, precision=precision)


def _dot_tn(a, b, precision=None):
    return lax.dot_general(a, b, (((0,), (0,)), ((), ())), preferred_element_type=F32, precision=precision)


def _sigmoid(x):
    return jax.nn.sigmoid(x)


def _modnorm(x, gain, shift, scale):
    ms = jnp.mean(x * x, axis=-1, keepdims=True)
    y = x * lax.rsqrt(ms + NORM_EPS) * gain
    return y * (1.0 + scale) + shift


def _mod_kernel(c_ref, w_ref, b_ref, o_ref):
    c = c_ref[...]
    s = c * _sigmoid(c)
    o_ref[...] = _bdot(s, w_ref[...]) + b_ref[...]


def _modulation(cond, w_ada, b_ada):
    tn = 1536
    n_out = 6 * D_MODEL
    return pl.pallas_call(
        _mod_kernel,
        grid=(DEPTH, n_out // tn),
        in_specs=[
            pl.BlockSpec((COND_ROWS, D_MODEL), lambda l, j: (0, 0)),
            pl.BlockSpec((None, D_MODEL, tn), lambda l, j: (l, 0, j)),
            pl.BlockSpec((None, 1, tn), lambda l, j: (l, 0, j)),
        ],
        out_specs=pl.BlockSpec((None, COND_ROWS, tn), lambda l, j: (l, 0, j)),
        out_shape=jax.ShapeDtypeStruct((DEPTH, COND_ROWS, n_out), F32),
        compiler_params=pltpu.CompilerParams(dimension_semantics=("arbitrary", "arbitrary")),
        name="modulation",
    )(cond, w_ada, b_ada.reshape(DEPTH, 1, n_out))


def _inproj_kernel(x_ref, g_ref, mod_ref, wa_ref, wb_ref, wc_ref, wg_ref, oa_ref, ob_ref, oc_ref, og_ref):
    h = _modnorm(x_ref[...], g_ref[...], mod_ref[0], mod_ref[1]).astype(BF16)
    oa_ref[...] = jnp.dot(h, wa_ref[...], preferred_element_type=F32)
    ob_ref[...] = jnp.dot(h, wb_ref[...], preferred_element_type=F32)
    oc_ref[...] = jnp.dot(h, wc_ref[...], preferred_element_type=F32)
    og_ref[...] = jnp.dot(h, wg_ref[...], preferred_element_type=F32)


def _resident(shape):
    nd = len(shape)
    return pl.BlockSpec(shape, lambda *_: (0,) * nd, pipeline_mode=pl.Buffered(1))


def _mod_spec(tm):
    return pl.BlockSpec((None, 6, 1, D_MODEL), lambda i, *_: (_tile_group(i, tm), 0, 0, 0))


def _inproj(x, gain, mod, wa, wb, wc, wg):
    tm = 256
    widths = (SEC_A, SEC_B, SEC_C, SEC_G)
    return pl.pallas_call(
        _inproj_kernel,
        grid=(N_TOK // tm,),
        in_specs=[
            pl.BlockSpec((tm, D_MODEL), lambda i: (i, 0)),
            _resident((1, D_MODEL)),
            _mod_spec(tm),
        ] + [_resident((D_MODEL, w)) for w in widths],
        out_specs=[pl.BlockSpec((tm, w), lambda i: (i, 0)) for w in widths],
        out_shape=[jax.ShapeDtypeStruct((N_TOK, w), F32) for w in widths],
        compiler_params=pltpu.CompilerParams(dimension_semantics=("arbitrary",), vmem_limit_bytes=VMEM_LIMIT),
        name="inproj",
    )(x, gain, mod, wa, wb, wc, wg)


def _segsum(x, width):
    n = x.shape[-1]
    gi = lax.broadcasted_iota(jnp.int32, (n, n), 0) // width
    gj = lax.broadcasted_iota(jnp.int32, (n, n), 1) // width
    e = jnp.where(gi == gj, 1.0, 0.0).astype(F32)
    return jnp.dot(x, e, preferred_element_type=F32, precision=HI)


def _swap_pairs(x):
    lane = lax.broadcasted_iota(jnp.int32, x.shape, 1)
    nxt = pltpu.roll(x, LANES - 1, 1)
    prv = pltpu.roll(x, 1, 1)
    return jnp.where(lane % 2 == 0, nxt, prv)


def _head_norm(x, gain):
    ms = _segsum(x * x, HD_A) * (1.0 / HD_A)
    return x * lax.rsqrt(ms + NORM_EPS) * gain


def _attn_kernel(*refs, t, latent):
    if latent:
        (pa_ref, qg_ref, kg_ref, ck_ref, cv_ref, cos_ref, sin_ref, o_ref, k_scr, v_scr, q_scr) = refs
    else:
        (pa_ref, qg_ref, kg_ref, o_ref, ko_ref, vo_ref, k_scr, v_scr, q_scr) = refs
    tk = k_scr.shape[0]
    past = tk - t
    kn = _head_norm(pa_ref[:, OFF_KA:OFF_VA], kg_ref[...])
    va = pa_ref[:, OFF_VA:OFF_QB]
    if latent:
        cos = cos_ref[...]
        sin = sin_ref[...]
        kn = kn * cos + _swap_pairs(kn) * sin
        k_scr[0:past, :] = ck_ref[...]
        v_scr[0:past, :] = cv_ref[...]
    else:
        ko_ref[...] = kn
        vo_ref[...] = va
    k_scr[past:tk, :] = kn
    v_scr[past:tk, :] = va
    for c in range(WIDTH_A // LANES):
        qn = _head_norm(pa_ref[:, c * LANES:(c + 1) * LANES], qg_ref[...])
        if latent:
            qn = qn * cos + _swap_pairs(qn) * sin
        q_scr[:, c * LANES:(c + 1) * LANES] = qn * (HD_A ** -0.5)

    group = N_HEADS_A // N_KV_A
    tq = min(t, 256)
    for kv in range(N_KV_A):
        kh = k_scr[:, kv * HD_A:(kv + 1) * HD_A].astype(BF16)
        vh = v_scr[:, kv * HD_A:(kv + 1) * HD_A].astype(BF16)
        for qb in range(t // tq):
            outs = []
            for g in range(group):
                h = kv * group + g
                qh = q_scr[qb * tq:(qb + 1) * tq, h * HD_A:(h + 1) * HD_A].astype(BF16)
                s = _dot_nt(qh, kh)
                m = jnp.max(s, axis=-1, keepdims=True)
                e = jnp.exp(s - m)
                p = e / jnp.sum(e, axis=-1, keepdims=True)
                outs.append(jnp.dot(p.astype(BF16), vh, preferred_element_type=F32))
            for j in range(group // 2):
                pair = jnp.concatenate([outs[2 * j], outs[2 * j + 1]], axis=1)
                col = (kv * group + 2 * j) * HD_A
                o_ref[qb * tq:(qb + 1) * tq, col:col + LANES] = pair


def _attention(pa, q_gain, k_gain, n_seq, t, row_block0, cache=None, rope=None):
    latent = cache is not None
    tk = t + (PAST_LEN if latent else 0)
    in_specs = [
        pl.BlockSpec((t, SEC_A), lambda s: (row_block0 + s, 0)),
        _resident((1, LANES)),
        _resident((1, LANES)),
    ]
    args = [pa, q_gain, k_gain]
    out_specs = [pl.BlockSpec((t, WIDTH_A), lambda s: (s, 0))]
    out_shape = [jax.ShapeDtypeStruct((n_seq * t, WIDTH_A), F32)]
    if latent:
        in_specs += [
            pl.BlockSpec((None, PAST_LEN, KV_A), lambda s: (s, 0, 0)),
            pl.BlockSpec((None, PAST_LEN, KV_A), lambda s: (s, 0, 0)),
            _resident((t, LANES)),
            _resident((t, LANES)),
        ]
        args += [cache[0], cache[1], rope[0], rope[1]]
    else:
        out_specs += [pl.BlockSpec((None, t, KV_A), lambda s: (s, 0, 0))] * 2
        out_shape += [jax.ShapeDtypeStruct((n_seq, t, KV_A), F32)] * 2
    return pl.pallas_call(
        functools.partial(_attn_kernel, t=t, latent=latent),
        grid=(n_seq,),
        in_specs=in_specs,
        out_specs=out_specs,
        out_shape=out_shape,
        scratch_shapes=[
            pltpu.VMEM((tk, KV_A), F32),
            pltpu.VMEM((tk, KV_A), F32),
            pltpu.VMEM((t, WIDTH_A), F32),
        ],
        compiler_params=pltpu.CompilerParams(dimension_semantics=("arbitrary",), vmem_limit_bytes=VMEM_LIMIT),
        name="attn_lat" if latent else "attn_ctx",
    )(*args)


def _rope_tables():
    tpos = np.arange(DEC_SEQ)
    row = (tpos // GRID_W).astype(np.float32)
    col = (tpos % GRID_W).astype(np.float32)
    n_pairs = HD_A // 4
    inv = (ROPE_THETA ** (-np.arange(n_pairs, dtype=np.float32) / n_pairs)).astype(np.float32)
    ang = np.concatenate([row[:, None] * inv, col[:, None] * inv], axis=-1)
    cos = np.repeat(np.cos(ang), 2, axis=-1)
    sin = np.repeat(np.sin(ang), 2, axis=-1)
    sign = np.tile(np.array([-1.0, 1.0], np.float32), HD_A // 2)
    reps = LANES // HD_A
    return (jnp.asarray(np.tile(cos, (1, reps)), F32), jnp.asarray(np.tile(sin * sign, (1, reps)), F32))


def _tri(n, reverse, strict):
    i = lax.broadcasted_iota(jnp.int32, (n, n), 0)
    j = lax.broadcasted_iota(jnp.int32, (n, n), 1)
    if reverse:
        keep = (j > i) if strict else (j >= i)
    else:
        keep = (j < i) if strict else (j <= i)
    return keep


def _head_masks(shape):
    lane = lax.broadcasted_iota(jnp.int32, shape, 1)
    first = lane < (LANES // 2)
    return first, jnp.logical_not(first)


def _log_sigmoid(z):
    return jnp.minimum(z, 0.0) - jnp.log1p(jnp.exp(-jnp.abs(z)))


def _gla_kernel(*refs, t, has_init, want_state):
    refs = list(refs)
    q_ref, k_ref, v_ref, g_ref, lr_ref, w2_ref, gb_ref, gain_ref = refs[:8]
    pos = 8
    if has_init:
        s0_ref = refs[pos]
        pos += 1
    o_ref = refs[pos]
    pos += 1
    if want_state:
        so_ref = refs[pos]
        pos += 1
    lg_scr, of_scr, ob_scr, st_scr = refs[pos:pos + 4]

    c = GLA_CHUNK
    n_chunks = t // c
    half = LANES // 2
    jj = lax.broadcasted_iota(jnp.int32, (half, LANES), 0)
    ll = lax.broadcasted_iota(jnp.int32, (half, LANES), 1)
    sel = [jnp.where(ll == jj + h * half, 1.0, 0.0).astype(F32) for h in range(2)]

    lr = lr_ref[...].astype(BF16)
    for d in range(2):
        z = jnp.dot(lr, w2_ref[d].astype(BF16), preferred_element_type=F32) + gb_ref[d]
        lg_scr[d] = _log_sigmoid(z) * (1.0 / GK_NORMALIZER)
        for h in range(2):
            if has_init:
                st_scr[d, h] = _dot_tn(s0_ref[d, h], sel[h], precision=HI)
            else:
                st_scr[d, h] = jnp.zeros((DV_B, LANES), F32)

    m0, m1 = _head_masks((c, LANES))
    hmask = (m0, m1)
    scale = DK_B ** -0.5

    def chunk_step(i, carry):
        for d in range(2):
            reverse = d == 1
            ci = (n_chunks - 1 - i) if reverse else i
            r0 = pl.multiple_of(ci * c, c)
            rows = pl.ds(r0, c)
            q = q_ref[rows, :]
            k = k_ref[rows, :]
            g = lg_scr[d, rows, :]
            cum = jnp.where(_tri(c, reverse, False), 1.0, 0.0).astype(F32)
            b = jnp.dot(cum, g, preferred_element_type=F32, precision=HI)
            bm = b[c // 2:c // 2 + 1, :]
            bl = b[0:1, :] if reverse else b[c - 1:c, :]
            e1 = jnp.exp(b - bm)
            e2 = jnp.exp(bm - b)
            qt = q * scale * e1
            kt = k * e2
            qh = qt * jnp.exp(bm)
            kc = kt * jnp.exp(bl - bm)
            keep = _tri(c, reverse, False)
            dec = jnp.exp(bl)
            for h in range(2):
                vh = v_ref[rows, h * DV_B:(h + 1) * DV_B]
                st = st_scr[d, h]
                att = _dot_nt(jnp.where(hmask[h], qt, 0.0), kt, precision=HI)
                att = jnp.where(keep, att, 0.0)
                o = _bdot(att, vh) + _dot_nt(jnp.where(hmask[h], qh, 0.0).astype(BF16), st.astype(BF16))
                if reverse:
                    ob_scr[h, rows, :] = o
                else:
                    of_scr[h, rows, :] = o
                upd = _dot_tn(vh.astype(BF16), jnp.where(hmask[h], kc, 0.0).astype(BF16))
                st_scr[d, h] = st * dec + upd
        return carry

    lax.fori_loop(0, n_chunks, chunk_step, 0)

    for h in range(2):
        o = of_scr[h] + ob_scr[h]
        ms = jnp.mean(o * o, axis=-1, keepdims=True)
        o = o * lax.rsqrt(ms + NORM_EPS) * gain_ref[...]
        gate = g_ref[:, h * DV_B:(h + 1) * DV_B]
        o_ref[:, h * DV_B:(h + 1) * DV_B] = o * (gate * _sigmoid(gate))
        if want_state:
            for d in range(2):
                so_ref[d, h] = _dot_nt(sel[h], st_scr[d, h], precision=HI)


def _gla(pb, w2pad, gk_b, gain, n_seq, t, row_block0, s0=None, want_state=False):
    has_init = s0 is not None
    pair_w = 2 * DV_B
    state_spec = pl.BlockSpec((None, 2, 2, DK_B, DV_B), lambda s, p: (s, 0, p, 0, 0))
    in_specs = [
        pl.BlockSpec((t, LANES), lambda s, p: (row_block0 + s, p)),
        pl.BlockSpec((t, LANES), lambda s, p: (row_block0 + s, WIDTH_BK // LANES + p)),
        pl.BlockSpec((t, pair_w), lambda s, p: (row_block0 + s, (2 * WIDTH_BK) // pair_w + p)),
        pl.BlockSpec((t, pair_w), lambda s, p: (row_block0 + s, (2 * WIDTH_BK + WIDTH_BV) // pair_w + p)),
        pl.BlockSpec((t, LANES), lambda s, p: (row_block0 + s, SEC_B_MAIN // LANES)),
        pl.BlockSpec((2, LANES, LANES), lambda s, p: (0, 0, p)),
        pl.BlockSpec((2, 1, LANES), lambda s, p: (0, 0, p)),
        pl.BlockSpec((1, DV_B), lambda s, p: (0, 0)),
    ]
    args = [pb, pb, pb, pb, pb, w2pad, gk_b, gain]
    if has_init:
        in_specs.append(state_spec)
        args.append(s0)
    out_specs = [pl.BlockSpec((t, pair_w), lambda s, p: (s, p))]
    out_shape = [jax.ShapeDtypeStruct((n_seq * t, WIDTH_BV), F32)]
    if want_state:
        out_specs.append(state_spec)
        out_shape.append(jax.ShapeDtypeStruct((n_seq, 2, N_HEADS_B, DK_B, DV_B), F32))
    return pl.pallas_call(
        functools.partial(_gla_kernel, t=t, has_init=has_init, want_state=want_state),
        grid=(n_seq, N_HEADS_B // 2),
        in_specs=in_specs,
        out_specs=out_specs,
        out_shape=out_shape,
        scratch_shapes=[
            pltpu.VMEM((2, t, LANES), F32),
            pltpu.VMEM((2, t, DV_B), F32),
            pltpu.VMEM((2, t, DV_B), F32),
            pltpu.VMEM((2, 2, DV_B, LANES), F32),
        ],
        compiler_params=pltpu.CompilerParams(dimension_semantics=("arbitrary", "arbitrary")),
        name="gla_lat" if has_init else "gla_ctx",
    )(*args)


def _shift_mix(u, mu):
    t = u.shape[0]
    row = lax.broadcasted_iota(jnp.int32, u.shape, 0)
    prev = jnp.where(row == 0, 0.0, pltpu.roll(u, 1, 0))
    nxt = jnp.where(row == t - 1, 0.0, pltpu.roll(u, t - 1, 0))
    return u + mu * (0.5 * (prev + nxt) - u)


_DIMS = {"nn": (((1,), (0,)), ((), ())), "nt": (((1,), (1,)), ((), ())), "tn": (((0,), (0,)), ((), ()))}


def _split_bf16(x):
    hi = x.astype(BF16)
    lo = (x - hi.astype(F32)).astype(BF16)
    return hi, lo


def _mm(a, b, form, passes):
    dims = _DIMS[form]
    if passes == 6:
        return lax.dot_general(a, b, dims, preferred_element_type=F32, precision=HI)
    if passes == 1:
        return lax.dot_general(a.astype(BF16), b.astype(BF16), dims, preferred_element_type=F32)
    a_hi, a_lo = _split_bf16(a)
    b_hi, b_lo = _split_bf16(b)
    ka = dims[0][0][0]
    kb = dims[0][1][0]
    a_cat = jnp.concatenate([a_hi, a_lo, a_hi], axis=ka)
    b_cat = jnp.concatenate([b_hi, b_hi, b_lo], axis=kb)
    return lax.dot_general(a_cat, b_cat, dims, preferred_element_type=F32)


RW_P_GRAM = 1
RW_P_INV = 3
RW_P_INVP = 3
RW_P_APPLY = 1
RW_P_STATE = 1


RW_UNROLL = 2


def _unit_lower_inverse(n_mat, eye, c):
    p = eye + n_mat
    m = n_mat
    for _ in range(int(math.log2(c)) - 1):
        m = _mm(m, m, "nn", RW_P_INV)
        p = p + _mm(p, m, "nn", RW_P_INVP)
    return p


def _rwkv_kernel(*refs, t, has_init, want_state):
    refs = list(refs)
    (r_ref, k_ref, v_ref, wl_ref, al_ref, gl_ref,
     mur_ref, muk_ref, muv_ref, muw_ref, mua_ref, mug_ref,
     w0_ref, w2_ref, a0_ref, a2_ref, g2_ref, kk_ref, ka_ref, rk_ref, lw_ref, lb_ref,
     cum_ref, strict_ref, incl_ref, blk_ref, eye_ref) = refs[:27]
    pos = 27
    if has_init:
        s0_ref = refs[pos]
        pos += 1
    o_ref = refs[pos]
    pos += 1
    if want_state:
        so_ref = refs[pos]
        pos += 1
    r_scr, v_scr, a_scr, dir_scr, y_scr, rp_scr, m_scr, n_scr, dec_scr, st_scr = refs[pos:pos + 10]

    c = RW_CHUNK
    n_chunks = t // c
    half = LANES // 2

    r = _shift_mix(r_ref[...], mur_ref[...])
    kc = _shift_mix(k_ref[...], muk_ref[...])
    vc = _shift_mix(v_ref[...], muv_ref[...])
    wl_in = jnp.tanh(_shift_mix(wl_ref[...], muw_ref[...]))
    al_in = _shift_mix(al_ref[...], mua_ref[...])
    g_in = _sigmoid(_shift_mix(gl_ref[...], mug_ref[...]))

    kkf = kc * kk_ref[...]
    kk = kkf * lax.rsqrt(_segsum(kkf * kkf, HD_C) + 1e-12)
    r_scr[...] = r
    v_scr[...] = vc
    a_scr[...] = -kk
    bonus = jnp.zeros_like(r)
    for d in range(2):
        wl = w0_ref[d] + _bdot(wl_in, w2_ref[d])
        dir_scr[d, 0] = -math.exp(-0.5) * _sigmoid(wl)
        a = _sigmoid(a0_ref[d] + _bdot(al_in, a2_ref[d]))
        kd = kc * (1.0 + (a - 1.0) * ka_ref[...])
        dir_scr[d, 1] = kd
        dir_scr[d, 2] = kk * a
        bonus = bonus + _segsum(r * kd * rk_ref[...], HD_C)
    bonus = bonus * vc
    g_c = _bdot(g_in, g2_ref[...])

    nblk = 2 * (LANES // HD_C)
    sw = nblk * HD_C
    if has_init:
        z = jnp.zeros((HD_C, HD_C), F32)
        blocks = [s0_ref[0, 0], s0_ref[0, 1], s0_ref[1, 0], s0_ref[1, 1]]
        st_scr[...] = jnp.concatenate(
            [jnp.concatenate([blocks[i] if i == j else z for j in range(nblk)], axis=1) for i in range(nblk)], axis=0)
    else:
        st_scr[...] = jnp.zeros((sw, sw), F32)

    m0, m1 = _head_masks((c, LANES))

    def per_head(x):
        xf, xb = x[0:c], x[c:2 * c]
        return jnp.concatenate([jnp.where(m0, xf, 0.0), jnp.where(m1, xf, 0.0),
                                jnp.where(m0, xb, 0.0), jnp.where(m1, xb, 0.0)], axis=0)

    def per_block(x):
        xf, xb = x[0:c], x[c:2 * c]
        return jnp.concatenate([xf, xf, xb, xb], axis=0)

    def side_by_side(x):
        return jnp.concatenate([x[0:c], x[c:2 * c]], axis=1)

    def heads_select(x):
        return jnp.concatenate([jnp.where(m0, x[0:c], x[c:2 * c]), jnp.where(m0, x[2 * c:3 * c], x[3 * c:4 * c])],
                               axis=1)

    def heads_add(x):
        return jnp.concatenate([x[0:c] + x[c:2 * c], x[2 * c:3 * c] + x[3 * c:4 * c]], axis=1)

    def chunk_terms(i, carry):
        rows_f = pl.ds(pl.multiple_of(i * c, c), c)
        rows_b = pl.ds(pl.multiple_of((n_chunks - 1 - i) * c, c), c)

        def both(ref_f, ref_b):
            return jnp.concatenate([ref_f[rows_f, :], ref_b[rows_b, :]], axis=0)

        rr = both(r_scr, r_scr)
        vv = both(v_scr, v_scr)
        aa = both(a_scr, a_scr)
        lw = both(dir_scr.at[0, 0], dir_scr.at[1, 0])
        kd = both(dir_scr.at[0, 1], dir_scr.at[1, 1])
        bb = both(dir_scr.at[0, 2], dir_scr.at[1, 2])
        cs = jnp.dot(cum_ref[...], lw, preferred_element_type=F32, precision=HI)
        mid = lambda x: jnp.concatenate([jnp.broadcast_to(x[c // 2:c // 2 + 1], (c, LANES)),
                                         jnp.broadcast_to(x[c + c // 2:c + c // 2 + 1], (c, LANES))], axis=0)
        last = lambda x: jnp.concatenate([jnp.broadcast_to(x[c - 1:c], (c, LANES)),
                                          jnp.broadcast_to(x[c:c + 1], (c, LANES))], axis=0)
        cm = mid(cs)
        cl = last(cs)
        e1 = jnp.exp(cs - cm)
        e2 = jnp.exp(cm - cs)
        ecm = jnp.exp(cm)
        ecl = jnp.exp(cl - cm)
        rt = rr * e1
        at = aa * e1 * jnp.exp(-lw)
        bt = bb * e2
        kt = kd * e2
        bc = bt * ecl
        kcl = kt * ecl
        lhs = jnp.concatenate([per_head(at), per_head(rt)], axis=0)
        gb = _mm(lhs, per_block(bt), "nt", RW_P_GRAM)
        gk = _mm(lhs, per_block(kt), "nt", RW_P_GRAM)
        a_ab = gb[0:sw] * strict_ref[...]
        a_rb = gb[sw:2 * sw] * incl_ref[...]
        a_ak = gk[0:sw] * strict_ref[...]
        a_rk = gk[sw:2 * sw] * incl_ref[...]
        p_inv = _unit_lower_inverse(a_ab, eye_ref[...], c)
        vst = per_block(vv)
        av = _mm(a_ak, vst, "nn", RW_P_APPLY)
        wa = _mm(p_inv, jnp.concatenate([av, per_head(at * ecm)], axis=1), "nn", RW_P_APPLY)
        ya = _mm(a_rb, wa, "nn", RW_P_APPLY)
        y1 = _mm(a_rk, vst, "nn", RW_P_APPLY) + ya[:, 0:LANES]
        rp = per_head(rt * ecm) + ya[:, LANES:2 * LANES]
        w1 = heads_select(wa[:, 0:LANES])
        ap = heads_add(wa[:, LANES:2 * LANES])
        bc2 = side_by_side(bc)
        y_scr[i] = heads_select(y1)
        rp_scr[i] = heads_add(rp)
        m_scr[i] = _mm(ap, bc2, "tn", RW_P_STATE) * blk_ref[...]
        n_scr[i] = _mm(jnp.concatenate([w1, side_by_side(vv)], axis=0),
                       jnp.concatenate([bc2, side_by_side(kcl)], axis=0), "tn", RW_P_STATE) * blk_ref[...]
        dec_scr[i] = jnp.broadcast_to(side_by_side(jnp.exp(cl))[0:1], (8, 2 * LANES))
        return carry

    lax.fori_loop(0, n_chunks, chunk_terms, 0, unroll=RW_UNROLL)

    def chunk_scan(i, carry):
        st = st_scr[...]
        y_scr[i] = y_scr[i] + _mm(rp_scr[i], st, "nt", RW_P_STATE)
        st_scr[...] = st * dec_scr[i, 0:1, :] + _mm(st, m_scr[i], "nn", RW_P_STATE) + n_scr[i]
        return carry

    lax.fori_loop(0, n_chunks, chunk_scan, 0)

    y = jnp.concatenate([y_scr[j, :, 0:LANES] + y_scr[n_chunks - 1 - j, :, LANES:2 * LANES]
                         for j in range(n_chunks)], axis=0)
    mean = _segsum(y, HD_C) * (1.0 / HD_C)
    yc = y - mean
    var = _segsum(yc * yc, HD_C) * (1.0 / HD_C)
    gn = yc * lax.rsqrt(var + GN_EPS) * lw_ref[...] + lb_ref[...]
    o_ref[...] = (gn + bonus) * g_c
    if want_state:
        for d in range(2):
            for h in range(2):
                b0 = (2 * d + h) * HD_C
                so_ref[d, h] = st_scr[b0:b0 + HD_C, b0:b0 + HD_C]


def _rwkv_chunk_constants():
    c = RW_CHUNK
    sw = 2 * LANES
    i = np.arange(sw)[:, None]
    j = np.arange(sw)[None, :]
    same = (i // c) == (j // c)
    bwd = i >= LANES
    strict = same & np.where(bwd, j > i, j < i)
    incl = same & np.where(bwd, j >= i, j <= i)
    blk = (i // HD_C) == (j // HD_C)
    ci = np.arange(2 * c)[:, None]
    cj = np.arange(2 * c)[None, :]
    cum = ((ci // c) == (cj // c)) & np.where(ci >= c, cj >= ci, cj <= ci)
    f = lambda m: jnp.asarray(m.astype(np.float32))
    return f(cum), f(strict), f(incl), f(blk), f(np.eye(sw))


def _rwkv(pc, mu_c, w0, w2pad, a0, a2pad, g2, k_k, k_a, r_k, lnx_w, lnx_b, n_seq, t, row_block0,
          s0=None, want_state=False):
    has_init = s0 is not None
    n_pairs = WIDTH_C // LANES
    blk_w = 3 * n_pairs
    blk_a = blk_w + 1
    blk_g = blk_w + 2
    state_spec = pl.BlockSpec((None, 2, 2, HD_C, HD_C), lambda s, p: (s, 0, p, 0, 0))

    def sec(col_fn):
        return pl.BlockSpec((t, LANES), lambda s, p: (row_block0 + s, col_fn(p)))

    def vec(col_fn):
        return pl.BlockSpec((1, LANES), lambda s, p: (0, col_fn(p)))

    col_fns = [lambda p: p, lambda p: n_pairs + p, lambda p: 2 * n_pairs + p,
               lambda p: blk_w, lambda p: blk_a, lambda p: blk_g]
    in_specs = [sec(f) for f in col_fns] + [vec(f) for f in col_fns]
    in_specs += [
        pl.BlockSpec((2, 1, LANES), lambda s, p: (0, 0, p)),
        pl.BlockSpec((2, LANES, LANES), lambda s, p: (0, 0, p)),
        pl.BlockSpec((2, 1, LANES), lambda s, p: (0, 0, p)),
        pl.BlockSpec((2, LANES, LANES), lambda s, p: (0, 0, p)),
        pl.BlockSpec((LANES, LANES), lambda s, p: (0, p)),
    ] + [vec(lambda p: p)] * 5
    args = [pc] * 6 + [mu_c] * 6 + [w0, w2pad, a0, a2pad, g2, k_k, k_a, r_k, lnx_w, lnx_b]
    consts = _rwkv_chunk_constants()
    in_specs += [pl.BlockSpec(m.shape, lambda s, p: (0, 0)) for m in consts]
    args += list(consts)
    if has_init:
        in_specs.append(state_spec)
        args.append(s0)
    out_specs = [pl.BlockSpec((t, LANES), lambda s, p: (s, p))]
    out_shape = [jax.ShapeDtypeStruct((n_seq * t, WIDTH_C), F32)]
    if want_state:
        out_specs.append(state_spec)
        out_shape.append(jax.ShapeDtypeStruct((n_seq, 2, N_HEADS_C, HD_C, HD_C), F32))
    n_steps = t // RW_CHUNK
    sw = 2 * LANES
    return pl.pallas_call(
        functools.partial(_rwkv_kernel, t=t, has_init=has_init, want_state=want_state),
        grid=(n_seq, n_pairs),
        in_specs=in_specs,
        out_specs=out_specs,
        out_shape=out_shape,
        scratch_shapes=[
            pltpu.VMEM((t, LANES), F32),
            pltpu.VMEM((t, LANES), F32),
            pltpu.VMEM((t, LANES), F32),
            pltpu.VMEM((2, 3, t, LANES), F32),
            pltpu.VMEM((n_steps, RW_CHUNK, sw), F32),
            pltpu.VMEM((n_steps, RW_CHUNK, sw), F32),
            pltpu.VMEM((n_steps, sw, sw), F32),
            pltpu.VMEM((n_steps, sw, sw), F32),
            pltpu.VMEM((n_steps, 8, sw), F32),
            pltpu.VMEM((sw, sw), F32),
        ],
        compiler_params=pltpu.CompilerParams(dimension_semantics=("arbitrary", "arbitrary")),
        name="rwkv_lat" if has_init else "rwkv_ctx",
    )(*args)


def _merge_kernel(x_ref, oa_ref, ob_ref, oc_ref, pg_ref, mod_ref, wa_ref, wb_ref, wc_ref, wo_ref, o_ref):
    merged = jnp.zeros(x_ref.shape, F32)
    for j, (b_ref, w_ref) in enumerate(((oa_ref, wa_ref), (ob_ref, wb_ref), (oc_ref, wc_ref))):
        gate = _sigmoid(pg_ref[:, j * D_MODEL:(j + 1) * D_MODEL])
        merged = merged + gate * jnp.dot(b_ref[...].astype(BF16), w_ref[...], preferred_element_type=F32)
    y = jnp.dot(merged.astype(BF16), wo_ref[...], preferred_element_type=F32)
    o_ref[...] = x_ref[...] + mod_ref[2] * y


def _merge(x, oa, ob, oc, pg, mod, wpa, wpb, wpc, wout):
    tm = TM
    row = lambda w: pl.BlockSpec((tm, w), lambda i: (i, 0))
    return pl.pallas_call(
        _merge_kernel,
        grid=(N_TOK // tm,),
        in_specs=[row(D_MODEL), row(WIDTH_A), row(WIDTH_BV), row(WIDTH_C), row(SEC_G), _mod_spec(tm),
                  _resident((WIDTH_A, D_MODEL)), _resident((WIDTH_BV, D_MODEL)),
                  _resident((WIDTH_C, D_MODEL)), _resident((D_MODEL, D_MODEL))],
        out_specs=row(D_MODEL),
        out_shape=jax.ShapeDtypeStruct((N_TOK, D_MODEL), F32),
        compiler_params=pltpu.CompilerParams(dimension_semantics=("arbitrary",), vmem_limit_bytes=VMEM_LIMIT),
        name="merge",
    )(x, oa, ob, oc, pg, mod, wpa, wpb, wpc, wout)


def _ffn_kernel(x_ref, g_ref, mod_ref, wg_ref, wu_ref, wo_ref, o_ref, h_scr, acc_scr):
    f = pl.program_id(1)

    @pl.when(f == 0)
    def _():
        h_scr[...] = _modnorm(x_ref[...], g_ref[...], mod_ref[3], mod_ref[4]).astype(BF16)
        acc_scr[...] = jnp.zeros_like(acc_scr)

    h = h_scr[...]
    gate = jnp.dot(h, wg_ref[...], preferred_element_type=F32)
    up = jnp.dot(h, wu_ref[...], preferred_element_type=F32)
    act = (gate * _sigmoid(gate) * up).astype(BF16)
    acc_scr[...] += jnp.dot(act, wo_ref[...], preferred_element_type=F32)

    @pl.when(f == pl.num_programs(1) - 1)
    def _():
        o_ref[...] = x_ref[...] + mod_ref[5] * acc_scr[...]


def _ffn(x, gain, mod, w_in, w_out):
    tm = TM
    nf = D_FF // TF
    return pl.pallas_call(
        _ffn_kernel,
        grid=(N_TOK // tm, nf),
        in_specs=[
            pl.BlockSpec((tm, D_MODEL), lambda i, f: (i, 0)),
            pl.BlockSpec((1, D_MODEL), lambda i, f: (0, 0)),
            _mod_spec(tm),
            pl.BlockSpec((D_MODEL, TF), lambda i, f: (0, f)),
            pl.BlockSpec((D_MODEL, TF), lambda i, f: (0, nf + f)),
            pl.BlockSpec((TF, D_MODEL), lambda i, f: (f, 0)),
        ],
        out_specs=pl.BlockSpec((tm, D_MODEL), lambda i, f: (i, 0)),
        out_shape=jax.ShapeDtypeStruct((N_TOK, D_MODEL), F32),
        scratch_shapes=[pltpu.VMEM((tm, D_MODEL), BF16), pltpu.VMEM((tm, D_MODEL), F32)],
        compiler_params=pltpu.CompilerParams(dimension_semantics=("arbitrary", "arbitrary"),
                                             vmem_limit_bytes=VMEM_LIMIT),
        name="ffn",
    )(x, gain, mod, w_in, w_in, w_out)


def _pad_rows(w, offset, total):
    pads = [(0, 0)] * w.ndim
    pads[-2] = (offset, total - offset - w.shape[-2])
    return jnp.pad(w, pads)


def kernel(x_prompt, x_sample, cache_k, cache_v, state_gla, state_rwkv, c, c_ctx, w_ada, b_ada, g_mix, g_ffn,
           w_in, q_gain, k_gain, gk_w2, gk_b, gla_gain, mu_c, w0, w2, a0, a2, g2, k_k, k_a, r_k, lnx_w, lnx_b,
           w_po_a, w_po_b, w_po_c, w_out, w_ffn_in, w_ffn_out):
    x = jnp.concatenate([x_prompt.reshape(N_TOK_CTX, D_MODEL), x_sample.reshape(N_TOK_LAT, D_MODEL)], axis=0)
    cond = jnp.concatenate([c_ctx[None, :], c, jnp.zeros((COND_ROWS - N_COND, D_MODEL), F32)], axis=0)
    mod = _modulation(cond, w_ada, b_ada)[:, :N_COND].reshape(DEPTH, N_COND, 6, 1, D_MODEL)

    wa = w_in[:, :, OFF_QA:OFF_QB].astype(BF16)
    wb = jnp.pad(w_in[:, :, OFF_QB:OFF_C], ((0, 0), (0, 0), (0, SEC_B - (OFF_C - OFF_QB)))).astype(BF16)
    wc = w_in[:, :, OFF_C:OFF_GATE].astype(BF16)
    wg = w_in[:, :, OFF_GATE:].astype(BF16)
    gk_w2p = jnp.stack([_pad_rows(gk_w2[:, 0], 0, LANES), _pad_rows(gk_w2[:, 1], GK_RANK, LANES)], axis=1)
    w2p = jnp.stack([_pad_rows(w2[:, 0], 0, LANES), _pad_rows(w2[:, 1], W_LORA, LANES)], axis=1)
    a2p = jnp.stack([_pad_rows(a2[:, 0], 0, LANES), _pad_rows(a2[:, 1], A_LORA, LANES)], axis=1)
    wpa, wpb, wpc, wout = (w.astype(BF16) for w in (w_po_a, w_po_b, w_po_c, w_out))
    wfi = w_ffn_in.astype(BF16)
    wfo = w_ffn_out.astype(BF16)
    rope = _rope_tables()
    lat_blk = N_TOK_CTX // DEC_SEQ

    ks, vs, sgs, srs = [], [], [], []
    for li in range(DEPTH):
        vec = lambda a: a[li].reshape(1, -1)
        pa, pb, pc, pg = _inproj(x, vec(g_mix), mod[li], wa[li], wb[li], wc[li], wg[li])

        qg = jnp.tile(vec(q_gain), (1, LANES // HD_A))
        kg = jnp.tile(vec(k_gain), (1, LANES // HD_A))
        oa_c, k_l, v_l = _attention(pa, qg, kg, BATCH, SEQ, 0)
        (oa_l,) = _attention(pa, qg, kg, DEC_BATCH, DEC_SEQ, lat_blk,
                             cache=(cache_k[:, li].reshape(DEC_BATCH, PAST_LEN, KV_A),
                                    cache_v[:, li].reshape(DEC_BATCH, PAST_LEN, KV_A)), rope=rope)

        gkb = gk_b[li].reshape(2, 1, WIDTH_BK)
        ob_c, sg_l = _gla(pb, gk_w2p[li], gkb, vec(gla_gain), BATCH, SEQ, 0, want_state=True)
        (ob_l,) = _gla(pb, gk_w2p[li], gkb, vec(gla_gain), DEC_BATCH, DEC_SEQ, lat_blk, s0=state_gla[:, li])

        rw_args = (pc, vec(mu_c), w0[li].reshape(2, 1, WIDTH_C), w2p[li], a0[li].reshape(2, 1, WIDTH_C), a2p[li],
                   g2[li], vec(k_k), vec(k_a), r_k[li].reshape(1, WIDTH_C), vec(lnx_w), vec(lnx_b))
        oc_c, sr_l = _rwkv(*rw_args, BATCH, SEQ, 0, want_state=True)
        (oc_l,) = _rwkv(*rw_args, DEC_BATCH, DEC_SEQ, lat_blk, s0=state_rwkv[:, li])

        oa = jnp.concatenate([oa_c, oa_l], axis=0)
        ob = jnp.concatenate([ob_c, ob_l], axis=0)
        oc = jnp.concatenate([oc_c, oc_l], axis=0)
        x = _merge(x, oa, ob, oc, pg, mod[li], wpa[li], wpb[li], wpc[li], wout[li])
        x = _ffn(x, vec(g_ffn), mod[li], wfi[li], wfo[li])

        ks.append(k_l.reshape(BATCH, SEQ, N_KV_A, HD_A))
        vs.append(v_l.reshape(BATCH, SEQ, N_KV_A, HD_A))
        sgs.append(sg_l)
        srs.append(sr_l)

    y_prompt = x[:N_TOK_CTX].reshape(BATCH, SEQ, D_MODEL)
    y_sample = x[N_TOK_CTX:].reshape(DEC_BATCH, DEC_SEQ, D_MODEL)
    return (y_prompt, y_sample, jnp.stack(ks, axis=1), jnp.stack(vs, axis=1),
            jnp.stack(sgs, axis=1), jnp.stack(srs, axis=1))
```

```python
import functools
import math

import numpy as np
import jax
import jax.numpy as jnp
from jax import lax
from jax.experimental import pallas as pl
from jax.experimental.pallas import tpu as pltpu

D_MODEL = 1024
BATCH = 16
SEQ = 256
DEPTH = 4
DEC_BATCH = 2
DEC_SEQ = 1024
PAST_LEN = 256
GRID_W = 64
N_HEADS_A = 8
N_KV_A = 2
HD_A = 64
ROPE_THETA = 10000.0
N_HEADS_B = 4
DK_B = 64
DV_B = 128
GK_RANK = 16
GK_NORMALIZER = 16.0
N_HEADS_C = 8
HD_C = 64
W_LORA = 64
A_LORA = 64
G_LORA = 128
GN_EPS = 64e-5
NORM_EPS = 1e-6
WIDTH_A = N_HEADS_A * HD_A
KV_A = N_KV_A * HD_A
WIDTH_BK = N_HEADS_B * DK_B
WIDTH_BV = N_HEADS_B * DV_B
WIDTH_C = N_HEADS_C * HD_C
N_BRANCH = 3
D_FF = ((8 * D_MODEL + 3 * 256 - 1) // (3 * 256)) * 256
OFF_QA = 0
OFF_KA = OFF_QA + WIDTH_A
OFF_VA = OFF_KA + KV_A
OFF_QB = OFF_VA + KV_A
OFF_KB = OFF_QB + WIDTH_BK
OFF_VB = OFF_KB + WIDTH_BK
OFF_GB = OFF_VB + WIDTH_BV
OFF_GKF = OFF_GB + WIDTH_BV
OFF_GKB = OFF_GKF + GK_RANK
OFF_C = OFF_GKB + GK_RANK
C_COLS = 3 * WIDTH_C + 2 * W_LORA + 2 * A_LORA + G_LORA
OFF_GATE = OFF_C + C_COLS
N_IN = OFF_GATE + N_BRANCH * D_MODEL

F32 = jnp.float32
BF16 = jnp.bfloat16

LANES = 128
VMEM_LIMIT = 56 * 1024 * 1024

N_TOK_CTX = BATCH * SEQ
N_TOK_LAT = DEC_BATCH * DEC_SEQ
N_TOK = N_TOK_CTX + N_TOK_LAT
N_COND = 1 + DEC_BATCH
COND_ROWS = 8
SEC_A = OFF_QB
SEC_B_MAIN = OFF_GKF - OFF_QB
SEC_B = SEC_B_MAIN + LANES
SEC_C = C_COLS
SEC_G = N_BRANCH * D_MODEL
TM = 512
TF = D_FF // 2
GLA_CHUNK = 64
RW_CHUNK = 64
HI = lax.Precision.HIGHEST


def _tile_group(i, tm):
    ctx_tiles = N_TOK_CTX // tm
    lat_tiles = DEC_SEQ // tm
    return jnp.where(i < ctx_tiles, 0, 1 + (i - ctx_tiles) // lat_tiles)


def _bdot(a, b):
    return jnp.dot(a.astype(BF16), b.astype(BF16), preferred_element_type=F32)


def _dot_nt(a, b, precision=None):
    return lax.dot_general(a, b, (((1,), (1,)), ((), ())), preferred_element_type=F32, precision=precision)


def _dot_tn(a, b, precision=None):
    return lax.dot_general(a, b, (((0,), (0,)), ((), ())), preferred_element_type=F32, precision=precision)


def _sigmoid(x):
    return jax.nn.sigmoid(x)


def _modnorm(x, gain, shift, scale):
    ms = jnp.mean(x * x, axis=-1, keepdims=True)
    y = x * lax.rsqrt(ms + NORM_EPS) * gain
    return y * (1.0 + scale) + shift


def _mod_kernel(c_ref, w_ref, b_ref, o_ref):
    c = c_ref[...]
    s = c * _sigmoid(c)
    o_ref[...] = _bdot(s, w_ref[...]) + b_ref[...]


def _modulation(cond, w_ada, b_ada):
    tn = 1536
    n_out = 6 * D_MODEL
    return pl.pallas_call(
        _mod_kernel,
        grid=(DEPTH, n_out // tn),
        in_specs=[
            pl.BlockSpec((COND_ROWS, D_MODEL), lambda l, j: (0, 0)),
            pl.BlockSpec((None, D_MODEL, tn), lambda l, j: (l, 0, j)),
            pl.BlockSpec((None, 1, tn), lambda l, j: (l, 0, j)),
        ],
        out_specs=pl.BlockSpec((None, COND_ROWS, tn), lambda l, j: (l, 0, j)),
        out_shape=jax.ShapeDtypeStruct((DEPTH, COND_ROWS, n_out), F32),
        compiler_params=pltpu.CompilerParams(dimension_semantics=("arbitrary", "arbitrary")),
        name="modulation",
    )(cond, w_ada, b_ada.reshape(DEPTH, 1, n_out))


def _inproj_kernel(x_ref, g_ref, mod_ref, wa_ref, wb_ref, wc_ref, wg_ref, oa_ref, ob_ref, oc_ref, og_ref):
    h = _modnorm(x_ref[...], g_ref[...], mod_ref[0], mod_ref[1]).astype(BF16)
    oa_ref[...] = jnp.dot(h, wa_ref[...], preferred_element_type=F32)
    ob_ref[...] = jnp.dot(h, wb_ref[...], preferred_element_type=F32)
    oc_ref[...] = jnp.dot(h, wc_ref[...], preferred_element_type=F32)
    og_ref[...] = jnp.dot(h, wg_ref[...], preferred_element_type=F32)


def _resident(shape):
    nd = len(shape)
    return pl.BlockSpec(shape, lambda *_: (0,) * nd, pipeline_mode=pl.Buffered(1))


def _mod_spec(tm):
    return pl.BlockSpec((None, 6, 1, D_MODEL), lambda i, *_: (_tile_group(i, tm), 0, 0, 0))


def _inproj(x, gain, mod, wa, wb, wc, wg):
    tm = 256
    widths = (SEC_A, SEC_B, SEC_C, SEC_G)
    return pl.pallas_call(
        _inproj_kernel,
        grid=(N_TOK // tm,),
        in_specs=[
            pl.BlockSpec((tm, D_MODEL), lambda i: (i, 0)),
            _resident((1, D_MODEL)),
            _mod_spec(tm),
        ] + [_resident((D_MODEL, w)) for w in widths],
        out_specs=[pl.BlockSpec((tm, w), lambda i: (i, 0)) for w in widths],
        out_shape=[jax.ShapeDtypeStruct((N_TOK, w), F32) for w in widths],
        compiler_params=pltpu.CompilerParams(dimension_semantics=("arbitrary",), vmem_limit_bytes=VMEM_LIMIT),
        name="inproj",
    )(x, gain, mod, wa, wb, wc, wg)


def _segsum(x, width):
    n = x.shape[-1]
    gi = lax.broadcasted_iota(jnp.int32, (n, n), 0) // width
    gj = lax.broadcasted_iota(jnp.int32, (n, n), 1) // width
    e = jnp.where(gi == gj, 1.0, 0.0).astype(F32)
    return jnp.dot(x, e, preferred_element_type=F32, precision=HI)


def _swap_pairs(x):
    lane = lax.broadcasted_iota(jnp.int32, x.shape, 1)
    nxt = pltpu.roll(x, LANES - 1, 1)
    prv = pltpu.roll(x, 1, 1)
    return jnp.where(lane % 2 == 0, nxt, prv)


def _head_norm(x, gain):
    ms = _segsum(x * x, HD_A) * (1.0 / HD_A)
    return x * lax.rsqrt(ms + NORM_EPS) * gain


def _attn_kernel(*refs, t, latent):
    if latent:
        (pa_ref, qg_ref, kg_ref, ck_ref, cv_ref, cos_ref, sin_ref, o_ref, k_scr, v_scr, q_scr) = refs
    else:
        (pa_ref, qg_ref, kg_ref, o_ref, ko_ref, vo_ref, k_scr, v_scr, q_scr) = refs
    tk = k_scr.shape[0]
    past = tk - t
    kn = _head_norm(pa_ref[:, OFF_KA:OFF_VA], kg_ref[...])
    va = pa_ref[:, OFF_VA:OFF_QB]
    if latent:
        cos = cos_ref[...]
        sin = sin_ref[...]
        kn = kn * cos + _swap_pairs(kn) * sin
        k_scr[0:past, :] = ck_ref[...]
        v_scr[0:past, :] = cv_ref[...]
    else:
        ko_ref[...] = kn
        vo_ref[...] = va
    k_scr[past:tk, :] = kn
    v_scr[past:tk, :] = va
    for c in range(WIDTH_A // LANES):
        qn = _head_norm(pa_ref[:, c * LANES:(c + 1) * LANES], qg_ref[...])
        if latent:
            qn = qn * cos + _swap_pairs(qn) * sin
        q_scr[:, c * LANES:(c + 1) * LANES] = qn * (HD_A ** -0.5)

    group = N_HEADS_A // N_KV_A
    tq = min(t, 256)
    for kv in range(N_KV_A):
        kh = k_scr[:, kv * HD_A:(kv + 1) * HD_A].astype(BF16)
        vh = v_scr[:, kv * HD_A:(kv + 1) * HD_A].astype(BF16)
        for qb in range(t // tq):
            outs = []
            for g in range(group):
                h = kv * group + g
                qh = q_scr[qb * tq:(qb + 1) * tq, h * HD_A:(h + 1) * HD_A].astype(BF16)
                s = _dot_nt(qh, kh)
                m = jnp.max(s, axis=-1, keepdims=True)
                e = jnp.exp(s - m)
                p = e / jnp.sum(e, axis=-1, keepdims=True)
                outs.append(jnp.dot(p.astype(BF16), vh, preferred_element_type=F32))
            for j in range(group // 2):
                pair = jnp.concatenate([outs[2 * j], outs[2 * j + 1]], axis=1)
                col = (kv * group + 2 * j) * HD_A
                o_ref[qb * tq:(qb + 1) * tq, col:col + LANES] = pair


def _attention(pa, q_gain, k_gain, n_seq, t, row_block0, cache=None, rope=None):
    latent = cache is not None
    tk = t + (PAST_LEN if latent else 0)
    in_specs = [
        pl.BlockSpec((t, SEC_A), lambda s: (row_block0 + s, 0)),
        _resident((1, LANES)),
        _resident((1, LANES)),
    ]
    args = [pa, q_gain, k_gain]
    out_specs = [pl.BlockSpec((t, WIDTH_A), lambda s: (s, 0))]
    out_shape = [jax.ShapeDtypeStruct((n_seq * t, WIDTH_A), F32)]
    if latent:
        in_specs += [
            pl.BlockSpec((None, PAST_LEN, KV_A), lambda s: (s, 0, 0)),
            pl.BlockSpec((None, PAST_LEN, KV_A), lambda s: (s, 0, 0)),
            _resident((t, LANES)),
            _resident((t, LANES)),
        ]
        args += [cache[0], cache[1], rope[0], rope[1]]
    else:
        out_specs += [pl.BlockSpec((None, t, KV_A), lambda s: (s, 0, 0))] * 2
        out_shape += [jax.ShapeDtypeStruct((n_seq, t, KV_A), F32)] * 2
    return pl.pallas_call(
        functools.partial(_attn_kernel, t=t, latent=latent),
        grid=(n_seq,),
        in_specs=in_specs,
        out_specs=out_specs,
        out_shape=out_shape,
        scratch_shapes=[
            pltpu.VMEM((tk, KV_A), F32),
            pltpu.VMEM((tk, KV_A), F32),
            pltpu.VMEM((t, WIDTH_A), F32),
        ],
        compiler_params=pltpu.CompilerParams(dimension_semantics=("arbitrary",), vmem_limit_bytes=VMEM_LIMIT),
        name="attn_lat" if latent else "attn_ctx",
    )(*args)


def _rope_tables():
    tpos = np.arange(DEC_SEQ)
    row = (tpos // GRID_W).astype(np.float32)
    col = (tpos % GRID_W).astype(np.float32)
    n_pairs = HD_A // 4
    inv = (ROPE_THETA ** (-np.arange(n_pairs, dtype=np.float32) / n_pairs)).astype(np.float32)
    ang = np.concatenate([row[:, None] * inv, col[:, None] * inv], axis=-1)
    cos = np.repeat(np.cos(ang), 2, axis=-1)
    sin = np.repeat(np.sin(ang), 2, axis=-1)
    sign = np.tile(np.array([-1.0, 1.0], np.float32), HD_A // 2)
    reps = LANES // HD_A
    return (jnp.asarray(np.tile(cos, (1, reps)), F32), jnp.asarray(np.tile(sin * sign, (1, reps)), F32))


def _tri(n, reverse, strict):
    i = lax.broadcasted_iota(jnp.int32, (n, n), 0)
    j = lax.broadcasted_iota(jnp.int32, (n, n), 1)
    if reverse:
        keep = (j > i) if strict else (j >= i)
    else:
        keep = (j < i) if strict else (j <= i)
    return keep


def _head_masks(shape):
    lane = lax.broadcasted_iota(jnp.int32, shape, 1)
    first = lane < (LANES // 2)
    return first, jnp.logical_not(first)


def _log_sigmoid(z):
    return jnp.minimum(z, 0.0) - jnp.log1p(jnp.exp(-jnp.abs(z)))


def _gla_kernel(*refs, t, has_init, want_state):
    refs = list(refs)
    (q_ref, k_ref, v_ref, g_ref, lr_ref, w2_ref, gb_ref, gain_ref,
     cum_ref, incl_ref, nmask_ref) = refs[:11]
    pos = 11
    if has_init:
        s0_ref = refs[pos]
        pos += 1
    o_ref = refs[pos]
    pos += 1
    if want_state:
        so_ref = refs[pos]
        pos += 1
    lg_scr, o_scr, qh_scr, n_scr, dec_scr, st_scr = refs[pos:pos + 6]

    c = GLA_CHUNK
    n_chunks = t // c
    half = LANES // 2
    jj = lax.broadcasted_iota(jnp.int32, (half, LANES), 0)
    ll = lax.broadcasted_iota(jnp.int32, (half, LANES), 1)
    sel = [jnp.where(ll == jj + h * half, 1.0, 0.0).astype(F32) for h in range(2)]

    lr = lr_ref[...].astype(BF16)
    for d in range(2):
        z = jnp.dot(lr, w2_ref[d].astype(BF16), preferred_element_type=F32) + gb_ref[d]
        lg_scr[d] = _log_sigmoid(z) * (1.0 / GK_NORMALIZER)
        for h in range(2):
            if has_init:
                st_scr[d, h * DV_B:(h + 1) * DV_B, :] = _dot_tn(s0_ref[d, h], sel[h], precision=HI)
            else:
                st_scr[d, h * DV_B:(h + 1) * DV_B, :] = jnp.zeros((DV_B, LANES), F32)

    m0, m1 = _head_masks((c, LANES))
    scale = DK_B ** -0.5

    def chunk_terms(i, carry):
        rows_f = pl.ds(pl.multiple_of(i * c, c), c)
        rows_b = pl.ds(pl.multiple_of((n_chunks - 1 - i) * c, c), c)
        q = jnp.concatenate([q_ref[rows_f, :], q_ref[rows_b, :]], axis=0)
        k = jnp.concatenate([k_ref[rows_f, :], k_ref[rows_b, :]], axis=0)
        g = jnp.concatenate([lg_scr[0, rows_f, :], lg_scr[1, rows_b, :]], axis=0)
        b = _mm_exact_lhs(cum_ref[...], g)
        bm = jnp.concatenate([jnp.broadcast_to(b[c // 2:c // 2 + 1], (c, LANES)),
                              jnp.broadcast_to(b[c + c // 2:c + c // 2 + 1], (c, LANES))], axis=0)
        bl = jnp.concatenate([jnp.broadcast_to(b[c - 1:c], (c, LANES)),
                              jnp.broadcast_to(b[c:c + 1], (c, LANES))], axis=0)
        e1 = jnp.exp(b - bm)
        e2 = jnp.exp(bm - b)
        qt = q * scale * e1
        kt = k * e2
        qh_scr[i] = qt * jnp.exp(bm)
        kc = kt * jnp.exp(bl - bm)
        qf, qb = qt[0:c], qt[c:2 * c]
        lhs = jnp.concatenate([jnp.where(m0, qf, 0.0), jnp.where(m1, qf, 0.0),
                               jnp.where(m0, qb, 0.0), jnp.where(m1, qb, 0.0)], axis=0)
        rhs = jnp.concatenate([kt[0:c], kt[0:c], kt[c:2 * c], kt[c:2 * c]], axis=0)
        att = _mm(lhs, rhs, "nt", GLA_P_ATT) * incl_ref[...]
        vf = v_ref[rows_f, :]
        vb = v_ref[rows_b, :]
        vst = jnp.concatenate([vf[:, 0:DV_B], vf[:, DV_B:2 * DV_B], vb[:, 0:DV_B], vb[:, DV_B:2 * DV_B]], axis=0)
        o_scr[i] = _mm(att, vst, "nn", 1)
        n_scr[i, 0] = _mm(vf, kc[0:c], "tn", 1) * nmask_ref[...]
        n_scr[i, 1] = _mm(vb, kc[c:2 * c], "tn", 1) * nmask_ref[...]
        dec = jnp.exp(bl)
        dec_scr[i, 0] = dec[0:8]
        dec_scr[i, 1] = dec[c:c + 8]
        return carry

    lax.fori_loop(0, n_chunks, chunk_terms, 0, unroll=GLA_UNROLL)

    def chunk_scan(i, carry):
        for d in range(2):
            st = st_scr[d]
            oi = _mm(qh_scr[i, d * c:(d + 1) * c, :], st, "nt", 1)
            for h in range(2):
                r0 = (2 * d + h) * c
                o_scr[i, r0:r0 + c, :] = o_scr[i, r0:r0 + c, :] + oi[:, h * DV_B:(h + 1) * DV_B]
            st_scr[d] = st * dec_scr[i, d, 0:1, :] + n_scr[i, d]
        return carry

    lax.fori_loop(0, n_chunks, chunk_scan, 0)

    for h in range(2):
        o = jnp.concatenate([o_scr[j, h * c:(h + 1) * c, :] + o_scr[n_chunks - 1 - j, (2 + h) * c:(3 + h) * c, :]
                             for j in range(n_chunks)], axis=0)
        ms = jnp.mean(o * o, axis=-1, keepdims=True)
        o = o * lax.rsqrt(ms + NORM_EPS) * gain_ref[...]
        gate = g_ref[:, h * DV_B:(h + 1) * DV_B]
        o_ref[:, h * DV_B:(h + 1) * DV_B] = o * (gate * _sigmoid(gate))
        if want_state:
            for d in range(2):
                so_ref[d, h] = _dot_nt(sel[h], st_scr[d, h * DV_B:(h + 1) * DV_B, :], precision=HI)


def _gla(pb, w2pad, gk_b, gain, n_seq, t, row_block0, s0=None, want_state=False):
    has_init = s0 is not None
    pair_w = 2 * DV_B
    state_spec = pl.BlockSpec((None, 2, 2, DK_B, DV_B), lambda s, p: (s, 0, p, 0, 0))
    in_specs = [
        pl.BlockSpec((t, LANES), lambda s, p: (row_block0 + s, p)),
        pl.BlockSpec((t, LANES), lambda s, p: (row_block0 + s, WIDTH_BK // LANES + p)),
        pl.BlockSpec((t, pair_w), lambda s, p: (row_block0 + s, (2 * WIDTH_BK) // pair_w + p)),
        pl.BlockSpec((t, pair_w), lambda s, p: (row_block0 + s, (2 * WIDTH_BK + WIDTH_BV) // pair_w + p)),
        pl.BlockSpec((t, LANES), lambda s, p: (row_block0 + s, SEC_B_MAIN // LANES)),
        pl.BlockSpec((2, LANES, LANES), lambda s, p: (0, 0, p)),
        pl.BlockSpec((2, 1, LANES), lambda s, p: (0, 0, p)),
        pl.BlockSpec((1, DV_B), lambda s, p: (0, 0)),
    ]
    args = [pb, pb, pb, pb, pb, w2pad, gk_b, gain]
    cum, _, incl, _, _ = _rwkv_chunk_constants()
    ri = np.arange(pair_w)[:, None] // DV_B
    li = np.arange(LANES)[None, :] // DK_B
    consts = (cum, incl, jnp.asarray((ri == li).astype(np.float32)))
    in_specs += [pl.BlockSpec(m.shape, lambda s, p: (0, 0)) for m in consts]
    args += list(consts)
    if has_init:
        in_specs.append(state_spec)
        args.append(s0)
    n_steps = t // GLA_CHUNK
    out_specs = [pl.BlockSpec((t, pair_w), lambda s, p: (s, p))]
    out_shape = [jax.ShapeDtypeStruct((n_seq * t, WIDTH_BV), F32)]
    if want_state:
        out_specs.append(state_spec)
        out_shape.append(jax.ShapeDtypeStruct((n_seq, 2, N_HEADS_B, DK_B, DV_B), F32))
    return pl.pallas_call(
        functools.partial(_gla_kernel, t=t, has_init=has_init, want_state=want_state),
        grid=(n_seq, N_HEADS_B // 2),
        in_specs=in_specs,
        out_specs=out_specs,
        out_shape=out_shape,
        scratch_shapes=[
            pltpu.VMEM((2, t, LANES), F32),
            pltpu.VMEM((n_steps, 4 * GLA_CHUNK, DV_B), F32),
            pltpu.VMEM((n_steps, 2 * GLA_CHUNK, LANES), F32),
            pltpu.VMEM((n_steps, 2, pair_w, LANES), F32),
            pltpu.VMEM((n_steps, 2, 8, LANES), F32),
            pltpu.VMEM((2, pair_w, LANES), F32),
        ],
        compiler_params=pltpu.CompilerParams(dimension_semantics=("arbitrary", "arbitrary")),
        name="gla_lat" if has_init else "gla_ctx",
    )(*args)


def _shift_mix(u, mu):
    t = u.shape[0]
    row = lax.broadcasted_iota(jnp.int32, u.shape, 0)
    prev = jnp.where(row == 0, 0.0, pltpu.roll(u, 1, 0))
    nxt = jnp.where(row == t - 1, 0.0, pltpu.roll(u, t - 1, 0))
    return u + mu * (0.5 * (prev + nxt) - u)


_DIMS = {"nn": (((1,), (0,)), ((), ())), "nt": (((1,), (1,)), ((), ())), "tn": (((0,), (0,)), ((), ()))}


def _split_bf16(x):
    hi = x.astype(BF16)
    lo = (x - hi.astype(F32)).astype(BF16)
    return hi, lo


def _mm(a, b, form, passes):
    dims = _DIMS[form]
    if passes == 6:
        return lax.dot_general(a, b, dims, preferred_element_type=F32, precision=HI)
    if passes == 1:
        return lax.dot_general(a.astype(BF16), b.astype(BF16), dims, preferred_element_type=F32)
    a_hi, a_lo = _split_bf16(a)
    b_hi, b_lo = _split_bf16(b)
    ka = dims[0][0][0]
    kb = dims[0][1][0]
    a_cat = jnp.concatenate([a_hi, a_lo, a_hi], axis=ka)
    b_cat = jnp.concatenate([b_hi, b_hi, b_lo], axis=kb)
    return lax.dot_general(a_cat, b_cat, dims, preferred_element_type=F32)


def _mm_exact_lhs(a, x):
    a16 = a.astype(BF16)
    x_hi, x_lo = _split_bf16(x)
    return jnp.dot(jnp.concatenate([a16, a16], axis=1), jnp.concatenate([x_hi, x_lo], axis=0),
                   preferred_element_type=F32)


GLA_P_ATT = 1
GLA_UNROLL = 4

RW_P_GRAM = 1
RW_P_INV = 1
RW_P_INVP = 1
RW_P_APPLY = 1
RW_P_STATE = 1


RW_UNROLL = 4
RW_REFINE = 1


def _unit_lower_inverse(n_mat, eye, c):
    p = eye + n_mat
    m = n_mat
    for _ in range(int(math.log2(c)) - 1):
        m = _mm(m, m, "nn", RW_P_INV)
        p = p + _mm(p, m, "nn", RW_P_INVP)
    return p


def _rwkv_kernel(*refs, t, has_init, want_state):
    refs = list(refs)
    (r_ref, k_ref, v_ref, wl_ref, al_ref, gl_ref,
     mur_ref, muk_ref, muv_ref, muw_ref, mua_ref, mug_ref,
     w0_ref, w2_ref, a0_ref, a2_ref, g2_ref, kk_ref, ka_ref, rk_ref, lw_ref, lb_ref,
     cum_ref, strict_ref, incl_ref, blk_ref, eye_ref) = refs[:27]
    pos = 27
    if has_init:
        s0_ref = refs[pos]
        pos += 1
    o_ref = refs[pos]
    pos += 1
    if want_state:
        so_ref = refs[pos]
        pos += 1
    r_scr, v_scr, a_scr, dir_scr, y_scr, rp_scr, m_scr, n_scr, dec_scr, st_scr = refs[pos:pos + 10]

    c = RW_CHUNK
    n_chunks = t // c
    half = LANES // 2

    r = _shift_mix(r_ref[...], mur_ref[...])
    kc = _shift_mix(k_ref[...], muk_ref[...])
    vc = _shift_mix(v_ref[...], muv_ref[...])
    wl_in = jnp.tanh(_shift_mix(wl_ref[...], muw_ref[...]))
    al_in = _shift_mix(al_ref[...], mua_ref[...])
    g_in = _sigmoid(_shift_mix(gl_ref[...], mug_ref[...]))

    kkf = kc * kk_ref[...]
    kk = kkf * lax.rsqrt(_segsum(kkf * kkf, HD_C) + 1e-12)
    r_scr[...] = r
    v_scr[...] = vc
    a_scr[...] = -kk
    bonus = jnp.zeros_like(r)
    for d in range(2):
        wl = w0_ref[d] + _bdot(wl_in, w2_ref[d])
        dir_scr[d, 0] = -math.exp(-0.5) * _sigmoid(wl)
        a = _sigmoid(a0_ref[d] + _bdot(al_in, a2_ref[d]))
        kd = kc * (1.0 + (a - 1.0) * ka_ref[...])
        dir_scr[d, 1] = kd
        dir_scr[d, 2] = kk * a
        bonus = bonus + _segsum(r * kd * rk_ref[...], HD_C)
    bonus = bonus * vc
    g_c = _bdot(g_in, g2_ref[...])

    nblk = 2 * (LANES // HD_C)
    sw = nblk * HD_C
    if has_init:
        z = jnp.zeros((HD_C, HD_C), F32)
        blocks = [s0_ref[0, 0], s0_ref[0, 1], s0_ref[1, 0], s0_ref[1, 1]]
        st_scr[...] = jnp.concatenate(
            [jnp.concatenate([blocks[i] if i == j else z for j in range(nblk)], axis=1) for i in range(nblk)], axis=0)
    else:
        st_scr[...] = jnp.zeros((sw, sw), F32)

    m0, m1 = _head_masks((c, LANES))

    def per_head(x):
        xf, xb = x[0:c], x[c:2 * c]
        return jnp.concatenate([jnp.where(m0, xf, 0.0), jnp.where(m1, xf, 0.0),
                                jnp.where(m0, xb, 0.0), jnp.where(m1, xb, 0.0)], axis=0)

    def per_block(x):
        xf, xb = x[0:c], x[c:2 * c]
        return jnp.concatenate([xf, xf, xb, xb], axis=0)

    def side_by_side(x):
        return jnp.concatenate([x[0:c], x[c:2 * c]], axis=1)

    def heads_select(x):
        return jnp.concatenate([jnp.where(m0, x[0:c], x[c:2 * c]), jnp.where(m0, x[2 * c:3 * c], x[3 * c:4 * c])],
                               axis=1)

    def heads_add(x):
        return jnp.concatenate([x[0:c] + x[c:2 * c], x[2 * c:3 * c] + x[3 * c:4 * c]], axis=1)

    def chunk_terms(i, carry):
        rows_f = pl.ds(pl.multiple_of(i * c, c), c)
        rows_b = pl.ds(pl.multiple_of((n_chunks - 1 - i) * c, c), c)

        def both(ref_f, ref_b):
            return jnp.concatenate([ref_f[rows_f, :], ref_b[rows_b, :]], axis=0)

        rr = both(r_scr, r_scr)
        vv = both(v_scr, v_scr)
        aa = both(a_scr, a_scr)
        lw = both(dir_scr.at[0, 0], dir_scr.at[1, 0])
        kd = both(dir_scr.at[0, 1], dir_scr.at[1, 1])
        bb = both(dir_scr.at[0, 2], dir_scr.at[1, 2])
        cs = _mm_exact_lhs(cum_ref[...], lw)
        mid = lambda x: jnp.concatenate([jnp.broadcast_to(x[c // 2:c // 2 + 1], (c, LANES)),
                                         jnp.broadcast_to(x[c + c // 2:c + c // 2 + 1], (c, LANES))], axis=0)
        last = lambda x: jnp.concatenate([jnp.broadcast_to(x[c - 1:c], (c, LANES)),
                                          jnp.broadcast_to(x[c:c + 1], (c, LANES))], axis=0)
        cm = mid(cs)
        cl = last(cs)
        e1 = jnp.exp(cs - cm)
        e2 = jnp.exp(cm - cs)
        ecm = jnp.exp(cm)
        ecl = jnp.exp(cl - cm)
        rt = rr * e1
        at = aa * e1 * jnp.exp(-lw)
        bt = bb * e2
        kt = kd * e2
        bc = bt * ecl
        kcl = kt * ecl
        lhs = jnp.concatenate([per_head(at), per_head(rt)], axis=0)
        gb = _mm(lhs, per_block(bt), "nt", RW_P_GRAM)
        gk = _mm(lhs, per_block(kt), "nt", RW_P_GRAM)
        a_ab = gb[0:sw] * strict_ref[...]
        a_rb = gb[sw:2 * sw] * incl_ref[...]
        a_ak = gk[0:sw] * strict_ref[...]
        a_rk = gk[sw:2 * sw] * incl_ref[...]
        p_inv = _unit_lower_inverse(a_ab, eye_ref[...], c)
        vst = per_block(vv)
        av = _mm(a_ak, vst, "nn", RW_P_APPLY)
        rhs = jnp.concatenate([av, per_head(at * ecm)], axis=1)
        wa = _mm(p_inv, rhs, "nn", RW_P_APPLY)
        for _ in range(RW_REFINE):
            res = rhs - wa + _mm(a_ab, wa, "nn", 3)
            wa = wa + _mm(p_inv, res, "nn", RW_P_APPLY)
        ya = _mm(a_rb, wa, "nn", RW_P_APPLY)
        y1 = _mm(a_rk, vst, "nn", RW_P_APPLY) + ya[:, 0:LANES]
        rp = per_head(rt * ecm) + ya[:, LANES:2 * LANES]
        w1 = heads_select(wa[:, 0:LANES])
        ap = heads_add(wa[:, LANES:2 * LANES])
        bc2 = side_by_side(bc)
        y_scr[i] = heads_select(y1)
        rp_scr[i] = heads_add(rp)
        m_scr[i] = _mm(ap, bc2, "tn", RW_P_STATE) * blk_ref[...]
        n_scr[i] = _mm(jnp.concatenate([w1, side_by_side(vv)], axis=0),
                       jnp.concatenate([bc2, side_by_side(kcl)], axis=0), "tn", RW_P_STATE) * blk_ref[...]
        dec_scr[i] = jnp.broadcast_to(side_by_side(jnp.exp(cl))[0:1], (8, 2 * LANES))
        return carry

    lax.fori_loop(0, n_chunks, chunk_terms, 0, unroll=RW_UNROLL)

    def chunk_scan(i, carry):
        st = st_scr[...]
        y_scr[i] = y_scr[i] + _mm(rp_scr[i], st, "nt", RW_P_STATE)
        st_scr[...] = st * dec_scr[i, 0:1, :] + _mm(st, m_scr[i], "nn", RW_P_STATE) + n_scr[i]
        return carry

    lax.fori_loop(0, n_chunks, chunk_scan, 0)

    y = jnp.concatenate([y_scr[j, :, 0:LANES] + y_scr[n_chunks - 1 - j, :, LANES:2 * LANES]
                         for j in range(n_chunks)], axis=0)
    mean = _segsum(y, HD_C) * (1.0 / HD_C)
    yc = y - mean
    var = _segsum(yc * yc, HD_C) * (1.0 / HD_C)
    gn = yc * lax.rsqrt(var + GN_EPS) * lw_ref[...] + lb_ref[...]
    o_ref[...] = (gn + bonus) * g_c
    if want_state:
        for d in range(2):
            for h in range(2):
                b0 = (2 * d + h) * HD_C
                so_ref[d, h] = st_scr[b0:b0 + HD_C, b0:b0 + HD_C]


def _rwkv_chunk_constants():
    c = RW_CHUNK
    sw = 2 * LANES
    i = np.arange(sw)[:, None]
    j = np.arange(sw)[None, :]
    same = (i // c) == (j // c)
    bwd = i >= LANES
    strict = same & np.where(bwd, j > i, j < i)
    incl = same & np.where(bwd, j >= i, j <= i)
    blk = (i // HD_C) == (j // HD_C)
    ci = np.arange(2 * c)[:, None]
    cj = np.arange(2 * c)[None, :]
    cum = ((ci // c) == (cj // c)) & np.where(ci >= c, cj >= ci, cj <= ci)
    f = lambda m: jnp.asarray(m.astype(np.float32))
    return f(cum), f(strict), f(incl), f(blk), f(np.eye(sw))


def _rwkv(pc, mu_c, w0, w2pad, a0, a2pad, g2, k_k, k_a, r_k, lnx_w, lnx_b, n_seq, t, row_block0,
          s0=None, want_state=False):
    has_init = s0 is not None
    n_pairs = WIDTH_C // LANES
    blk_w = 3 * n_pairs
    blk_a = blk_w + 1
    blk_g = blk_w + 2
    state_spec = pl.BlockSpec((None, 2, 2, HD_C, HD_C), lambda s, p: (s, 0, p, 0, 0))

    def sec(col_fn):
        return pl.BlockSpec((t, LANES), lambda s, p: (row_block0 + s, col_fn(p)))

    def vec(col_fn):
        return pl.BlockSpec((1, LANES), lambda s, p: (0, col_fn(p)))

    col_fns = [lambda p: p, lambda p: n_pairs + p, lambda p: 2 * n_pairs + p,
               lambda p: blk_w, lambda p: blk_a, lambda p: blk_g]
    in_specs = [sec(f) for f in col_fns] + [vec(f) for f in col_fns]
    in_specs += [
        pl.BlockSpec((2, 1, LANES), lambda s, p: (0, 0, p)),
        pl.BlockSpec((2, LANES, LANES), lambda s, p: (0, 0, p)),
        pl.BlockSpec((2, 1, LANES), lambda s, p: (0, 0, p)),
        pl.BlockSpec((2, LANES, LANES), lambda s, p: (0, 0, p)),
        pl.BlockSpec((LANES, LANES), lambda s, p: (0, p)),
    ] + [vec(lambda p: p)] * 5
    args = [pc] * 6 + [mu_c] * 6 + [w0, w2pad, a0, a2pad, g2, k_k, k_a, r_k, lnx_w, lnx_b]
    consts = _rwkv_chunk_constants()
    in_specs += [pl.BlockSpec(m.shape, lambda s, p: (0, 0)) for m in consts]
    args += list(consts)
    if has_init:
        in_specs.append(state_spec)
        args.append(s0)
    out_specs = [pl.BlockSpec((t, LANES), lambda s, p: (s, p))]
    out_shape = [jax.ShapeDtypeStruct((n_seq * t, WIDTH_C), F32)]
    if want_state:
        out_specs.append(state_spec)
        out_shape.append(jax.ShapeDtypeStruct((n_seq, 2, N_HEADS_C, HD_C, HD_C), F32))
    n_steps = t // RW_CHUNK
    sw = 2 * LANES
    return pl.pallas_call(
        functools.partial(_rwkv_kernel, t=t, has_init=has_init, want_state=want_state),
        grid=(n_seq, n_pairs),
        in_specs=in_specs,
        out_specs=out_specs,
        out_shape=out_shape,
        scratch_shapes=[
            pltpu.VMEM((t, LANES), F32),
            pltpu.VMEM((t, LANES), F32),
            pltpu.VMEM((t, LANES), F32),
            pltpu.VMEM((2, 3, t, LANES), F32),
            pltpu.VMEM((n_steps, RW_CHUNK, sw), F32),
            pltpu.VMEM((n_steps, RW_CHUNK, sw), F32),
            pltpu.VMEM((n_steps, sw, sw), F32),
            pltpu.VMEM((n_steps, sw, sw), F32),
            pltpu.VMEM((n_steps, 8, sw), F32),
            pltpu.VMEM((sw, sw), F32),
        ],
        compiler_params=pltpu.CompilerParams(dimension_semantics=("arbitrary", "arbitrary")),
        name="rwkv_lat" if has_init else "rwkv_ctx",
    )(*args)


def _merge_kernel(x_ref, oa_ref, ob_ref, oc_ref, pg_ref, mod_ref, wa_ref, wb_ref, wc_ref, wo_ref, o_ref):
    merged = jnp.zeros(x_ref.shape, F32)
    for j, (b_ref, w_ref) in enumerate(((oa_ref, wa_ref), (ob_ref, wb_ref), (oc_ref, wc_ref))):
        gate = _sigmoid(pg_ref[:, j * D_MODEL:(j + 1) * D_MODEL])
        merged = merged + gate * jnp.dot(b_ref[...].astype(BF16), w_ref[...], preferred_element_type=F32)
    y = jnp.dot(merged.astype(BF16), wo_ref[...], preferred_element_type=F32)
    o_ref[...] = x_ref[...] + mod_ref[2] * y


def _merge(x, oa, ob, oc, pg, mod, wpa, wpb, wpc, wout):
    tm = TM
    row = lambda w: pl.BlockSpec((tm, w), lambda i: (i, 0))
    return pl.pallas_call(
        _merge_kernel,
        grid=(N_TOK // tm,),
        in_specs=[row(D_MODEL), row(WIDTH_A), row(WIDTH_BV), row(WIDTH_C), row(SEC_G), _mod_spec(tm),
                  _resident((WIDTH_A, D_MODEL)), _resident((WIDTH_BV, D_MODEL)),
                  _resident((WIDTH_C, D_MODEL)), _resident((D_MODEL, D_MODEL))],
        out_specs=row(D_MODEL),
        out_shape=jax.ShapeDtypeStruct((N_TOK, D_MODEL), F32),
        compiler_params=pltpu.CompilerParams(dimension_semantics=("arbitrary",), vmem_limit_bytes=VMEM_LIMIT),
        name="merge",
    )(x, oa, ob, oc, pg, mod, wpa, wpb, wpc, wout)


def _ffn_kernel(x_ref, g_ref, mod_ref, wg_ref, wu_ref, wo_ref, o_ref, h_scr, acc_scr):
    f = pl.program_id(1)

    @pl.when(f == 0)
    def _():
        h_scr[...] = _modnorm(x_ref[...], g_ref[...], mod_ref[3], mod_ref[4]).astype(BF16)
        acc_scr[...] = jnp.zeros_like(acc_scr)

    h = h_scr[...]
    gate = jnp.dot(h, wg_ref[...], preferred_element_type=F32)
    up = jnp.dot(h, wu_ref[...], preferred_element_type=F32)
    act = (gate * _sigmoid(gate) * up).astype(BF16)
    acc_scr[...] += jnp.dot(act, wo_ref[...], preferred_element_type=F32)

    @pl.when(f == pl.num_programs(1) - 1)
    def _():
        o_ref[...] = x_ref[...] + mod_ref[5] * acc_scr[...]


def _ffn(x, gain, mod, w_in, w_out):
    tm = TM
    nf = D_FF // TF
    return pl.pallas_call(
        _ffn_kernel,
        grid=(N_TOK // tm, nf),
        in_specs=[
            pl.BlockSpec((tm, D_MODEL), lambda i, f: (i, 0)),
            pl.BlockSpec((1, D_MODEL), lambda i, f: (0, 0)),
            _mod_spec(tm),
            pl.BlockSpec((D_MODEL, TF), lambda i, f: (0, f)),
            pl.BlockSpec((D_MODEL, TF), lambda i, f: (0, nf + f)),
            pl.BlockSpec((TF, D_MODEL), lambda i, f: (f, 0)),
        ],
        out_specs=pl.BlockSpec((tm, D_MODEL), lambda i, f: (i, 0)),
        out_shape=jax.ShapeDtypeStruct((N_TOK, D_MODEL), F32),
        scratch_shapes=[pltpu.VMEM((tm, D_MODEL), BF16), pltpu.VMEM((tm, D_MODEL), F32)],
        compiler_params=pltpu.CompilerParams(dimension_semantics=("arbitrary", "arbitrary"),
                                             vmem_limit_bytes=VMEM_LIMIT),
        name="ffn",
    )(x, gain, mod, w_in, w_in, w_out)


def _pad_rows(w, offset, total):
    pads = [(0, 0)] * w.ndim
    pads[-2] = (offset, total - offset - w.shape[-2])
    return jnp.pad(w, pads)


def kernel(x_prompt, x_sample, cache_k, cache_v, state_gla, state_rwkv, c, c_ctx, w_ada, b_ada, g_mix, g_ffn,
           w_in, q_gain, k_gain, gk_w2, gk_b, gla_gain, mu_c, w0, w2, a0, a2, g2, k_k, k_a, r_k, lnx_w, lnx_b,
           w_po_a, w_po_b, w_po_c, w_out, w_ffn_in, w_ffn_out):
    x = jnp.concatenate([x_prompt.reshape(N_TOK_CTX, D_MODEL), x_sample.reshape(N_TOK_LAT, D_MODEL)], axis=0)
    cond = jnp.concatenate([c_ctx[None, :], c, jnp.zeros((COND_ROWS - N_COND, D_MODEL), F32)], axis=0)
    mod = _modulation(cond, w_ada, b_ada)[:, :N_COND].reshape(DEPTH, N_COND, 6, 1, D_MODEL)

    wa = w_in[:, :, OFF_QA:OFF_QB].astype(BF16)
    wb = jnp.pad(w_in[:, :, OFF_QB:OFF_C], ((0, 0), (0, 0), (0, SEC_B - (OFF_C - OFF_QB)))).astype(BF16)
    wc = w_in[:, :, OFF_C:OFF_GATE].astype(BF16)
    wg = w_in[:, :, OFF_GATE:].astype(BF16)
    gk_w2p = jnp.stack([_pad_rows(gk_w2[:, 0], 0, LANES), _pad_rows(gk_w2[:, 1], GK_RANK, LANES)], axis=1)
    w2p = jnp.stack([_pad_rows(w2[:, 0], 0, LANES), _pad_rows(w2[:, 1], W_LORA, LANES)], axis=1)
    a2p = jnp.stack([_pad_rows(a2[:, 0], 0, LANES), _pad_rows(a2[:, 1], A_LORA, LANES)], axis=1)
    wpa, wpb, wpc, wout = (w.astype(BF16) for w in (w_po_a, w_po_b, w_po_c, w_out))
    wfi = w_ffn_in.astype(BF16)
    wfo = w_ffn_out.astype(BF16)
    rope = _rope_tables()
    lat_blk = N_TOK_CTX // DEC_SEQ

    ks, vs, sgs, srs = [], [], [], []
    for li in range(DEPTH):
        vec = lambda a: a[li].reshape(1, -1)
        pa, pb, pc, pg = _inproj(x, vec(g_mix), mod[li], wa[li], wb[li], wc[li], wg[li])

        qg = jnp.tile(vec(q_gain), (1, LANES // HD_A))
        kg = jnp.tile(vec(k_gain), (1, LANES // HD_A))
        oa_c, k_l, v_l = _attention(pa, qg, kg, BATCH, SEQ, 0)
        (oa_l,) = _attention(pa, qg, kg, DEC_BATCH, DEC_SEQ, lat_blk,
                             cache=(cache_k[:, li].reshape(DEC_BATCH, PAST_LEN, KV_A),
                                    cache_v[:, li].reshape(DEC_BATCH, PAST_LEN, KV_A)), rope=rope)

        gkb = gk_b[li].reshape(2, 1, WIDTH_BK)
        ob_c, sg_l = _gla(pb, gk_w2p[li], gkb, vec(gla_gain), BATCH, SEQ, 0, want_state=True)
        (ob_l,) = _gla(pb, gk_w2p[li], gkb, vec(gla_gain), DEC_BATCH, DEC_SEQ, lat_blk, s0=state_gla[:, li])

        rw_args = (pc, vec(mu_c), w0[li].reshape(2, 1, WIDTH_C), w2p[li], a0[li].reshape(2, 1, WIDTH_C), a2p[li],
                   g2[li], vec(k_k), vec(k_a), r_k[li].reshape(1, WIDTH_C), vec(lnx_w), vec(lnx_b))
        oc_c, sr_l = _rwkv(*rw_args, BATCH, SEQ, 0, want_state=True)
        (oc_l,) = _rwkv(*rw_args, DEC_BATCH, DEC_SEQ, lat_blk, s0=state_rwkv[:, li])

        oa = jnp.concatenate([oa_c, oa_l], axis=0)
        ob = jnp.concatenate([ob_c, ob_l], axis=0)
        oc = jnp.concatenate([oc_c, oc_l], axis=0)
        x = _merge(x, oa, ob, oc, pg, mod[li], wpa[li], wpb[li], wpc[li], wout[li])
        x = _ffn(x, vec(g_ffn), mod[li], wfi[li], wfo[li])

        ks.append(k_l.reshape(BATCH, SEQ, N_KV_A, HD_A))
        vs.append(v_l.reshape(BATCH, SEQ, N_KV_A, HD_A))
        sgs.append(sg_l)
        srs.append(sr_l)

    y_prompt = x[:N_TOK_CTX].reshape(BATCH, SEQ, D_MODEL)
    y_sample = x[N_TOK_CTX:].reshape(DEC_BATCH, DEC_SEQ, D_MODEL)
    return (y_prompt, y_sample, jnp.stack(ks, axis=1), jnp.stack(vs, axis=1),
            jnp.stack(sgs, axis=1), jnp.stack(srs, axis=1))
```

```python
import functools
import math

import numpy as np
import jax
import jax.numpy as jnp
from jax import lax
from jax.experimental import pallas as pl
from jax.experimental.pallas import tpu as pltpu

D_MODEL = 1024
BATCH = 16
SEQ = 256
DEPTH = 4
DEC_BATCH = 2
DEC_SEQ = 1024
PAST_LEN = 256
GRID_W = 64
N_HEADS_A = 8
N_KV_A = 2
HD_A = 64
ROPE_THETA = 10000.0
N_HEADS_B = 4
DK_B = 64
DV_B = 128
GK_RANK = 16
GK_NORMALIZER = 16.0
N_HEADS_C = 8
HD_C = 64
W_LORA = 64
A_LORA = 64
G_LORA = 128
GN_EPS = 64e-5
NORM_EPS = 1e-6
WIDTH_A = N_HEADS_A * HD_A
KV_A = N_KV_A * HD_A
WIDTH_BK = N_HEADS_B * DK_B
WIDTH_BV = N_HEADS_B * DV_B
WIDTH_C = N_HEADS_C * HD_C
N_BRANCH = 3
D_FF = ((8 * D_MODEL + 3 * 256 - 1) // (3 * 256)) * 256
OFF_QA = 0
OFF_KA = OFF_QA + WIDTH_A
OFF_VA = OFF_KA + KV_A
OFF_QB = OFF_VA + KV_A
OFF_KB = OFF_QB + WIDTH_BK
OFF_VB = OFF_KB + WIDTH_BK
OFF_GB = OFF_VB + WIDTH_BV
OFF_GKF = OFF_GB + WIDTH_BV
OFF_GKB = OFF_GKF + GK_RANK
OFF_C = OFF_GKB + GK_RANK
C_COLS = 3 * WIDTH_C + 2 * W_LORA + 2 * A_LORA + G_LORA
OFF_GATE = OFF_C + C_COLS
N_IN = OFF_GATE + N_BRANCH * D_MODEL

F32 = jnp.float32
BF16 = jnp.bfloat16

LANES = 128
VMEM_LIMIT = 56 * 1024 * 1024

N_TOK_CTX = BATCH * SEQ
N_TOK_LAT = DEC_BATCH * DEC_SEQ
N_TOK = N_TOK_CTX + N_TOK_LAT
N_COND = 1 + DEC_BATCH
COND_ROWS = 8
SEC_A = OFF_QB
SEC_B_MAIN = OFF_GKF - OFF_QB
SEC_B = SEC_B_MAIN + LANES
SEC_C = C_COLS
SEC_G = N_BRANCH * D_MODEL
TM = 512
TM_FFN = 1024
TF = D_FF // 2
GLA_CHUNK = 64
RW_CHUNK = 64
HI = lax.Precision.HIGHEST


def _tile_group(i, tm):
    ctx_tiles = N_TOK_CTX // tm
    lat_tiles = DEC_SEQ // tm
    return jnp.where(i < ctx_tiles, 0, 1 + (i - ctx_tiles) // lat_tiles)


def _bdot(a, b):
    return jnp.dot(a.astype(BF16), b.astype(BF16), preferred_element_type=F32)


def _dot_nt(a, b, precision=None):
    return lax.dot_general(a, b, (((1,), (1,)), ((), ())), preferred_element_type=F32, precision=precision)


def _dot_tn(a, b, precision=None):
    return lax.dot_general(a, b, (((0,), (0,)), ((), ())), preferred_element_type=F32, precision=precision)


def _sigmoid(x):
    return jax.nn.sigmoid(x)


def _modnorm(x, gain, shift, scale):
    ms = jnp.mean(x * x, axis=-1, keepdims=True)
    y = x * lax.rsqrt(ms + NORM_EPS) * gain
    return y * (1.0 + scale) + shift


def _mod_kernel(c_ref, w_ref, b_ref, o_ref):
    c = c_ref[...]
    s = c * _sigmoid(c)
    o_ref[...] = _bdot(s, w_ref[...]) + b_ref[...]


def _modulation(cond, w_ada, b_ada):
    tn = 1536
    n_out = 6 * D_MODEL
    return pl.pallas_call(
        _mod_kernel,
        grid=(DEPTH, n_out // tn),
        in_specs=[
            pl.BlockSpec((COND_ROWS, D_MODEL), lambda l, j: (0, 0)),
            pl.BlockSpec((None, D_MODEL, tn), lambda l, j: (l, 0, j)),
            pl.BlockSpec((None, 1, tn), lambda l, j: (l, 0, j)),
        ],
        out_specs=pl.BlockSpec((None, COND_ROWS, tn), lambda l, j: (l, 0, j)),
        out_shape=jax.ShapeDtypeStruct((DEPTH, COND_ROWS, n_out), F32),
        compiler_params=pltpu.CompilerParams(dimension_semantics=("arbitrary", "arbitrary")),
        name="modulation",
    )(cond, w_ada, b_ada.reshape(DEPTH, 1, n_out))


def _inproj_kernel(x_ref, g_ref, mod_ref, wa_ref, wb_ref, wc_ref, wg_ref, oa_ref, ob_ref, oc_ref, og_ref):
    h = _modnorm(x_ref[...], g_ref[...], mod_ref[0], mod_ref[1]).astype(BF16)
    oa_ref[...] = jnp.dot(h, wa_ref[...], preferred_element_type=F32)
    ob_ref[...] = jnp.dot(h, wb_ref[...], preferred_element_type=F32)
    oc_ref[...] = jnp.dot(h, wc_ref[...], preferred_element_type=F32)
    og_ref[...] = jnp.dot(h, wg_ref[...], preferred_element_type=F32)


def _resident(shape):
    nd = len(shape)
    return pl.BlockSpec(shape, lambda *_: (0,) * nd, pipeline_mode=pl.Buffered(1))


def _mod_spec(tm):
    return pl.BlockSpec((None, 6, 1, D_MODEL), lambda i, *_: (_tile_group(i, tm), 0, 0, 0))


def _inproj(x, gain, mod, wa, wb, wc, wg):
    tm = 256
    widths = (SEC_A, SEC_B, SEC_C, SEC_G)
    return pl.pallas_call(
        _inproj_kernel,
        grid=(N_TOK // tm,),
        in_specs=[
            pl.BlockSpec((tm, D_MODEL), lambda i: (i, 0)),
            _resident((1, D_MODEL)),
            _mod_spec(tm),
        ] + [_resident((D_MODEL, w)) for w in widths],
        out_specs=[pl.BlockSpec((tm, w), lambda i: (i, 0)) for w in widths],
        out_shape=[jax.ShapeDtypeStruct((N_TOK, w), F32) for w in widths],
        compiler_params=pltpu.CompilerParams(dimension_semantics=("arbitrary",), vmem_limit_bytes=VMEM_LIMIT),
        name="inproj",
    )(x, gain, mod, wa, wb, wc, wg)


def _segsum(x, width):
    n = x.shape[-1]
    gi = lax.broadcasted_iota(jnp.int32, (n, n), 0) // width
    gj = lax.broadcasted_iota(jnp.int32, (n, n), 1) // width
    e = jnp.where(gi == gj, 1.0, 0.0).astype(BF16)
    hi = x.astype(BF16)
    lo = (x - hi.astype(F32)).astype(BF16)
    return jnp.dot(jnp.concatenate([hi, lo], axis=1), jnp.concatenate([e, e], axis=0),
                   preferred_element_type=F32)


def _swap_pairs(x):
    lane = lax.broadcasted_iota(jnp.int32, x.shape, 1)
    nxt = pltpu.roll(x, LANES - 1, 1)
    prv = pltpu.roll(x, 1, 1)
    return jnp.where(lane % 2 == 0, nxt, prv)


def _head_norm(x, gain):
    ms = _segsum(x * x, HD_A) * (1.0 / HD_A)
    return x * lax.rsqrt(ms + NORM_EPS) * gain


def _attn_kernel(*refs, t, latent):
    if latent:
        (pa_ref, qg_ref, kg_ref, ck_ref, cv_ref, cos_ref, sin_ref, o_ref, k_scr, v_scr, q_scr) = refs
    else:
        (pa_ref, qg_ref, kg_ref, o_ref, ko_ref, vo_ref, k_scr, v_scr, q_scr) = refs
    tk = k_scr.shape[0]
    past = tk - t
    kn = _head_norm(pa_ref[:, OFF_KA:OFF_VA], kg_ref[...])
    va = pa_ref[:, OFF_VA:OFF_QB]
    if latent:
        cos = cos_ref[...]
        sin = sin_ref[...]
        kn = kn * cos + _swap_pairs(kn) * sin
        k_scr[0:past, :] = ck_ref[...]
        v_scr[0:past, :] = cv_ref[...]
    else:
        ko_ref[...] = kn
        vo_ref[...] = va
    k_scr[past:tk, :] = kn
    v_scr[past:tk, :] = va
    for c in range(WIDTH_A // LANES):
        qn = _head_norm(pa_ref[:, c * LANES:(c + 1) * LANES], qg_ref[...])
        if latent:
            qn = qn * cos + _swap_pairs(qn) * sin
        q_scr[:, c * LANES:(c + 1) * LANES] = qn * (HD_A ** -0.5)

    group = N_HEADS_A // N_KV_A
    tq = min(t, 256)
    for kv in range(N_KV_A):
        kh = k_scr[:, kv * HD_A:(kv + 1) * HD_A].astype(BF16)
        vh = v_scr[:, kv * HD_A:(kv + 1) * HD_A].astype(BF16)
        for qb in range(t // tq):
            outs = []
            for g in range(group):
                h = kv * group + g
                qh = q_scr[qb * tq:(qb + 1) * tq, h * HD_A:(h + 1) * HD_A].astype(BF16)
                s = _dot_nt(qh, kh)
                m = jnp.max(s, axis=-1, keepdims=True)
                e = jnp.exp(s - m)
                pv = jnp.dot(e.astype(BF16), vh, preferred_element_type=F32)
                outs.append(pv / jnp.sum(e, axis=-1, keepdims=True))
            for j in range(group // 2):
                pair = jnp.concatenate([outs[2 * j], outs[2 * j + 1]], axis=1)
                col = (kv * group + 2 * j) * HD_A
                o_ref[qb * tq:(qb + 1) * tq, col:col + LANES] = pair


def _attention(pa, q_gain, k_gain, n_seq, t, row_block0, cache=None, rope=None):
    latent = cache is not None
    tk = t + (PAST_LEN if latent else 0)
    in_specs = [
        pl.BlockSpec((t, SEC_A), lambda s: (row_block0 + s, 0)),
        _resident((1, LANES)),
        _resident((1, LANES)),
    ]
    args = [pa, q_gain, k_gain]
    out_specs = [pl.BlockSpec((t, WIDTH_A), lambda s: (s, 0))]
    out_shape = [jax.ShapeDtypeStruct((n_seq * t, WIDTH_A), F32)]
    if latent:
        in_specs += [
            pl.BlockSpec((None, PAST_LEN, KV_A), lambda s: (s, 0, 0)),
            pl.BlockSpec((None, PAST_LEN, KV_A), lambda s: (s, 0, 0)),
            _resident((t, LANES)),
            _resident((t, LANES)),
        ]
        args += [cache[0], cache[1], rope[0], rope[1]]
    else:
        out_specs += [pl.BlockSpec((None, t, KV_A), lambda s: (s, 0, 0))] * 2
        out_shape += [jax.ShapeDtypeStruct((n_seq, t, KV_A), F32)] * 2
    return pl.pallas_call(
        functools.partial(_attn_kernel, t=t, latent=latent),
        grid=(n_seq,),
        in_specs=in_specs,
        out_specs=out_specs,
        out_shape=out_shape,
        scratch_shapes=[
            pltpu.VMEM((tk, KV_A), F32),
            pltpu.VMEM((tk, KV_A), F32),
            pltpu.VMEM((t, WIDTH_A), F32),
        ],
        compiler_params=pltpu.CompilerParams(dimension_semantics=("arbitrary",), vmem_limit_bytes=VMEM_LIMIT),
        name="attn_lat" if latent else "attn_ctx",
    )(*args)


def _rope_tables():
    tpos = np.arange(DEC_SEQ)
    row = (tpos // GRID_W).astype(np.float32)
    col = (tpos % GRID_W).astype(np.float32)
    n_pairs = HD_A // 4
    inv = (ROPE_THETA ** (-np.arange(n_pairs, dtype=np.float32) / n_pairs)).astype(np.float32)
    ang = np.concatenate([row[:, None] * inv, col[:, None] * inv], axis=-1)
    cos = np.repeat(np.cos(ang), 2, axis=-1)
    sin = np.repeat(np.sin(ang), 2, axis=-1)
    sign = np.tile(np.array([-1.0, 1.0], np.float32), HD_A // 2)
    reps = LANES // HD_A
    return (jnp.asarray(np.tile(cos, (1, reps)), F32), jnp.asarray(np.tile(sin * sign, (1, reps)), F32))


def _tri(n, reverse, strict):
    i = lax.broadcasted_iota(jnp.int32, (n, n), 0)
    j = lax.broadcasted_iota(jnp.int32, (n, n), 1)
    if reverse:
        keep = (j > i) if strict else (j >= i)
    else:
        keep = (j < i) if strict else (j <= i)
    return keep


def _chunk_rows(index, c):
    if isinstance(index, int):
        return pl.ds(index * c, c)
    return pl.ds(pl.multiple_of(index * c, c), c)


def _head_masks(shape):
    lane = lax.broadcasted_iota(jnp.int32, shape, 1)
    first = lane < (LANES // 2)
    return first, jnp.logical_not(first)


def _log_sigmoid(z):
    return jnp.minimum(z, 0.0) - jnp.log1p(jnp.exp(-jnp.abs(z)))


def _gla_kernel(*refs, t, has_init, want_state):
    refs = list(refs)
    (q_ref, k_ref, v_ref, g_ref, lr_ref, w2_ref, gb_ref, gain_ref,
     cum_ref, incl_ref, nmask_ref) = refs[:11]
    pos = 11
    if has_init:
        s0_ref = refs[pos]
        pos += 1
    o_ref = refs[pos]
    pos += 1
    if want_state:
        so_ref = refs[pos]
        pos += 1
    lg_scr, o_scr, qh_scr, n_scr, dec_scr, st_scr = refs[pos:pos + 6]

    c = GLA_CHUNK
    n_chunks = t // c
    half = LANES // 2
    jj = lax.broadcasted_iota(jnp.int32, (half, LANES), 0)
    ll = lax.broadcasted_iota(jnp.int32, (half, LANES), 1)
    sel = [jnp.where(ll == jj + h * half, 1.0, 0.0).astype(F32) for h in range(2)]

    lr = lr_ref[...].astype(BF16)
    for d in range(2):
        z = jnp.dot(lr, w2_ref[d].astype(BF16), preferred_element_type=F32) + gb_ref[d]
        lg_scr[d] = _log_sigmoid(z) * (1.0 / GK_NORMALIZER)
        for h in range(2):
            if has_init:
                st_scr[d, h * DV_B:(h + 1) * DV_B, :] = _dot_tn(s0_ref[d, h], sel[h], precision=HI)
            else:
                st_scr[d, h * DV_B:(h + 1) * DV_B, :] = jnp.zeros((DV_B, LANES), F32)

    m0, m1 = _head_masks((c, LANES))
    scale = DK_B ** -0.5

    def chunk_terms(i, carry):
        rows_f = _chunk_rows(i, c)
        rows_b = _chunk_rows(n_chunks - 1 - i, c)
        q = jnp.concatenate([q_ref[rows_f, :], q_ref[rows_b, :]], axis=0)
        k = jnp.concatenate([k_ref[rows_f, :], k_ref[rows_b, :]], axis=0)
        g = jnp.concatenate([lg_scr[0, rows_f, :], lg_scr[1, rows_b, :]], axis=0)
        b = _mm_exact_lhs(cum_ref[...], g)
        bm = jnp.concatenate([jnp.broadcast_to(b[c // 2:c // 2 + 1], (c, LANES)),
                              jnp.broadcast_to(b[c + c // 2:c + c // 2 + 1], (c, LANES))], axis=0)
        bl = jnp.concatenate([jnp.broadcast_to(b[c - 1:c], (c, LANES)),
                              jnp.broadcast_to(b[c:c + 1], (c, LANES))], axis=0)
        e1 = jnp.exp(b - bm)
        e2 = jnp.exp(bm - b)
        qt = q * scale * e1
        kt = k * e2
        qh_scr[i] = qt * jnp.exp(bm)
        kc = kt * jnp.exp(bl - bm)
        qf, qb = qt[0:c], qt[c:2 * c]
        lhs = jnp.concatenate([jnp.where(m0, qf, 0.0), jnp.where(m1, qf, 0.0),
                               jnp.where(m0, qb, 0.0), jnp.where(m1, qb, 0.0)], axis=0)
        rhs = jnp.concatenate([kt[0:c], kt[0:c], kt[c:2 * c], kt[c:2 * c]], axis=0)
        att = _mm(lhs, rhs, "nt", GLA_P_ATT) * incl_ref[...]
        vf = v_ref[rows_f, :]
        vb = v_ref[rows_b, :]
        vst = jnp.concatenate([vf[:, 0:DV_B], vf[:, DV_B:2 * DV_B], vb[:, 0:DV_B], vb[:, DV_B:2 * DV_B]], axis=0)
        o_scr[i] = _mm(att, vst, "nn", 1)
        n_scr[i, 0] = _mm(vf, kc[0:c], "tn", 1) * nmask_ref[...]
        n_scr[i, 1] = _mm(vb, kc[c:2 * c], "tn", 1) * nmask_ref[...]
        dec = jnp.exp(bl)
        dec_scr[i, 0] = dec[0:8]
        dec_scr[i, 1] = dec[c:c + 8]
        return carry

    if n_chunks <= GLA_UNROLL:
        for i in range(n_chunks):
            chunk_terms(i, 0)
    else:
        lax.fori_loop(0, n_chunks, chunk_terms, 0, unroll=GLA_UNROLL)

    def chunk_scan(i, carry):
        for d in range(2):
            st = st_scr[d]
            oi = _mm(qh_scr[i, d * c:(d + 1) * c, :], st, "nt", 1)
            for h in range(2):
                r0 = (2 * d + h) * c
                o_scr[i, r0:r0 + c, :] = o_scr[i, r0:r0 + c, :] + oi[:, h * DV_B:(h + 1) * DV_B]
            st_scr[d] = st * dec_scr[i, d, 0:1, :] + n_scr[i, d]
        return carry

    lax.fori_loop(0, n_chunks, chunk_scan, 0)

    for h in range(2):
        o = jnp.concatenate([o_scr[j, h * c:(h + 1) * c, :] + o_scr[n_chunks - 1 - j, (2 + h) * c:(3 + h) * c, :]
                             for j in range(n_chunks)], axis=0)
        ms = jnp.mean(o * o, axis=-1, keepdims=True)
        o = o * lax.rsqrt(ms + NORM_EPS) * gain_ref[...]
        gate = g_ref[:, h * DV_B:(h + 1) * DV_B]
        o_ref[:, h * DV_B:(h + 1) * DV_B] = o * (gate * _sigmoid(gate))
        if want_state:
            for d in range(2):
                so_ref[d, h] = _dot_nt(sel[h], st_scr[d, h * DV_B:(h + 1) * DV_B, :], precision=HI)


def _gla(pb, w2pad, gk_b, gain, n_seq, t, row_block0, s0=None, want_state=False):
    has_init = s0 is not None
    pair_w = 2 * DV_B
    state_spec = pl.BlockSpec((None, 2, 2, DK_B, DV_B), lambda s, p: (s, 0, p, 0, 0))
    in_specs = [
        pl.BlockSpec((t, LANES), lambda s, p: (row_block0 + s, p)),
        pl.BlockSpec((t, LANES), lambda s, p: (row_block0 + s, WIDTH_BK // LANES + p)),
        pl.BlockSpec((t, pair_w), lambda s, p: (row_block0 + s, (2 * WIDTH_BK) // pair_w + p)),
        pl.BlockSpec((t, pair_w), lambda s, p: (row_block0 + s, (2 * WIDTH_BK + WIDTH_BV) // pair_w + p)),
        pl.BlockSpec((t, LANES), lambda s, p: (row_block0 + s, SEC_B_MAIN // LANES)),
        pl.BlockSpec((2, LANES, LANES), lambda s, p: (0, 0, p)),
        pl.BlockSpec((2, 1, LANES), lambda s, p: (0, 0, p)),
        pl.BlockSpec((1, DV_B), lambda s, p: (0, 0)),
    ]
    args = [pb, pb, pb, pb, pb, w2pad, gk_b, gain]
    cum, _, incl, _, _ = _rwkv_chunk_constants()
    ri = np.arange(pair_w)[:, None] // DV_B
    li = np.arange(LANES)[None, :] // DK_B
    consts = (cum, incl, jnp.asarray((ri == li).astype(np.float32)))
    in_specs += [pl.BlockSpec(m.shape, lambda s, p: (0, 0)) for m in consts]
    args += list(consts)
    if has_init:
        in_specs.append(state_spec)
        args.append(s0)
    n_steps = t // GLA_CHUNK
    out_specs = [pl.BlockSpec((t, pair_w), lambda s, p: (s, p))]
    out_shape = [jax.ShapeDtypeStruct((n_seq * t, WIDTH_BV), F32)]
    if want_state:
        out_specs.append(state_spec)
        out_shape.append(jax.ShapeDtypeStruct((n_seq, 2, N_HEADS_B, DK_B, DV_B), F32))
    return pl.pallas_call(
        functools.partial(_gla_kernel, t=t, has_init=has_init, want_state=want_state),
        grid=(n_seq, N_HEADS_B // 2),
        in_specs=in_specs,
        out_specs=out_specs,
        out_shape=out_shape,
        scratch_shapes=[
            pltpu.VMEM((2, t, LANES), F32),
            pltpu.VMEM((n_steps, 4 * GLA_CHUNK, DV_B), F32),
            pltpu.VMEM((n_steps, 2 * GLA_CHUNK, LANES), F32),
            pltpu.VMEM((n_steps, 2, pair_w, LANES), F32),
            pltpu.VMEM((n_steps, 2, 8, LANES), F32),
            pltpu.VMEM((2, pair_w, LANES), F32),
        ],
        compiler_params=pltpu.CompilerParams(dimension_semantics=("arbitrary", "arbitrary")),
        name="gla_lat" if has_init else "gla_ctx",
    )(*args)


def _shift_mix(u, mu):
    t = u.shape[0]
    row = lax.broadcasted_iota(jnp.int32, u.shape, 0)
    prev = jnp.where(row == 0, 0.0, pltpu.roll(u, 1, 0))
    nxt = jnp.where(row == t - 1, 0.0, pltpu.roll(u, t - 1, 0))
    return u + mu * (0.5 * (prev + nxt) - u)


_DIMS = {"nn": (((1,), (0,)), ((), ())), "nt": (((1,), (1,)), ((), ())), "tn": (((0,), (0,)), ((), ()))}


def _split_bf16(x):
    hi = x.astype(BF16)
    lo = (x - hi.astype(F32)).astype(BF16)
    return hi, lo


def _mm(a, b, form, passes):
    dims = _DIMS[form]
    if passes == 6:
        return lax.dot_general(a, b, dims, preferred_element_type=F32, precision=HI)
    if passes == 1:
        return lax.dot_general(a.astype(BF16), b.astype(BF16), dims, preferred_element_type=F32)
    a_hi, a_lo = _split_bf16(a)
    b_hi, b_lo = _split_bf16(b)
    ka = dims[0][0][0]
    kb = dims[0][1][0]
    a_cat = jnp.concatenate([a_hi, a_lo, a_hi], axis=ka)
    b_cat = jnp.concatenate([b_hi, b_hi, b_lo], axis=kb)
    return lax.dot_general(a_cat, b_cat, dims, preferred_element_type=F32)


def _mm_exact_lhs(a, x):
    a16 = a.astype(BF16)
    x_hi, x_lo = _split_bf16(x)
    return jnp.dot(jnp.concatenate([a16, a16], axis=1), jnp.concatenate([x_hi, x_lo], axis=0),
                   preferred_element_type=F32)


GLA_P_ATT = 1
GLA_UNROLL = 4

RW_UNROLL = 4


def _half_neumann(n_mat, eye, c):
    sw = n_mat.shape[0]
    p = eye + n_mat
    m = _mm(n_mat, n_mat, "nn", 1)
    for _ in range(int(math.log2(c)) - 3):
        both = _mm(jnp.concatenate([m, p], axis=0), m, "nn", 1)
        m = both[0:sw]
        p = p + both[sw:2 * sw]
    return p + _mm(p, m, "nn", 1)


def _rwkv_kernel(*refs, t, has_init, want_state):
    refs = list(refs)
    (r_ref, k_ref, v_ref, wl_ref, al_ref, gl_ref,
     mur_ref, muk_ref, muv_ref, muw_ref, mua_ref, mug_ref,
     w0_ref, w2_ref, a0_ref, a2_ref, g2_ref, kk_ref, ka_ref, rk_ref, lw_ref, lb_ref,
     cum_ref, strict_ref, incl_ref, blk_ref, eye_ref) = refs[:27]
    pos = 27
    if has_init:
        s0_ref = refs[pos]
        pos += 1
    o_ref = refs[pos]
    pos += 1
    if want_state:
        so_ref = refs[pos]
        pos += 1
    r_scr, v_scr, a_scr, dir_scr, y_scr, rp_scr, m_scr, n_scr, dec_scr, st_scr = refs[pos:pos + 10]

    c = RW_CHUNK
    n_chunks = t // c
    half = LANES // 2

    r = _shift_mix(r_ref[...], mur_ref[...])
    kc = _shift_mix(k_ref[...], muk_ref[...])
    vc = _shift_mix(v_ref[...], muv_ref[...])
    wl_in = jnp.tanh(_shift_mix(wl_ref[...], muw_ref[...]))
    al_in = _shift_mix(al_ref[...], mua_ref[...])
    g_in = _sigmoid(_shift_mix(gl_ref[...], mug_ref[...]))

    kkf = kc * kk_ref[...]
    kk = kkf * lax.rsqrt(_segsum(kkf * kkf, HD_C) + 1e-12)
    r_scr[...] = r
    v_scr[...] = vc
    a_scr[...] = -kk
    bonus = jnp.zeros_like(r)
    for d in range(2):
        wl = w0_ref[d] + _bdot(wl_in, w2_ref[d])
        dir_scr[d, 0] = -math.exp(-0.5) * _sigmoid(wl)
        a = _sigmoid(a0_ref[d] + _bdot(al_in, a2_ref[d]))
        kd = kc * (1.0 + (a - 1.0) * ka_ref[...])
        dir_scr[d, 1] = kd
        dir_scr[d, 2] = kk * a
        bonus = bonus + _segsum(r * kd * rk_ref[...], HD_C)
    bonus = bonus * vc
    g_c = _bdot(g_in, g2_ref[...])

    nblk = 2 * (LANES // HD_C)
    sw = nblk * HD_C
    if has_init:
        z = jnp.zeros((HD_C, HD_C), F32)
        blocks = [s0_ref[0, 0], s0_ref[0, 1], s0_ref[1, 0], s0_ref[1, 1]]
        st_scr[...] = jnp.concatenate(
            [jnp.concatenate([blocks[i] if i == j else z for j in range(nblk)], axis=1) for i in range(nblk)], axis=0)
    else:
        st_scr[...] = jnp.zeros((sw, sw), F32)

    m0, m1 = _head_masks((c, LANES))

    def per_head(x):
        xf, xb = x[0:c], x[c:2 * c]
        return jnp.concatenate([jnp.where(m0, xf, 0.0), jnp.where(m1, xf, 0.0),
                                jnp.where(m0, xb, 0.0), jnp.where(m1, xb, 0.0)], axis=0)

    def per_block(x):
        xf, xb = x[0:c], x[c:2 * c]
        return jnp.concatenate([xf, xf, xb, xb], axis=0)

    def side_by_side(x):
        return jnp.concatenate([x[0:c], x[c:2 * c]], axis=1)

    def heads_select(x):
        return jnp.concatenate([jnp.where(m0, x[0:c], x[c:2 * c]), jnp.where(m0, x[2 * c:3 * c], x[3 * c:4 * c])],
                               axis=1)

    def heads_add(x):
        return jnp.concatenate([x[0:c] + x[c:2 * c], x[2 * c:3 * c] + x[3 * c:4 * c]], axis=1)

    def chunk_terms(i, carry):
        rows_f = _chunk_rows(i, c)
        rows_b = _chunk_rows(n_chunks - 1 - i, c)

        def both(ref_f, ref_b):
            return jnp.concatenate([ref_f[rows_f, :], ref_b[rows_b, :]], axis=0)

        rr = both(r_scr, r_scr)
        vv = both(v_scr, v_scr)
        aa = both(a_scr, a_scr)
        lw = both(dir_scr.at[0, 0], dir_scr.at[1, 0])
        kd = both(dir_scr.at[0, 1], dir_scr.at[1, 1])
        bb = both(dir_scr.at[0, 2], dir_scr.at[1, 2])
        cs = _mm_exact_lhs(cum_ref[...], lw)
        mid = lambda x: jnp.concatenate([jnp.broadcast_to(x[c // 2:c // 2 + 1], (c, LANES)),
                                         jnp.broadcast_to(x[c + c // 2:c + c // 2 + 1], (c, LANES))], axis=0)
        last = lambda x: jnp.concatenate([jnp.broadcast_to(x[c - 1:c], (c, LANES)),
                                          jnp.broadcast_to(x[c:c + 1], (c, LANES))], axis=0)
        cm = mid(cs)
        cl = last(cs)
        e1 = jnp.exp(cs - cm)
        e2 = jnp.exp(cm - cs)
        ecm = jnp.exp(cm)
        ecl = jnp.exp(cl - cm)
        rt = rr * e1
        at = aa * e1 * jnp.exp(-lw)
        bt = bb * e2
        kt = kd * e2
        bc = bt * ecl
        kcl = kt * ecl
        lhs = jnp.concatenate([per_head(at), per_head(rt)], axis=0)
        gram = _mm(lhs, jnp.concatenate([per_block(bt), per_block(kt)], axis=0), "nt", 1)
        a_ab = gram[0:sw, 0:sw] * strict_ref[...]
        a_rb = gram[sw:2 * sw, 0:sw] * incl_ref[...]
        a_ak = gram[0:sw, sw:2 * sw] * strict_ref[...]
        a_rk = gram[sw:2 * sw, sw:2 * sw] * incl_ref[...]
        p_half = _half_neumann(a_ab, eye_ref[...], c)
        vst = per_block(vv)
        av = _mm(a_ak, vst, "nn", 1)
        rhs = jnp.concatenate([av, per_head(at * ecm)], axis=1)
        wa = _mm(p_half, rhs, "nn", 1)
        res = rhs - wa + _mm(a_ab, wa, "nn", 3)
        wa = wa + _mm(p_half, res, "nn", 1)
        ya = _mm(jnp.concatenate([a_rb, a_rk], axis=1),
                 jnp.concatenate([wa, jnp.concatenate([vst, jnp.zeros_like(vst)], axis=1)], axis=0), "nn", 1)
        rp = per_head(rt * ecm) + ya[:, LANES:2 * LANES]
        w1 = heads_select(wa[:, 0:LANES])
        ap = heads_add(wa[:, LANES:2 * LANES])
        y_scr[i] = heads_select(ya[:, 0:LANES])
        rp_scr[i] = heads_add(rp)
        vv2 = side_by_side(vv)
        lhs_t = jnp.concatenate([jnp.concatenate([ap, w1], axis=1),
                                 jnp.concatenate([jnp.zeros_like(vv2), vv2], axis=1)], axis=0)
        mn = _mm(lhs_t, jnp.concatenate([side_by_side(bc), side_by_side(kcl)], axis=0), "tn", 1)
        m_scr[i] = mn[0:sw] * blk_ref[...]
        n_scr[i] = mn[sw:2 * sw] * blk_ref[...]
        dec_scr[i] = jnp.broadcast_to(side_by_side(jnp.exp(cl))[0:1], (8, 2 * LANES))
        return carry

    if n_chunks <= RW_UNROLL:
        for i in range(n_chunks):
            chunk_terms(i, 0)
    else:
        lax.fori_loop(0, n_chunks, chunk_terms, 0, unroll=RW_UNROLL)

    def chunk_scan(i, carry):
        st = st_scr[...]
        y_scr[i] = y_scr[i] + _mm(rp_scr[i], st, "nt", 1)
        st_scr[...] = st * dec_scr[i, 0:1, :] + _mm(st, m_scr[i], "nn", 1) + n_scr[i]
        return carry

    lax.fori_loop(0, n_chunks, chunk_scan, 0)

    y = jnp.concatenate([y_scr[j, :, 0:LANES] + y_scr[n_chunks - 1 - j, :, LANES:2 * LANES]
                         for j in range(n_chunks)], axis=0)
    mean = _segsum(y, HD_C) * (1.0 / HD_C)
    yc = y - mean
    var = _segsum(yc * yc, HD_C) * (1.0 / HD_C)
    gn = yc * lax.rsqrt(var + GN_EPS) * lw_ref[...] + lb_ref[...]
    o_ref[...] = (gn + bonus) * g_c
    if want_state:
        for d in range(2):
            for h in range(2):
                b0 = (2 * d + h) * HD_C
                so_ref[d, h] = st_scr[b0:b0 + HD_C, b0:b0 + HD_C]


def _rwkv_chunk_constants():
    c = RW_CHUNK
    sw = 2 * LANES
    i = np.arange(sw)[:, None]
    j = np.arange(sw)[None, :]
    same = (i // c) == (j // c)
    bwd = i >= LANES
    strict = same & np.where(bwd, j > i, j < i)
    incl = same & np.where(bwd, j >= i, j <= i)
    blk = (i // HD_C) == (j // HD_C)
    ci = np.arange(2 * c)[:, None]
    cj = np.arange(2 * c)[None, :]
    cum = ((ci // c) == (cj // c)) & np.where(ci >= c, cj >= ci, cj <= ci)
    f = lambda m: jnp.asarray(m.astype(np.float32))
    return f(cum), f(strict), f(incl), f(blk), f(np.eye(sw))


def _rwkv(pc, mu_c, w0, w2pad, a0, a2pad, g2, k_k, k_a, r_k, lnx_w, lnx_b, n_seq, t, row_block0,
          s0=None, want_state=False):
    has_init = s0 is not None
    n_pairs = WIDTH_C // LANES
    blk_w = 3 * n_pairs
    blk_a = blk_w + 1
    blk_g = blk_w + 2
    state_spec = pl.BlockSpec((None, 2, 2, HD_C, HD_C), lambda s, p: (s, 0, p, 0, 0))

    def sec(col_fn):
        return pl.BlockSpec((t, LANES), lambda s, p: (row_block0 + s, col_fn(p)))

    def vec(col_fn):
        return pl.BlockSpec((1, LANES), lambda s, p: (0, col_fn(p)))

    col_fns = [lambda p: p, lambda p: n_pairs + p, lambda p: 2 * n_pairs + p,
               lambda p: blk_w, lambda p: blk_a, lambda p: blk_g]
    in_specs = [sec(f) for f in col_fns] + [vec(f) for f in col_fns]
    in_specs += [
        pl.BlockSpec((2, 1, LANES), lambda s, p: (0, 0, p)),
        pl.BlockSpec((2, LANES, LANES), lambda s, p: (0, 0, p)),
        pl.BlockSpec((2, 1, LANES), lambda s, p: (0, 0, p)),
        pl.BlockSpec((2, LANES, LANES), lambda s, p: (0, 0, p)),
        pl.BlockSpec((LANES, LANES), lambda s, p: (0, p)),
    ] + [vec(lambda p: p)] * 5
    args = [pc] * 6 + [mu_c] * 6 + [w0, w2pad, a0, a2pad, g2, k_k, k_a, r_k, lnx_w, lnx_b]
    consts = _rwkv_chunk_constants()
    in_specs += [pl.BlockSpec(m.shape, lambda s, p: (0, 0)) for m in consts]
    args += list(consts)
    if has_init:
        in_specs.append(state_spec)
        args.append(s0)
    out_specs = [pl.BlockSpec((t, LANES), lambda s, p: (s, p))]
    out_shape = [jax.ShapeDtypeStruct((n_seq * t, WIDTH_C), F32)]
    if want_state:
        out_specs.append(state_spec)
        out_shape.append(jax.ShapeDtypeStruct((n_seq, 2, N_HEADS_C, HD_C, HD_C), F32))
    n_steps = t // RW_CHUNK
    sw = 2 * LANES
    return pl.pallas_call(
        functools.partial(_rwkv_kernel, t=t, has_init=has_init, want_state=want_state),
        grid=(n_seq, n_pairs),
        in_specs=in_specs,
        out_specs=out_specs,
        out_shape=out_shape,
        scratch_shapes=[
            pltpu.VMEM((t, LANES), F32),
            pltpu.VMEM((t, LANES), F32),
            pltpu.VMEM((t, LANES), F32),
            pltpu.VMEM((2, 3, t, LANES), F32),
            pltpu.VMEM((n_steps, RW_CHUNK, sw), F32),
            pltpu.VMEM((n_steps, RW_CHUNK, sw), F32),
            pltpu.VMEM((n_steps, sw, sw), F32),
            pltpu.VMEM((n_steps, sw, sw), F32),
            pltpu.VMEM((n_steps, 8, sw), F32),
            pltpu.VMEM((sw, sw), F32),
        ],
        compiler_params=pltpu.CompilerParams(dimension_semantics=("arbitrary", "arbitrary")),
        name="rwkv_lat" if has_init else "rwkv_ctx",
    )(*args)


def _merge_kernel(x_ref, oa_ref, ob_ref, oc_ref, pg_ref, mod_ref, wa_ref, wb_ref, wc_ref, wo_ref, o_ref):
    merged = jnp.zeros(x_ref.shape, F32)
    for j, (b_ref, w_ref) in enumerate(((oa_ref, wa_ref), (ob_ref, wb_ref), (oc_ref, wc_ref))):
        gate = _sigmoid(pg_ref[:, j * D_MODEL:(j + 1) * D_MODEL])
        merged = merged + gate * jnp.dot(b_ref[...].astype(BF16), w_ref[...], preferred_element_type=F32)
    y = jnp.dot(merged.astype(BF16), wo_ref[...], preferred_element_type=F32)
    o_ref[...] = x_ref[...] + mod_ref[2] * y


def _merge(x, oa, ob, oc, pg, mod, wpa, wpb, wpc, wout):
    tm = TM
    row = lambda w: pl.BlockSpec((tm, w), lambda i: (i, 0))
    return pl.pallas_call(
        _merge_kernel,
        grid=(N_TOK // tm,),
        in_specs=[row(D_MODEL), row(WIDTH_A), row(WIDTH_BV), row(WIDTH_C), row(SEC_G), _mod_spec(tm),
                  _resident((WIDTH_A, D_MODEL)), _resident((WIDTH_BV, D_MODEL)),
                  _resident((WIDTH_C, D_MODEL)), _resident((D_MODEL, D_MODEL))],
        out_specs=row(D_MODEL),
        out_shape=jax.ShapeDtypeStruct((N_TOK, D_MODEL), F32),
        compiler_params=pltpu.CompilerParams(dimension_semantics=("arbitrary",), vmem_limit_bytes=VMEM_LIMIT),
        name="merge",
    )(x, oa, ob, oc, pg, mod, wpa, wpb, wpc, wout)


def _ffn_kernel(x_ref, g_ref, mod_ref, wg_ref, wu_ref, wo_ref, o_ref, h_scr, acc_scr):
    f = pl.program_id(1)

    @pl.when(f == 0)
    def _():
        h_scr[...] = _modnorm(x_ref[...], g_ref[...], mod_ref[3], mod_ref[4]).astype(BF16)
        acc_scr[...] = jnp.zeros_like(acc_scr)

    h = h_scr[...]
    gate = jnp.dot(h, wg_ref[...], preferred_element_type=F32)
    up = jnp.dot(h, wu_ref[...], preferred_element_type=F32)
    act = (gate * _sigmoid(gate) * up).astype(BF16)
    acc_scr[...] += jnp.dot(act, wo_ref[...], preferred_element_type=F32)

    @pl.when(f == pl.num_programs(1) - 1)
    def _():
        o_ref[...] = x_ref[...] + mod_ref[5] * acc_scr[...]


def _ffn(x, gain, mod, w_in, w_out):
    tm = TM_FFN
    nf = D_FF // TF
    return pl.pallas_call(
        _ffn_kernel,
        grid=(N_TOK // tm, nf),
        in_specs=[
            pl.BlockSpec((tm, D_MODEL), lambda i, f: (i, 0)),
            pl.BlockSpec((1, D_MODEL), lambda i, f: (0, 0)),
            _mod_spec(tm),
            pl.BlockSpec((D_MODEL, TF), lambda i, f: (0, f)),
            pl.BlockSpec((D_MODEL, TF), lambda i, f: (0, nf + f)),
            pl.BlockSpec((TF, D_MODEL), lambda i, f: (f, 0)),
        ],
        out_specs=pl.BlockSpec((tm, D_MODEL), lambda i, f: (i, 0)),
        out_shape=jax.ShapeDtypeStruct((N_TOK, D_MODEL), F32),
        scratch_shapes=[pltpu.VMEM((tm, D_MODEL), BF16), pltpu.VMEM((tm, D_MODEL), F32)],
        compiler_params=pltpu.CompilerParams(dimension_semantics=("arbitrary", "arbitrary"),
                                             vmem_limit_bytes=VMEM_LIMIT),
        name="ffn",
    )(x, gain, mod, w_in, w_in, w_out)


def _pad_rows(w, offset, total):
    pads = [(0, 0)] * w.ndim
    pads[-2] = (offset, total - offset - w.shape[-2])
    return jnp.pad(w, pads)


def kernel(x_prompt, x_sample, cache_k, cache_v, state_gla, state_rwkv, c, c_ctx, w_ada, b_ada, g_mix, g_ffn,
           w_in, q_gain, k_gain, gk_w2, gk_b, gla_gain, mu_c, w0, w2, a0, a2, g2, k_k, k_a, r_k, lnx_w, lnx_b,
           w_po_a, w_po_b, w_po_c, w_out, w_ffn_in, w_ffn_out):
    x = jnp.concatenate([x_prompt.reshape(N_TOK_CTX, D_MODEL), x_sample.reshape(N_TOK_LAT, D_MODEL)], axis=0)
    cond = jnp.concatenate([c_ctx[None, :], c, jnp.zeros((COND_ROWS - N_COND, D_MODEL), F32)], axis=0)
    mod = _modulation(cond, w_ada, b_ada)[:, :N_COND].reshape(DEPTH, N_COND, 6, 1, D_MODEL)

    wa = w_in[:, :, OFF_QA:OFF_QB].astype(BF16)
    wb = jnp.pad(w_in[:, :, OFF_QB:OFF_C], ((0, 0), (0, 0), (0, SEC_B - (OFF_C - OFF_QB)))).astype(BF16)
    wc = w_in[:, :, OFF_C:OFF_GATE].astype(BF16)
    wg = w_in[:, :, OFF_GATE:].astype(BF16)
    gk_w2p = jnp.stack([_pad_rows(gk_w2[:, 0], 0, LANES), _pad_rows(gk_w2[:, 1], GK_RANK, LANES)], axis=1)
    w2p = jnp.stack([_pad_rows(w2[:, 0], 0, LANES), _pad_rows(w2[:, 1], W_LORA, LANES)], axis=1)
    a2p = jnp.stack([_pad_rows(a2[:, 0], 0, LANES), _pad_rows(a2[:, 1], A_LORA, LANES)], axis=1)
    wpa, wpb, wpc, wout = (w.astype(BF16) for w in (w_po_a, w_po_b, w_po_c, w_out))
    wfi = w_ffn_in.astype(BF16)
    wfo = w_ffn_out.astype(BF16)
    rope = _rope_tables()
    lat_blk = N_TOK_CTX // DEC_SEQ

    ks, vs, sgs, srs = [], [], [], []
    for li in range(DEPTH):
        vec = lambda a: a[li].reshape(1, -1)
        pa, pb, pc, pg = _inproj(x, vec(g_mix), mod[li], wa[li], wb[li], wc[li], wg[li])

        qg = jnp.tile(vec(q_gain), (1, LANES // HD_A))
        kg = jnp.tile(vec(k_gain), (1, LANES // HD_A))
        oa_c, k_l, v_l = _attention(pa, qg, kg, BATCH, SEQ, 0)
        (oa_l,) = _attention(pa, qg, kg, DEC_BATCH, DEC_SEQ, lat_blk,
                             cache=(cache_k[:, li].reshape(DEC_BATCH, PAST_LEN, KV_A),
                                    cache_v[:, li].reshape(DEC_BATCH, PAST_LEN, KV_A)), rope=rope)

        gkb = gk_b[li].reshape(2, 1, WIDTH_BK)
        ob_c, sg_l = _gla(pb, gk_w2p[li], gkb, vec(gla_gain), BATCH, SEQ, 0, want_state=True)
        (ob_l,) = _gla(pb, gk_w2p[li], gkb, vec(gla_gain), DEC_BATCH, DEC_SEQ, lat_blk, s0=state_gla[:, li])

        rw_args = (pc, vec(mu_c), w0[li].reshape(2, 1, WIDTH_C), w2p[li], a0[li].reshape(2, 1, WIDTH_C), a2p[li],
                   g2[li], vec(k_k), vec(k_a), r_k[li].reshape(1, WIDTH_C), vec(lnx_w), vec(lnx_b))
        oc_c, sr_l = _rwkv(*rw_args, BATCH, SEQ, 0, want_state=True)
        (oc_l,) = _rwkv(*rw_args, DEC_BATCH, DEC_SEQ, lat_blk, s0=state_rwkv[:, li])

        oa = jnp.concatenate([oa_c, oa_l], axis=0)
        ob = jnp.concatenate([ob_c, ob_l], axis=0)
        oc = jnp.concatenate([oc_c, oc_l], axis=0)
        x = _merge(x, oa, ob, oc, pg, mod[li], wpa[li], wpb[li], wpc[li], wout[li])
        x = _ffn(x, vec(g_ffn), mod[li], wfi[li], wfo[li])

        ks.append(k_l.reshape(BATCH, SEQ, N_KV_A, HD_A))
        vs.append(v_l.reshape(BATCH, SEQ, N_KV_A, HD_A))
        sgs.append(sg_l)
        srs.append(sr_l)

    y_prompt = x[:N_TOK_CTX].reshape(BATCH, SEQ, D_MODEL)
    y_sample = x[N_TOK_CTX:].reshape(DEC_BATCH, DEC_SEQ, D_MODEL)
    return (y_prompt, y_sample, jnp.stack(ks, axis=1), jnp.stack(vs, axis=1),
            jnp.stack(sgs, axis=1), jnp.stack(srs, axis=1))
```

```python
import functools
import math

import numpy as np
import jax
import jax.numpy as jnp
from jax import lax
from jax.experimental import pallas as pl
from jax.experimental.pallas import tpu as pltpu

D_MODEL = 1024
BATCH = 16
SEQ = 256
DEPTH = 4
DEC_BATCH = 2
DEC_SEQ = 1024
PAST_LEN = 256
GRID_W = 64
N_HEADS_A = 8
N_KV_A = 2
HD_A = 64
ROPE_THETA = 10000.0
N_HEADS_B = 4
DK_B = 64
DV_B = 128
GK_RANK = 16
GK_NORMALIZER = 16.0
N_HEADS_C = 8
HD_C = 64
W_LORA = 64
A_LORA = 64
G_LORA = 128
GN_EPS = 64e-5
NORM_EPS = 1e-6
WIDTH_A = N_HEADS_A * HD_A
KV_A = N_KV_A * HD_A
WIDTH_BK = N_HEADS_B * DK_B
WIDTH_BV = N_HEADS_B * DV_B
WIDTH_C = N_HEADS_C * HD_C
N_BRANCH = 3
D_FF = ((8 * D_MODEL + 3 * 256 - 1) // (3 * 256)) * 256
OFF_QA = 0
OFF_KA = OFF_QA + WIDTH_A
OFF_VA = OFF_KA + KV_A
OFF_QB = OFF_VA + KV_A
OFF_KB = OFF_QB + WIDTH_BK
OFF_VB = OFF_KB + WIDTH_BK
OFF_GB = OFF_VB + WIDTH_BV
OFF_GKF = OFF_GB + WIDTH_BV
OFF_GKB = OFF_GKF + GK_RANK
OFF_C = OFF_GKB + GK_RANK
C_COLS = 3 * WIDTH_C + 2 * W_LORA + 2 * A_LORA + G_LORA
OFF_GATE = OFF_C + C_COLS
N_IN = OFF_GATE + N_BRANCH * D_MODEL

F32 = jnp.float32
BF16 = jnp.bfloat16

LANES = 128
VMEM_LIMIT = 56 * 1024 * 1024

N_TOK_CTX = BATCH * SEQ
N_TOK_LAT = DEC_BATCH * DEC_SEQ
N_TOK = N_TOK_CTX + N_TOK_LAT
N_COND = 1 + DEC_BATCH
COND_ROWS = 8
SEC_A = OFF_QB
SEC_B_MAIN = OFF_GKF - OFF_QB
SEC_B = SEC_B_MAIN + LANES
SEC_C = C_COLS
SEC_G = N_BRANCH * D_MODEL
TM = 512
TM_FFN = 1024
TF = D_FF // 2
GLA_CHUNK = 64
RW_CHUNK = 64
HI = lax.Precision.HIGHEST


def _tile_group(i, tm):
    ctx_tiles = N_TOK_CTX // tm
    lat_tiles = DEC_SEQ // tm
    return jnp.where(i < ctx_tiles, 0, 1 + (i - ctx_tiles) // lat_tiles)


def _bdot(a, b):
    return jnp.dot(a.astype(BF16), b.astype(BF16), preferred_element_type=F32)


def _dot_nt(a, b, precision=None):
    return lax.dot_general(a, b, (((1,), (1,)), ((), ())), preferred_element_type=F32, precision=precision)


def _dot_tn(a, b, precision=None):
    return lax.dot_general(a, b, (((0,), (0,)), ((), ())), preferred_element_type=F32, precision=precision)


def _sigmoid(x):
    return jax.nn.sigmoid(x)


def _modnorm(x, gain, shift, scale):
    ms = jnp.mean(x * x, axis=-1, keepdims=True)
    y = x * lax.rsqrt(ms + NORM_EPS) * gain
    return y * (1.0 + scale) + shift


def _mod_kernel(c_ref, w_ref, b_ref, o_ref):
    c = c_ref[...]
    s = c * _sigmoid(c)
    o_ref[...] = _bdot(s, w_ref[...]) + b_ref[...]


def _modulation(cond, w_ada, b_ada):
    tn = 1536
    n_out = 6 * D_MODEL
    return pl.pallas_call(
        _mod_kernel,
        grid=(DEPTH, n_out // tn),
        in_specs=[
            pl.BlockSpec((COND_ROWS, D_MODEL), lambda l, j: (0, 0)),
            pl.BlockSpec((None, D_MODEL, tn), lambda l, j: (l, 0, j)),
            pl.BlockSpec((None, 1, tn), lambda l, j: (l, 0, j)),
        ],
        out_specs=pl.BlockSpec((None, COND_ROWS, tn), lambda l, j: (l, 0, j)),
        out_shape=jax.ShapeDtypeStruct((DEPTH, COND_ROWS, n_out), F32),
        compiler_params=pltpu.CompilerParams(dimension_semantics=("arbitrary", "arbitrary")),
        name="modulation",
    )(cond, w_ada, b_ada.reshape(DEPTH, 1, n_out))


def _inproj_kernel(x_ref, g_ref, mod_ref, wa_ref, wb_ref, wc_ref, wg_ref, oa_ref, ob_ref, oc_ref, og_ref):
    h = _modnorm(x_ref[...], g_ref[...], mod_ref[0], mod_ref[1]).astype(BF16)
    oa_ref[...] = jnp.dot(h, wa_ref[...], preferred_element_type=F32)
    ob_ref[...] = jnp.dot(h, wb_ref[...], preferred_element_type=F32)
    oc_ref[...] = jnp.dot(h, wc_ref[...], preferred_element_type=F32)
    og_ref[...] = jnp.dot(h, wg_ref[...], preferred_element_type=F32)


def _resident(shape):
    nd = len(shape)
    return pl.BlockSpec(shape, lambda *_: (0,) * nd, pipeline_mode=pl.Buffered(1))


def _layer_resident(li, shape):
    nd = len(shape)
    return pl.BlockSpec((None,) + tuple(shape), lambda *_: (li,) + (0,) * nd, pipeline_mode=pl.Buffered(1))


def _mod_spec(li, tm):
    return pl.BlockSpec((None, None, 6, 1, D_MODEL), lambda i, *_: (li, _tile_group(i, tm), 0, 0, 0))


def _inproj(li, x, gain, mod, wa, wb, wc, wg):
    tm = 256
    widths = (SEC_A, SEC_B, SEC_C, SEC_G)
    return pl.pallas_call(
        _inproj_kernel,
        grid=(N_TOK // tm,),
        in_specs=[
            pl.BlockSpec((tm, D_MODEL), lambda i: (i, 0)),
            _layer_resident(li, (1, D_MODEL)),
            _mod_spec(li, tm),
        ] + [_layer_resident(li, (D_MODEL, w)) for w in widths],
        out_specs=[pl.BlockSpec((tm, w), lambda i: (i, 0)) for w in widths],
        out_shape=[jax.ShapeDtypeStruct((N_TOK, w), F32) for w in widths],
        compiler_params=pltpu.CompilerParams(dimension_semantics=("arbitrary",), vmem_limit_bytes=VMEM_LIMIT),
        name="inproj",
    )(x, gain, mod, wa, wb, wc, wg)


def _segsum(x, width):
    n = x.shape[-1]
    gi = lax.broadcasted_iota(jnp.int32, (n, n), 0) // width
    gj = lax.broadcasted_iota(jnp.int32, (n, n), 1) // width
    e = jnp.where(gi == gj, 1.0, 0.0).astype(BF16)
    hi = x.astype(BF16)
    lo = (x - hi.astype(F32)).astype(BF16)
    return jnp.dot(jnp.concatenate([hi, lo], axis=1), jnp.concatenate([e, e], axis=0),
                   preferred_element_type=F32)


def _swap_pairs(x):
    lane = lax.broadcasted_iota(jnp.int32, x.shape, 1)
    nxt = pltpu.roll(x, LANES - 1, 1)
    prv = pltpu.roll(x, 1, 1)
    return jnp.where(lane % 2 == 0, nxt, prv)


def _head_norm(x, gain):
    ms = _segsum(x * x, HD_A) * (1.0 / HD_A)
    return x * lax.rsqrt(ms + NORM_EPS) * gain


def _attn_kernel(*refs, t, latent):
    if latent:
        (pa_ref, qg_ref, kg_ref, ck_ref, cv_ref, cos_ref, sin_ref, o_ref, k_scr, v_scr, q_scr) = refs
    else:
        (pa_ref, qg_ref, kg_ref, o_ref, ko_ref, vo_ref, k_scr, v_scr, q_scr) = refs
    tk = k_scr.shape[0]
    past = tk - t
    kn = _head_norm(pa_ref[:, OFF_KA:OFF_VA], kg_ref[...])
    va = pa_ref[:, OFF_VA:OFF_QB]
    if latent:
        cos = cos_ref[...]
        sin = sin_ref[...]
        kn = kn * cos + _swap_pairs(kn) * sin
        k_scr[0:past, :] = ck_ref[...]
        v_scr[0:past, :] = cv_ref[...]
    else:
        ko_ref[...] = kn
        vo_ref[...] = va
    k_scr[past:tk, :] = kn
    v_scr[past:tk, :] = va
    for c in range(WIDTH_A // LANES):
        qn = _head_norm(pa_ref[:, c * LANES:(c + 1) * LANES], qg_ref[...])
        if latent:
            qn = qn * cos + _swap_pairs(qn) * sin
        q_scr[:, c * LANES:(c + 1) * LANES] = qn * (HD_A ** -0.5)

    group = N_HEADS_A // N_KV_A
    tq = min(t, 256)
    for kv in range(N_KV_A):
        kh = k_scr[:, kv * HD_A:(kv + 1) * HD_A].astype(BF16)
        vh = v_scr[:, kv * HD_A:(kv + 1) * HD_A].astype(BF16)
        for qb in range(t // tq):
            outs = []
            for g in range(group):
                h = kv * group + g
                qh = q_scr[qb * tq:(qb + 1) * tq, h * HD_A:(h + 1) * HD_A].astype(BF16)
                s = _dot_nt(qh, kh)
                m = jnp.max(s, axis=-1, keepdims=True)
                e = jnp.exp(s - m)
                pv = jnp.dot(e.astype(BF16), vh, preferred_element_type=F32)
                outs.append(pv / jnp.sum(e, axis=-1, keepdims=True))
            for j in range(group // 2):
                pair = jnp.concatenate([outs[2 * j], outs[2 * j + 1]], axis=1)
                col = (kv * group + 2 * j) * HD_A
                o_ref[qb * tq:(qb + 1) * tq, col:col + LANES] = pair


def _attention(pa, q_gain, k_gain, n_seq, t, row_block0, cache=None, rope=None):
    latent = cache is not None
    tk = t + (PAST_LEN if latent else 0)
    in_specs = [
        pl.BlockSpec((t, SEC_A), lambda s: (row_block0 + s, 0)),
        _resident((1, LANES)),
        _resident((1, LANES)),
    ]
    args = [pa, q_gain, k_gain]
    out_specs = [pl.BlockSpec((t, WIDTH_A), lambda s: (s, 0))]
    out_shape = [jax.ShapeDtypeStruct((n_seq * t, WIDTH_A), F32)]
    if latent:
        in_specs += [
            pl.BlockSpec((None, PAST_LEN, KV_A), lambda s: (s, 0, 0)),
            pl.BlockSpec((None, PAST_LEN, KV_A), lambda s: (s, 0, 0)),
            _resident((t, LANES)),
            _resident((t, LANES)),
        ]
        args += [cache[0], cache[1], rope[0], rope[1]]
    else:
        out_specs += [pl.BlockSpec((None, t, KV_A), lambda s: (s, 0, 0))] * 2
        out_shape += [jax.ShapeDtypeStruct((n_seq, t, KV_A), F32)] * 2
    return pl.pallas_call(
        functools.partial(_attn_kernel, t=t, latent=latent),
        grid=(n_seq,),
        in_specs=in_specs,
        out_specs=out_specs,
        out_shape=out_shape,
        scratch_shapes=[
            pltpu.VMEM((tk, KV_A), F32),
            pltpu.VMEM((tk, KV_A), F32),
            pltpu.VMEM((t, WIDTH_A), F32),
        ],
        compiler_params=pltpu.CompilerParams(dimension_semantics=("arbitrary",), vmem_limit_bytes=VMEM_LIMIT),
        name="attn_lat" if latent else "attn_ctx",
    )(*args)


def _rope_tables():
    tpos = np.arange(DEC_SEQ)
    row = (tpos // GRID_W).astype(np.float32)
    col = (tpos % GRID_W).astype(np.float32)
    n_pairs = HD_A // 4
    inv = (ROPE_THETA ** (-np.arange(n_pairs, dtype=np.float32) / n_pairs)).astype(np.float32)
    ang = np.concatenate([row[:, None] * inv, col[:, None] * inv], axis=-1)
    cos = np.repeat(np.cos(ang), 2, axis=-1)
    sin = np.repeat(np.sin(ang), 2, axis=-1)
    sign = np.tile(np.array([-1.0, 1.0], np.float32), HD_A // 2)
    reps = LANES // HD_A
    return (jnp.asarray(np.tile(cos, (1, reps)), F32), jnp.asarray(np.tile(sin * sign, (1, reps)), F32))


def _tri(n, reverse, strict):
    i = lax.broadcasted_iota(jnp.int32, (n, n), 0)
    j = lax.broadcasted_iota(jnp.int32, (n, n), 1)
    if reverse:
        keep = (j > i) if strict else (j >= i)
    else:
        keep = (j < i) if strict else (j <= i)
    return keep


def _chunk_rows(index, c):
    if isinstance(index, int):
        return pl.ds(index * c, c)
    return pl.ds(pl.multiple_of(index * c, c), c)


def _head_masks(shape):
    lane = lax.broadcasted_iota(jnp.int32, shape, 1)
    first = lane < (LANES // 2)
    return first, jnp.logical_not(first)


def _log_sigmoid(z):
    return jnp.minimum(z, 0.0) - jnp.log1p(jnp.exp(-jnp.abs(z)))


def _gla_kernel(*refs, t, has_init, want_state):
    refs = list(refs)
    (q_ref, k_ref, v_ref, g_ref, lr_ref, w2_ref, gb_ref, gain_ref,
     cum_ref, incl_ref, nmask_ref) = refs[:11]
    pos = 11
    if has_init:
        s0_ref = refs[pos]
        pos += 1
    o_ref = refs[pos]
    pos += 1
    if want_state:
        so_ref = refs[pos]
        pos += 1
    lg_scr, o_scr, qh_scr, n_scr, dec_scr, st_scr = refs[pos:pos + 6]

    c = GLA_CHUNK
    n_chunks = t // c
    half = LANES // 2
    jj = lax.broadcasted_iota(jnp.int32, (half, LANES), 0)
    ll = lax.broadcasted_iota(jnp.int32, (half, LANES), 1)
    sel = [jnp.where(ll == jj + h * half, 1.0, 0.0).astype(F32) for h in range(2)]

    lr = lr_ref[...].astype(BF16)
    for d in range(2):
        z = jnp.dot(lr, w2_ref[d].astype(BF16), preferred_element_type=F32) + gb_ref[d]
        lg_scr[d] = _log_sigmoid(z) * (1.0 / GK_NORMALIZER)
        for h in range(2):
            if has_init:
                st_scr[d, h * DV_B:(h + 1) * DV_B, :] = _dot_tn(s0_ref[d, h], sel[h], precision=HI)
            else:
                st_scr[d, h * DV_B:(h + 1) * DV_B, :] = jnp.zeros((DV_B, LANES), F32)

    m0, m1 = _head_masks((c, LANES))
    scale = DK_B ** -0.5

    def chunk_terms(i, carry):
        rows_f = _chunk_rows(i, c)
        rows_b = _chunk_rows(n_chunks - 1 - i, c)
        q = jnp.concatenate([q_ref[rows_f, :], q_ref[rows_b, :]], axis=0)
        k = jnp.concatenate([k_ref[rows_f, :], k_ref[rows_b, :]], axis=0)
        g = jnp.concatenate([lg_scr[0, rows_f, :], lg_scr[1, rows_b, :]], axis=0)
        b = _mm_exact_lhs(cum_ref[...], g)
        bm = jnp.concatenate([jnp.broadcast_to(b[c // 2:c // 2 + 1], (c, LANES)),
                              jnp.broadcast_to(b[c + c // 2:c + c // 2 + 1], (c, LANES))], axis=0)
        bl = jnp.concatenate([jnp.broadcast_to(b[c - 1:c], (c, LANES)),
                              jnp.broadcast_to(b[c:c + 1], (c, LANES))], axis=0)
        e1 = jnp.exp(b - bm)
        e2 = jnp.exp(bm - b)
        qt = q * scale * e1
        kt = k * e2
        qh_scr[i] = qt * jnp.exp(bm)
        kc = kt * jnp.exp(bl - bm)
        qf, qb = qt[0:c], qt[c:2 * c]
        lhs = jnp.concatenate([jnp.where(m0, qf, 0.0), jnp.where(m1, qf, 0.0),
                               jnp.where(m0, qb, 0.0), jnp.where(m1, qb, 0.0)], axis=0)
        rhs = jnp.concatenate([kt[0:c], kt[0:c], kt[c:2 * c], kt[c:2 * c]], axis=0)
        att = _mm(lhs, rhs, "nt", GLA_P_ATT) * incl_ref[...]
        vf = v_ref[rows_f, :]
        vb = v_ref[rows_b, :]
        vst = jnp.concatenate([vf[:, 0:DV_B], vf[:, DV_B:2 * DV_B], vb[:, 0:DV_B], vb[:, DV_B:2 * DV_B]], axis=0)
        o_scr[i] = _mm(att, vst, "nn", 1)
        n_scr[i, 0] = _mm(vf, kc[0:c], "tn", 1) * nmask_ref[...]
        n_scr[i, 1] = _mm(vb, kc[c:2 * c], "tn", 1) * nmask_ref[...]
        dec = jnp.exp(bl)
        dec_scr[i, 0] = dec[0:8]
        dec_scr[i, 1] = dec[c:c + 8]
        return carry

    if n_chunks <= GLA_UNROLL:
        for i in range(n_chunks):
            chunk_terms(i, 0)
    else:
        lax.fori_loop(0, n_chunks, chunk_terms, 0, unroll=GLA_UNROLL)

    def chunk_scan(i, carry):
        for d in range(2):
            st = st_scr[d]
            oi = _mm(qh_scr[i, d * c:(d + 1) * c, :], st, "nt", 1)
            for h in range(2):
                r0 = (2 * d + h) * c
                o_scr[i, r0:r0 + c, :] = o_scr[i, r0:r0 + c, :] + oi[:, h * DV_B:(h + 1) * DV_B]
            st_scr[d] = st * dec_scr[i, d, 0:1, :] + n_scr[i, d]
        return carry

    lax.fori_loop(0, n_chunks, chunk_scan, 0)

    for h in range(2):
        o = jnp.concatenate([o_scr[j, h * c:(h + 1) * c, :] + o_scr[n_chunks - 1 - j, (2 + h) * c:(3 + h) * c, :]
                             for j in range(n_chunks)], axis=0)
        ms = jnp.mean(o * o, axis=-1, keepdims=True)
        o = o * lax.rsqrt(ms + NORM_EPS) * gain_ref[...]
        gate = g_ref[:, h * DV_B:(h + 1) * DV_B]
        o_ref[:, h * DV_B:(h + 1) * DV_B] = o * (gate * _sigmoid(gate))
        if want_state:
            for d in range(2):
                so_ref[d, h] = _dot_nt(sel[h], st_scr[d, h * DV_B:(h + 1) * DV_B, :], precision=HI)


def _gla(pb, w2pad, gk_b, gain, n_seq, t, row_block0, s0=None, want_state=False):
    has_init = s0 is not None
    pair_w = 2 * DV_B
    state_spec = pl.BlockSpec((None, 2, 2, DK_B, DV_B), lambda s, p: (s, 0, p, 0, 0))
    in_specs = [
        pl.BlockSpec((t, LANES), lambda s, p: (row_block0 + s, p)),
        pl.BlockSpec((t, LANES), lambda s, p: (row_block0 + s, WIDTH_BK // LANES + p)),
        pl.BlockSpec((t, pair_w), lambda s, p: (row_block0 + s, (2 * WIDTH_BK) // pair_w + p)),
        pl.BlockSpec((t, pair_w), lambda s, p: (row_block0 + s, (2 * WIDTH_BK + WIDTH_BV) // pair_w + p)),
        pl.BlockSpec((t, LANES), lambda s, p: (row_block0 + s, SEC_B_MAIN // LANES)),
        pl.BlockSpec((2, LANES, LANES), lambda s, p: (0, 0, p)),
        pl.BlockSpec((2, 1, LANES), lambda s, p: (0, 0, p)),
        pl.BlockSpec((1, DV_B), lambda s, p: (0, 0)),
    ]
    args = [pb, pb, pb, pb, pb, w2pad, gk_b, gain]
    cum, _, incl, _, _ = _rwkv_chunk_constants()
    ri = np.arange(pair_w)[:, None] // DV_B
    li = np.arange(LANES)[None, :] // DK_B
    consts = (cum, incl, jnp.asarray((ri == li).astype(np.float32)))
    in_specs += [pl.BlockSpec(m.shape, lambda s, p: (0, 0)) for m in consts]
    args += list(consts)
    if has_init:
        in_specs.append(state_spec)
        args.append(s0)
    n_steps = t // GLA_CHUNK
    out_specs = [pl.BlockSpec((t, pair_w), lambda s, p: (s, p))]
    out_shape = [jax.ShapeDtypeStruct((n_seq * t, WIDTH_BV), F32)]
    if want_state:
        out_specs.append(state_spec)
        out_shape.append(jax.ShapeDtypeStruct((n_seq, 2, N_HEADS_B, DK_B, DV_B), F32))
    return pl.pallas_call(
        functools.partial(_gla_kernel, t=t, has_init=has_init, want_state=want_state),
        grid=(n_seq, N_HEADS_B // 2),
        in_specs=in_specs,
        out_specs=out_specs,
        out_shape=out_shape,
        scratch_shapes=[
            pltpu.VMEM((2, t, LANES), F32),
            pltpu.VMEM((n_steps, 4 * GLA_CHUNK, DV_B), F32),
            pltpu.VMEM((n_steps, 2 * GLA_CHUNK, LANES), F32),
            pltpu.VMEM((n_steps, 2, pair_w, LANES), F32),
            pltpu.VMEM((n_steps, 2, 8, LANES), F32),
            pltpu.VMEM((2, pair_w, LANES), F32),
        ],
        compiler_params=pltpu.CompilerParams(dimension_semantics=("arbitrary", "arbitrary")),
        name="gla_lat" if has_init else "gla_ctx",
    )(*args)


def _shift_mix(u, mu):
    t = u.shape[0]
    row = lax.broadcasted_iota(jnp.int32, u.shape, 0)
    prev = jnp.where(row == 0, 0.0, pltpu.roll(u, 1, 0))
    nxt = jnp.where(row == t - 1, 0.0, pltpu.roll(u, t - 1, 0))
    return u + mu * (0.5 * (prev + nxt) - u)


_DIMS = {"nn": (((1,), (0,)), ((), ())), "nt": (((1,), (1,)), ((), ())), "tn": (((0,), (0,)), ((), ()))}


def _split_bf16(x):
    hi = x.astype(BF16)
    lo = (x - hi.astype(F32)).astype(BF16)
    return hi, lo


def _mm(a, b, form, passes):
    dims = _DIMS[form]
    if passes == 6:
        return lax.dot_general(a, b, dims, preferred_element_type=F32, precision=HI)
    if passes == 1:
        return lax.dot_general(a.astype(BF16), b.astype(BF16), dims, preferred_element_type=F32)
    a_hi, a_lo = _split_bf16(a)
    b_hi, b_lo = _split_bf16(b)
    ka = dims[0][0][0]
    kb = dims[0][1][0]
    a_cat = jnp.concatenate([a_hi, a_lo, a_hi], axis=ka)
    b_cat = jnp.concatenate([b_hi, b_hi, b_lo], axis=kb)
    return lax.dot_general(a_cat, b_cat, dims, preferred_element_type=F32)


def _mm_exact_lhs(a, x):
    a16 = a.astype(BF16)
    x_hi, x_lo = _split_bf16(x)
    return jnp.dot(jnp.concatenate([a16, a16], axis=1), jnp.concatenate([x_hi, x_lo], axis=0),
                   preferred_element_type=F32)


GLA_P_ATT = 1
GLA_UNROLL = 4

RW_UNROLL = 4


def _half_neumann(n_mat, eye, c):
    sw = n_mat.shape[0]
    p = eye + n_mat
    m = _mm(n_mat, n_mat, "nn", 1)
    for _ in range(int(math.log2(c)) - 3):
        both = _mm(jnp.concatenate([m, p], axis=0), m, "nn", 1)
        m = both[0:sw]
        p = p + both[sw:2 * sw]
    return p + _mm(p, m, "nn", 1)


def _rwkv_kernel(*refs, t, has_init, want_state):
    refs = list(refs)
    (r_ref, k_ref, v_ref, wl_ref, al_ref, gl_ref,
     mur_ref, muk_ref, muv_ref, muw_ref, mua_ref, mug_ref,
     w0_ref, w2_ref, a0_ref, a2_ref, g2_ref, kk_ref, ka_ref, rk_ref, lw_ref, lb_ref,
     cum_ref, strict_ref, incl_ref, blk_ref, eye_ref) = refs[:27]
    pos = 27
    if has_init:
        s0_ref = refs[pos]
        pos += 1
    o_ref = refs[pos]
    pos += 1
    if want_state:
        so_ref = refs[pos]
        pos += 1
    r_scr, v_scr, a_scr, dir_scr, y_scr, rp_scr, m_scr, n_scr, dec_scr, st_scr = refs[pos:pos + 10]

    c = RW_CHUNK
    n_chunks = t // c
    half = LANES // 2

    r = _shift_mix(r_ref[...], mur_ref[...])
    kc = _shift_mix(k_ref[...], muk_ref[...])
    vc = _shift_mix(v_ref[...], muv_ref[...])
    wl_in = jnp.tanh(_shift_mix(wl_ref[...], muw_ref[...]))
    al_in = _shift_mix(al_ref[...], mua_ref[...])
    g_in = _sigmoid(_shift_mix(gl_ref[...], mug_ref[...]))

    kkf = kc * kk_ref[...]
    kk = kkf * lax.rsqrt(_segsum(kkf * kkf, HD_C) + 1e-12)
    r_scr[...] = r
    v_scr[...] = vc
    a_scr[...] = -kk
    bonus = jnp.zeros_like(r)
    for d in range(2):
        wl = w0_ref[d] + _bdot(wl_in, w2_ref[d])
        dir_scr[d, 0] = -math.exp(-0.5) * _sigmoid(wl)
        a = _sigmoid(a0_ref[d] + _bdot(al_in, a2_ref[d]))
        kd = kc * (1.0 + (a - 1.0) * ka_ref[...])
        dir_scr[d, 1] = kd
        dir_scr[d, 2] = kk * a
        bonus = bonus + _segsum(r * kd * rk_ref[...], HD_C)
    bonus = bonus * vc
    g_c = _bdot(g_in, g2_ref[...])

    nblk = 2 * (LANES // HD_C)
    sw = nblk * HD_C
    if has_init:
        z = jnp.zeros((HD_C, HD_C), F32)
        blocks = [s0_ref[0, 0], s0_ref[0, 1], s0_ref[1, 0], s0_ref[1, 1]]
        st_scr[...] = jnp.concatenate(
            [jnp.concatenate([blocks[i] if i == j else z for j in range(nblk)], axis=1) for i in range(nblk)], axis=0)
    else:
        st_scr[...] = jnp.zeros((sw, sw), F32)

    m0, m1 = _head_masks((c, LANES))

    def per_head(x):
        xf, xb = x[0:c], x[c:2 * c]
        return jnp.concatenate([jnp.where(m0, xf, 0.0), jnp.where(m1, xf, 0.0),
                                jnp.where(m0, xb, 0.0), jnp.where(m1, xb, 0.0)], axis=0)

    def per_block(x):
        xf, xb = x[0:c], x[c:2 * c]
        return jnp.concatenate([xf, xf, xb, xb], axis=0)

    def side_by_side(x):
        return jnp.concatenate([x[0:c], x[c:2 * c]], axis=1)

    def heads_select(x):
        return jnp.concatenate([jnp.where(m0, x[0:c], x[c:2 * c]), jnp.where(m0, x[2 * c:3 * c], x[3 * c:4 * c])],
                               axis=1)

    def heads_add(x):
        return jnp.concatenate([x[0:c] + x[c:2 * c], x[2 * c:3 * c] + x[3 * c:4 * c]], axis=1)

    def chunk_terms(i, carry):
        rows_f = _chunk_rows(i, c)
        rows_b = _chunk_rows(n_chunks - 1 - i, c)

        def both(ref_f, ref_b):
            return jnp.concatenate([ref_f[rows_f, :], ref_b[rows_b, :]], axis=0)

        rr = both(r_scr, r_scr)
        vv = both(v_scr, v_scr)
        aa = both(a_scr, a_scr)
        lw = both(dir_scr.at[0, 0], dir_scr.at[1, 0])
        kd = both(dir_scr.at[0, 1], dir_scr.at[1, 1])
        bb = both(dir_scr.at[0, 2], dir_scr.at[1, 2])
        cs = _mm_exact_lhs(cum_ref[...], lw)
        mid = lambda x: jnp.concatenate([jnp.broadcast_to(x[c // 2:c // 2 + 1], (c, LANES)),
                                         jnp.broadcast_to(x[c + c // 2:c + c // 2 + 1], (c, LANES))], axis=0)
        last = lambda x: jnp.concatenate([jnp.broadcast_to(x[c - 1:c], (c, LANES)),
                                          jnp.broadcast_to(x[c:c + 1], (c, LANES))], axis=0)
        cm = mid(cs)
        cl = last(cs)
        e1 = jnp.exp(cs - cm)
        e2 = jnp.exp(cm - cs)
        ecm = jnp.exp(cm)
        ecl = jnp.exp(cl - cm)
        rt = rr * e1
        at = aa * e1 * jnp.exp(-lw)
        bt = bb * e2
        kt = kd * e2
        bc = bt * ecl
        kcl = kt * ecl
        lhs = jnp.concatenate([per_head(at), per_head(rt)], axis=0)
        gram = _mm(lhs, jnp.concatenate([per_block(bt), per_block(kt)], axis=0), "nt", 1)
        a_ab = gram[0:sw, 0:sw] * strict_ref[...]
        a_rb = gram[sw:2 * sw, 0:sw] * incl_ref[...]
        a_ak = gram[0:sw, sw:2 * sw] * strict_ref[...]
        a_rk = gram[sw:2 * sw, sw:2 * sw] * incl_ref[...]
        p_half = _half_neumann(a_ab, eye_ref[...], c)
        vst = per_block(vv)
        av = _mm(a_ak, vst, "nn", 1)
        rhs = jnp.concatenate([av, per_head(at * ecm)], axis=1)
        wa = _mm(p_half, rhs, "nn", 1)
        res = rhs - wa + _mm(a_ab, wa, "nn", 3)
        wa = wa + _mm(p_half, res, "nn", 1)
        ya = _mm(jnp.concatenate([a_rb, a_rk], axis=1),
                 jnp.concatenate([wa, jnp.concatenate([vst, jnp.zeros_like(vst)], axis=1)], axis=0), "nn", 1)
        rp = per_head(rt * ecm) + ya[:, LANES:2 * LANES]
        w1 = heads_select(wa[:, 0:LANES])
        ap = heads_add(wa[:, LANES:2 * LANES])
        y_scr[i] = heads_select(ya[:, 0:LANES])
        rp_scr[i] = heads_add(rp)
        vv2 = side_by_side(vv)
        lhs_t = jnp.concatenate([jnp.concatenate([ap, w1], axis=1),
                                 jnp.concatenate([jnp.zeros_like(vv2), vv2], axis=1)], axis=0)
        mn = _mm(lhs_t, jnp.concatenate([side_by_side(bc), side_by_side(kcl)], axis=0), "tn", 1)
        m_scr[i] = mn[0:sw] * blk_ref[...]
        n_scr[i] = mn[sw:2 * sw] * blk_ref[...]
        dec_scr[i] = jnp.broadcast_to(side_by_side(jnp.exp(cl))[0:1], (8, 2 * LANES))
        return carry

    if n_chunks <= RW_UNROLL:
        for i in range(n_chunks):
            chunk_terms(i, 0)
    else:
        lax.fori_loop(0, n_chunks, chunk_terms, 0, unroll=RW_UNROLL)

    def chunk_scan(i, carry):
        st = st_scr[...]
        y_scr[i] = y_scr[i] + _mm(rp_scr[i], st, "nt", 1)
        st_scr[...] = st * dec_scr[i, 0:1, :] + _mm(st, m_scr[i], "nn", 1) + n_scr[i]
        return carry

    lax.fori_loop(0, n_chunks, chunk_scan, 0)

    y = jnp.concatenate([y_scr[j, :, 0:LANES] + y_scr[n_chunks - 1 - j, :, LANES:2 * LANES]
                         for j in range(n_chunks)], axis=0)
    mean = _segsum(y, HD_C) * (1.0 / HD_C)
    yc = y - mean
    var = _segsum(yc * yc, HD_C) * (1.0 / HD_C)
    gn = yc * lax.rsqrt(var + GN_EPS) * lw_ref[...] + lb_ref[...]
    o_ref[...] = (gn + bonus) * g_c
    if want_state:
        for d in range(2):
            for h in range(2):
                b0 = (2 * d + h) * HD_C
                so_ref[d, h] = st_scr[b0:b0 + HD_C, b0:b0 + HD_C]


def _rwkv_chunk_constants():
    c = RW_CHUNK
    sw = 2 * LANES
    i = np.arange(sw)[:, None]
    j = np.arange(sw)[None, :]
    same = (i // c) == (j // c)
    bwd = i >= LANES
    strict = same & np.where(bwd, j > i, j < i)
    incl = same & np.where(bwd, j >= i, j <= i)
    blk = (i // HD_C) == (j // HD_C)
    ci = np.arange(2 * c)[:, None]
    cj = np.arange(2 * c)[None, :]
    cum = ((ci // c) == (cj // c)) & np.where(ci >= c, cj >= ci, cj <= ci)
    f = lambda m: jnp.asarray(m.astype(np.float32))
    return f(cum), f(strict), f(incl), f(blk), f(np.eye(sw))


def _rwkv(pc, mu_c, w0, w2pad, a0, a2pad, g2, k_k, k_a, r_k, lnx_w, lnx_b, n_seq, t, row_block0,
          s0=None, want_state=False):
    has_init = s0 is not None
    n_pairs = WIDTH_C // LANES
    blk_w = 3 * n_pairs
    blk_a = blk_w + 1
    blk_g = blk_w + 2
    state_spec = pl.BlockSpec((None, 2, 2, HD_C, HD_C), lambda s, p: (s, 0, p, 0, 0))

    def sec(col_fn):
        return pl.BlockSpec((t, LANES), lambda s, p: (row_block0 + s, col_fn(p)))

    def vec(col_fn):
        return pl.BlockSpec((1, LANES), lambda s, p: (0, col_fn(p)))

    col_fns = [lambda p: p, lambda p: n_pairs + p, lambda p: 2 * n_pairs + p,
               lambda p: blk_w, lambda p: blk_a, lambda p: blk_g]
    in_specs = [sec(f) for f in col_fns] + [vec(f) for f in col_fns]
    in_specs += [
        pl.BlockSpec((2, 1, LANES), lambda s, p: (0, 0, p)),
        pl.BlockSpec((2, LANES, LANES), lambda s, p: (0, 0, p)),
        pl.BlockSpec((2, 1, LANES), lambda s, p: (0, 0, p)),
        pl.BlockSpec((2, LANES, LANES), lambda s, p: (0, 0, p)),
        pl.BlockSpec((LANES, LANES), lambda s, p: (0, p)),
    ] + [vec(lambda p: p)] * 5
    args = [pc] * 6 + [mu_c] * 6 + [w0, w2pad, a0, a2pad, g2, k_k, k_a, r_k, lnx_w, lnx_b]
    consts = _rwkv_chunk_constants()
    in_specs += [pl.BlockSpec(m.shape, lambda s, p: (0, 0)) for m in consts]
    args += list(consts)
    if has_init:
        in_specs.append(state_spec)
        args.append(s0)
    out_specs = [pl.BlockSpec((t, LANES), lambda s, p: (s, p))]
    out_shape = [jax.ShapeDtypeStruct((n_seq * t, WIDTH_C), F32)]
    if want_state:
        out_specs.append(state_spec)
        out_shape.append(jax.ShapeDtypeStruct((n_seq, 2, N_HEADS_C, HD_C, HD_C), F32))
    n_steps = t // RW_CHUNK
    sw = 2 * LANES
    return pl.pallas_call(
        functools.partial(_rwkv_kernel, t=t, has_init=has_init, want_state=want_state),
        grid=(n_seq, n_pairs),
        in_specs=in_specs,
        out_specs=out_specs,
        out_shape=out_shape,
        scratch_shapes=[
            pltpu.VMEM((t, LANES), F32),
            pltpu.VMEM((t, LANES), F32),
            pltpu.VMEM((t, LANES), F32),
            pltpu.VMEM((2, 3, t, LANES), F32),
            pltpu.VMEM((n_steps, RW_CHUNK, sw), F32),
            pltpu.VMEM((n_steps, RW_CHUNK, sw), F32),
            pltpu.VMEM((n_steps, sw, sw), F32),
            pltpu.VMEM((n_steps, sw, sw), F32),
            pltpu.VMEM((n_steps, 8, sw), F32),
            pltpu.VMEM((sw, sw), F32),
        ],
        compiler_params=pltpu.CompilerParams(dimension_semantics=("arbitrary", "arbitrary")),
        name="rwkv_lat" if has_init else "rwkv_ctx",
    )(*args)


def _merge_kernel(x_ref, oac_ref, oal_ref, obc_ref, obl_ref, occ_ref, ocl_ref, pg_ref, mod_ref,
                  wa_ref, wb_ref, wc_ref, wo_ref, o_ref, *, ctx_tiles):
    is_ctx = pl.program_id(0) < ctx_tiles
    merged = jnp.zeros(x_ref.shape, F32)
    for j, (c_ref, l_ref, w_ref) in enumerate(((oac_ref, oal_ref, wa_ref), (obc_ref, obl_ref, wb_ref),
                                              (occ_ref, ocl_ref, wc_ref))):
        gate = _sigmoid(pg_ref[:, j * D_MODEL:(j + 1) * D_MODEL])
        branch = jnp.where(is_ctx, c_ref[...], l_ref[...]).astype(BF16)
        merged = merged + gate * jnp.dot(branch, w_ref[...], preferred_element_type=F32)
    y = jnp.dot(merged.astype(BF16), wo_ref[...], preferred_element_type=F32)
    o_ref[...] = x_ref[...] + mod_ref[2] * y


def _merge(li, x, branches, pg, mod, wpa, wpb, wpc, wout):
    tm = TM
    ctx_tiles = N_TOK_CTX // tm
    row = lambda w: pl.BlockSpec((tm, w), lambda i: (i, 0))
    ctx_row = lambda w: pl.BlockSpec((tm, w), lambda i: (jnp.minimum(i, ctx_tiles - 1), 0))
    lat_row = lambda w: pl.BlockSpec((tm, w), lambda i: (jnp.maximum(i - ctx_tiles, 0), 0))
    branch_specs = []
    for w in (WIDTH_A, WIDTH_BV, WIDTH_C):
        branch_specs += [ctx_row(w), lat_row(w)]
    return pl.pallas_call(
        functools.partial(_merge_kernel, ctx_tiles=ctx_tiles),
        grid=(N_TOK // tm,),
        in_specs=[row(D_MODEL)] + branch_specs + [row(SEC_G), _mod_spec(li, tm),
                  _layer_resident(li, (WIDTH_A, D_MODEL)), _layer_resident(li, (WIDTH_BV, D_MODEL)),
                  _layer_resident(li, (WIDTH_C, D_MODEL)), _layer_resident(li, (D_MODEL, D_MODEL))],
        out_specs=row(D_MODEL),
        out_shape=jax.ShapeDtypeStruct((N_TOK, D_MODEL), F32),
        compiler_params=pltpu.CompilerParams(dimension_semantics=("arbitrary",), vmem_limit_bytes=VMEM_LIMIT),
        name="merge",
    )(x, *[b for pair in branches for b in pair], pg, mod, wpa, wpb, wpc, wout)


def _ffn_kernel(x_ref, g_ref, mod_ref, wg_ref, wu_ref, wo_ref, o_ref, h_scr, acc_scr):
    f = pl.program_id(1)

    @pl.when(f == 0)
    def _():
        h_scr[...] = _modnorm(x_ref[...], g_ref[...], mod_ref[3], mod_ref[4]).astype(BF16)
        acc_scr[...] = jnp.zeros_like(acc_scr)

    h = h_scr[...]
    gate = jnp.dot(h, wg_ref[...], preferred_element_type=F32)
    up = jnp.dot(h, wu_ref[...], preferred_element_type=F32)
    act = (gate * _sigmoid(gate) * up).astype(BF16)
    acc_scr[...] += jnp.dot(act, wo_ref[...], preferred_element_type=F32)

    @pl.when(f == pl.num_programs(1) - 1)
    def _():
        o_ref[...] = x_ref[...] + mod_ref[5] * acc_scr[...]


def _ffn(li, x, gain, mod, w_in, w_out):
    tm = TM_FFN
    nf = D_FF // TF
    return pl.pallas_call(
        _ffn_kernel,
        grid=(N_TOK // tm, nf),
        in_specs=[
            pl.BlockSpec((tm, D_MODEL), lambda i, f: (i, 0)),
            pl.BlockSpec((None, 1, D_MODEL), lambda i, f: (li, 0, 0)),
            _mod_spec(li, tm),
            pl.BlockSpec((None, D_MODEL, TF), lambda i, f: (li, 0, f)),
            pl.BlockSpec((None, D_MODEL, TF), lambda i, f: (li, 0, nf + f)),
            pl.BlockSpec((None, TF, D_MODEL), lambda i, f: (li, f, 0)),
        ],
        out_specs=pl.BlockSpec((tm, D_MODEL), lambda i, f: (i, 0)),
        out_shape=jax.ShapeDtypeStruct((N_TOK, D_MODEL), F32),
        scratch_shapes=[pltpu.VMEM((tm, D_MODEL), BF16), pltpu.VMEM((tm, D_MODEL), F32)],
        compiler_params=pltpu.CompilerParams(dimension_semantics=("arbitrary", "arbitrary"),
                                             vmem_limit_bytes=VMEM_LIMIT),
        name="ffn",
    )(x, gain, mod, w_in, w_in, w_out)


def _pad_rows(w, offset, total):
    pads = [(0, 0)] * w.ndim
    pads[-2] = (offset, total - offset - w.shape[-2])
    return jnp.pad(w, pads)


def kernel(x_prompt, x_sample, cache_k, cache_v, state_gla, state_rwkv, c, c_ctx, w_ada, b_ada, g_mix, g_ffn,
           w_in, q_gain, k_gain, gk_w2, gk_b, gla_gain, mu_c, w0, w2, a0, a2, g2, k_k, k_a, r_k, lnx_w, lnx_b,
           w_po_a, w_po_b, w_po_c, w_out, w_ffn_in, w_ffn_out):
    x = jnp.concatenate([x_prompt.reshape(N_TOK_CTX, D_MODEL), x_sample.reshape(N_TOK_LAT, D_MODEL)], axis=0)
    cond = jnp.concatenate([c_ctx[None, :], c, jnp.zeros((COND_ROWS - N_COND, D_MODEL), F32)], axis=0)
    mod = _modulation(cond, w_ada, b_ada)[:, :N_COND].reshape(DEPTH, N_COND, 6, 1, D_MODEL)

    wa = w_in[:, :, OFF_QA:OFF_QB].astype(BF16)
    wb = jnp.pad(w_in[:, :, OFF_QB:OFF_C], ((0, 0), (0, 0), (0, SEC_B - (OFF_C - OFF_QB)))).astype(BF16)
    wc = w_in[:, :, OFF_C:OFF_GATE].astype(BF16)
    wg = w_in[:, :, OFF_GATE:].astype(BF16)
    gk_w2p = jnp.stack([_pad_rows(gk_w2[:, 0], 0, LANES), _pad_rows(gk_w2[:, 1], GK_RANK, LANES)], axis=1)
    w2p = jnp.stack([_pad_rows(w2[:, 0], 0, LANES), _pad_rows(w2[:, 1], W_LORA, LANES)], axis=1)
    a2p = jnp.stack([_pad_rows(a2[:, 0], 0, LANES), _pad_rows(a2[:, 1], A_LORA, LANES)], axis=1)
    wpa, wpb, wpc, wout = (w.astype(BF16) for w in (w_po_a, w_po_b, w_po_c, w_out))
    wfi = w_ffn_in.astype(BF16)
    wfo = w_ffn_out.astype(BF16)
    rope = _rope_tables()
    lat_blk = N_TOK_CTX // DEC_SEQ
    g_mix3 = g_mix.reshape(DEPTH, 1, D_MODEL)
    g_ffn3 = g_ffn.reshape(DEPTH, 1, D_MODEL)

    ks, vs, sgs, srs = [], [], [], []
    for li in range(DEPTH):
        vec = lambda a: a[li].reshape(1, -1)
        pa, pb, pc, pg = _inproj(li, x, g_mix3, mod, wa, wb, wc, wg)

        qg = jnp.tile(vec(q_gain), (1, LANES // HD_A))
        kg = jnp.tile(vec(k_gain), (1, LANES // HD_A))
        oa_c, k_l, v_l = _attention(pa, qg, kg, BATCH, SEQ, 0)
        (oa_l,) = _attention(pa, qg, kg, DEC_BATCH, DEC_SEQ, lat_blk,
                             cache=(cache_k[:, li].reshape(DEC_BATCH, PAST_LEN, KV_A),
                                    cache_v[:, li].reshape(DEC_BATCH, PAST_LEN, KV_A)), rope=rope)

        gkb = gk_b[li].reshape(2, 1, WIDTH_BK)
        ob_c, sg_l = _gla(pb, gk_w2p[li], gkb, vec(gla_gain), BATCH, SEQ, 0, want_state=True)
        (ob_l,) = _gla(pb, gk_w2p[li], gkb, vec(gla_gain), DEC_BATCH, DEC_SEQ, lat_blk, s0=state_gla[:, li])

        rw_args = (pc, vec(mu_c), w0[li].reshape(2, 1, WIDTH_C), w2p[li], a0[li].reshape(2, 1, WIDTH_C), a2p[li],
                   g2[li], vec(k_k), vec(k_a), r_k[li].reshape(1, WIDTH_C), vec(lnx_w), vec(lnx_b))
        oc_c, sr_l = _rwkv(*rw_args, BATCH, SEQ, 0, want_state=True)
        (oc_l,) = _rwkv(*rw_args, DEC_BATCH, DEC_SEQ, lat_blk, s0=state_rwkv[:, li])

        x = _merge(li, x, ((oa_c, oa_l), (ob_c, ob_l), (oc_c, oc_l)), pg, mod, wpa, wpb, wpc, wout)
        x = _ffn(li, x, g_ffn3, mod, wfi, wfo)

        ks.append(k_l.reshape(BATCH, SEQ, N_KV_A, HD_A))
        vs.append(v_l.reshape(BATCH, SEQ, N_KV_A, HD_A))
        sgs.append(sg_l)
        srs.append(sr_l)

    y_prompt = x[:N_TOK_CTX].reshape(BATCH, SEQ, D_MODEL)
    y_sample = x[N_TOK_CTX:].reshape(DEC_BATCH, DEC_SEQ, D_MODEL)
    return (y_prompt, y_sample, jnp.stack(ks, axis=1), jnp.stack(vs, axis=1),
            jnp.stack(sgs, axis=1), jnp.stack(srs, axis=1))
```

```python
import functools
import math

import numpy as np
import jax
import jax.numpy as jnp
from jax import lax
from jax.experimental import pallas as pl
from jax.experimental.pallas import tpu as pltpu

D_MODEL = 1024
BATCH = 16
SEQ = 256
DEPTH = 4
DEC_BATCH = 2
DEC_SEQ = 1024
PAST_LEN = 256
GRID_W = 64
N_HEADS_A = 8
N_KV_A = 2
HD_A = 64
ROPE_THETA = 10000.0
N_HEADS_B = 4
DK_B = 64
DV_B = 128
GK_RANK = 16
GK_NORMALIZER = 16.0
N_HEADS_C = 8
HD_C = 64
W_LORA = 64
A_LORA = 64
G_LORA = 128
GN_EPS = 64e-5
NORM_EPS = 1e-6
WIDTH_A = N_HEADS_A * HD_A
KV_A = N_KV_A * HD_A
WIDTH_BK = N_HEADS_B * DK_B
WIDTH_BV = N_HEADS_B * DV_B
WIDTH_C = N_HEADS_C * HD_C
N_BRANCH = 3
D_FF = ((8 * D_MODEL + 3 * 256 - 1) // (3 * 256)) * 256
OFF_QA = 0
OFF_KA = OFF_QA + WIDTH_A
OFF_VA = OFF_KA + KV_A
OFF_QB = OFF_VA + KV_A
OFF_KB = OFF_QB + WIDTH_BK
OFF_VB = OFF_KB + WIDTH_BK
OFF_GB = OFF_VB + WIDTH_BV
OFF_GKF = OFF_GB + WIDTH_BV
OFF_GKB = OFF_GKF + GK_RANK
OFF_C = OFF_GKB + GK_RANK
C_COLS = 3 * WIDTH_C + 2 * W_LORA + 2 * A_LORA + G_LORA
OFF_GATE = OFF_C + C_COLS
N_IN = OFF_GATE + N_BRANCH * D_MODEL

F32 = jnp.float32
BF16 = jnp.bfloat16

LANES = 128
VMEM_LIMIT = 56 * 1024 * 1024

N_TOK_CTX = BATCH * SEQ
N_TOK_LAT = DEC_BATCH * DEC_SEQ
N_TOK = N_TOK_CTX + N_TOK_LAT
N_COND = 1 + DEC_BATCH
COND_ROWS = 8
SEC_A = OFF_QB
SEC_B_MAIN = OFF_GKF - OFF_QB
SEC_B = SEC_B_MAIN + LANES
SEC_C = C_COLS
SEC_G = N_BRANCH * D_MODEL
TM = 512
TM_FFN = 1024
TF = D_FF // 2
GLA_CHUNK = 64
RW_CHUNK = 64
HI = lax.Precision.HIGHEST


def _tile_group(i, tm):
    ctx_tiles = N_TOK_CTX // tm
    lat_tiles = DEC_SEQ // tm
    return jnp.where(i < ctx_tiles, 0, 1 + (i - ctx_tiles) // lat_tiles)


def _bdot(a, b):
    return jnp.dot(a.astype(BF16), b.astype(BF16), preferred_element_type=F32)


def _dot_nt(a, b, precision=None):
    return lax.dot_general(a, b, (((1,), (1,)), ((), ())), preferred_element_type=F32, precision=precision)


def _dot_tn(a, b, precision=None):
    return lax.dot_general(a, b, (((0,), (0,)), ((), ())), preferred_element_type=F32, precision=precision)


def _sigmoid(x):
    return jax.nn.sigmoid(x)


def _modnorm(x, gain, shift, scale):
    ms = jnp.mean(x * x, axis=-1, keepdims=True)
    y = x * lax.rsqrt(ms + NORM_EPS) * gain
    return y * (1.0 + scale) + shift


def _mod_kernel(c_ref, w_ref, b_ref, o_ref):
    c = c_ref[...]
    s = c * _sigmoid(c)
    o_ref[...] = _bdot(s, w_ref[...]) + b_ref[...]


def _modulation(cond, w_ada, b_ada):
    tn = 1536
    n_out = 6 * D_MODEL
    return pl.pallas_call(
        _mod_kernel,
        grid=(DEPTH, n_out // tn),
        in_specs=[
            pl.BlockSpec((COND_ROWS, D_MODEL), lambda l, j: (0, 0)),
            pl.BlockSpec((None, D_MODEL, tn), lambda l, j: (l, 0, j)),
            pl.BlockSpec((None, 1, tn), lambda l, j: (l, 0, j)),
        ],
        out_specs=pl.BlockSpec((None, COND_ROWS, tn), lambda l, j: (l, 0, j)),
        out_shape=jax.ShapeDtypeStruct((DEPTH, COND_ROWS, n_out), F32),
        compiler_params=pltpu.CompilerParams(dimension_semantics=("arbitrary", "arbitrary")),
        name="modulation",
    )(cond, w_ada, b_ada.reshape(DEPTH, 1, n_out))


def _inproj_kernel(x_ref, g_ref, mod_ref, wa_ref, wb_ref, wc_ref, wg_ref, oa_ref, ob_ref, oc_ref, og_ref):
    h = _modnorm(x_ref[...], g_ref[...], mod_ref[0], mod_ref[1]).astype(BF16)
    oa_ref[...] = jnp.dot(h, wa_ref[...], preferred_element_type=F32)
    ob_ref[...] = jnp.dot(h, wb_ref[...], preferred_element_type=F32)
    oc_ref[...] = jnp.dot(h, wc_ref[...], preferred_element_type=F32)
    og_ref[...] = jnp.dot(h, wg_ref[...], preferred_element_type=F32)


def _resident(shape):
    nd = len(shape)
    return pl.BlockSpec(shape, lambda *_: (0,) * nd, pipeline_mode=pl.Buffered(1))


def _layer_resident(li, shape):
    nd = len(shape)
    return pl.BlockSpec((None,) + tuple(shape), lambda *_: (li,) + (0,) * nd, pipeline_mode=pl.Buffered(1))


def _mod_spec(li, tm):
    return pl.BlockSpec((None, None, 6, 1, D_MODEL), lambda i, *_: (li, _tile_group(i, tm), 0, 0, 0))


def _inproj(li, x, gain, mod, wa, wb, wc, wg):
    tm = 256
    widths = (SEC_A, SEC_B, SEC_C, SEC_G)
    return pl.pallas_call(
        _inproj_kernel,
        grid=(N_TOK // tm,),
        in_specs=[
            pl.BlockSpec((tm, D_MODEL), lambda i: (i, 0)),
            _layer_resident(li, (1, D_MODEL)),
            _mod_spec(li, tm),
        ] + [_layer_resident(li, (D_MODEL, w)) for w in widths],
        out_specs=[pl.BlockSpec((tm, w), lambda i: (i, 0)) for w in widths],
        out_shape=[jax.ShapeDtypeStruct((N_TOK, w), F32) for w in widths],
        compiler_params=pltpu.CompilerParams(dimension_semantics=("arbitrary",), vmem_limit_bytes=VMEM_LIMIT),
        name="inproj",
    )(x, gain, mod, wa, wb, wc, wg)


def _segsum(x, width):
    n = x.shape[-1]
    gi = lax.broadcasted_iota(jnp.int32, (n, n), 0) // width
    gj = lax.broadcasted_iota(jnp.int32, (n, n), 1) // width
    e = jnp.where(gi == gj, 1.0, 0.0).astype(BF16)
    hi = x.astype(BF16)
    lo = (x - hi.astype(F32)).astype(BF16)
    return jnp.dot(jnp.concatenate([hi, lo], axis=1), jnp.concatenate([e, e], axis=0),
                   preferred_element_type=F32)


def _swap_pairs(x):
    lane = lax.broadcasted_iota(jnp.int32, x.shape, 1)
    nxt = pltpu.roll(x, LANES - 1, 1)
    prv = pltpu.roll(x, 1, 1)
    return jnp.where(lane % 2 == 0, nxt, prv)


def _head_norm(x, gain):
    ms = _segsum(x * x, HD_A) * (1.0 / HD_A)
    return x * lax.rsqrt(ms + NORM_EPS) * gain


def _attn_kernel(*refs, t, latent):
    if latent:
        (pa_ref, qg_ref, kg_ref, ck_ref, cv_ref, cos_ref, sin_ref, o_ref, k_scr, v_scr, q_scr) = refs
    else:
        (pa_ref, qg_ref, kg_ref, o_ref, ko_ref, vo_ref, k_scr, v_scr, q_scr) = refs
    tk = k_scr.shape[0]
    past = tk - t
    kn = _head_norm(pa_ref[:, OFF_KA:OFF_VA], kg_ref[...])
    va = pa_ref[:, OFF_VA:OFF_QB]
    if latent:
        cos = cos_ref[...]
        sin = sin_ref[...]
        kn = kn * cos + _swap_pairs(kn) * sin
        k_scr[0:past, :] = ck_ref[...]
        v_scr[0:past, :] = cv_ref[...]
    else:
        ko_ref[...] = kn
        vo_ref[...] = va
    k_scr[past:tk, :] = kn
    v_scr[past:tk, :] = va
    for c in range(WIDTH_A // LANES):
        qn = _head_norm(pa_ref[:, c * LANES:(c + 1) * LANES], qg_ref[...])
        if latent:
            qn = qn * cos + _swap_pairs(qn) * sin
        q_scr[:, c * LANES:(c + 1) * LANES] = qn * (HD_A ** -0.5)

    group = N_HEADS_A // N_KV_A
    tq = min(t, 256)
    for kv in range(N_KV_A):
        kh = k_scr[:, kv * HD_A:(kv + 1) * HD_A].astype(BF16)
        vh = v_scr[:, kv * HD_A:(kv + 1) * HD_A].astype(BF16)
        for qb in range(t // tq):
            outs = [None] * group

            def one_head(g, kv=kv, qb=qb, kh=kh, vh=vh, outs=outs):
                h = kv * group + g
                qh = q_scr[qb * tq:(qb + 1) * tq, h * HD_A:(h + 1) * HD_A].astype(BF16)
                s = _dot_nt(qh, kh)
                yield
                m = jnp.max(s, axis=-1, keepdims=True)
                e = jnp.exp(s - m)
                pv = jnp.dot(e.astype(BF16), vh, preferred_element_type=F32)
                yield
                outs[g] = pv / jnp.sum(e, axis=-1, keepdims=True)

            _trace_in_lockstep(one_head, range(group))
            for j in range(group // 2):
                pair = jnp.concatenate([outs[2 * j], outs[2 * j + 1]], axis=1)
                col = (kv * group + 2 * j) * HD_A
                o_ref[qb * tq:(qb + 1) * tq, col:col + LANES] = pair


def _attention(pa, q_gain, k_gain, n_seq, t, row_block0, cache=None, rope=None):
    latent = cache is not None
    tk = t + (PAST_LEN if latent else 0)
    in_specs = [
        pl.BlockSpec((t, SEC_A), lambda s: (row_block0 + s, 0)),
        _resident((1, LANES)),
        _resident((1, LANES)),
    ]
    args = [pa, q_gain, k_gain]
    out_specs = [pl.BlockSpec((t, WIDTH_A), lambda s: (s, 0))]
    out_shape = [jax.ShapeDtypeStruct((n_seq * t, WIDTH_A), F32)]
    if latent:
        in_specs += [
            pl.BlockSpec((None, PAST_LEN, KV_A), lambda s: (s, 0, 0)),
            pl.BlockSpec((None, PAST_LEN, KV_A), lambda s: (s, 0, 0)),
            _resident((t, LANES)),
            _resident((t, LANES)),
        ]
        args += [cache[0], cache[1], rope[0], rope[1]]
    else:
        out_specs += [pl.BlockSpec((None, t, KV_A), lambda s: (s, 0, 0))] * 2
        out_shape += [jax.ShapeDtypeStruct((n_seq, t, KV_A), F32)] * 2
    return pl.pallas_call(
        functools.partial(_attn_kernel, t=t, latent=latent),
        grid=(n_seq,),
        in_specs=in_specs,
        out_specs=out_specs,
        out_shape=out_shape,
        scratch_shapes=[
            pltpu.VMEM((tk, KV_A), F32),
            pltpu.VMEM((tk, KV_A), F32),
            pltpu.VMEM((t, WIDTH_A), F32),
        ],
        compiler_params=pltpu.CompilerParams(dimension_semantics=("arbitrary",), vmem_limit_bytes=VMEM_LIMIT),
        name="attn_lat" if latent else "attn_ctx",
    )(*args)


def _rope_tables():
    tpos = np.arange(DEC_SEQ)
    row = (tpos // GRID_W).astype(np.float32)
    col = (tpos % GRID_W).astype(np.float32)
    n_pairs = HD_A // 4
    inv = (ROPE_THETA ** (-np.arange(n_pairs, dtype=np.float32) / n_pairs)).astype(np.float32)
    ang = np.concatenate([row[:, None] * inv, col[:, None] * inv], axis=-1)
    cos = np.repeat(np.cos(ang), 2, axis=-1)
    sin = np.repeat(np.sin(ang), 2, axis=-1)
    sign = np.tile(np.array([-1.0, 1.0], np.float32), HD_A // 2)
    reps = LANES // HD_A
    return (jnp.asarray(np.tile(cos, (1, reps)), F32), jnp.asarray(np.tile(sin * sign, (1, reps)), F32))


def _tri(n, reverse, strict):
    i = lax.broadcasted_iota(jnp.int32, (n, n), 0)
    j = lax.broadcasted_iota(jnp.int32, (n, n), 1)
    if reverse:
        keep = (j > i) if strict else (j >= i)
    else:
        keep = (j < i) if strict else (j <= i)
    return keep


_DONE = object()


def _trace_in_lockstep(step_fn, steps):
    live = [step_fn(i) for i in steps]
    while live:
        live = [g for g in live if next(g, _DONE) is not _DONE]


def _for_each_step_group(step_fn, n_steps, group):
    if n_steps <= group:
        _trace_in_lockstep(step_fn, range(n_steps))
        return

    def body(j, carry):
        _trace_in_lockstep(step_fn, [j * group + k for k in range(group)])
        return carry

    lax.fori_loop(0, n_steps // group, body, 0)


def _chunk_rows(index, c):
    if isinstance(index, int):
        return pl.ds(index * c, c)
    return pl.ds(pl.multiple_of(index * c, c), c)


def _head_masks(shape):
    lane = lax.broadcasted_iota(jnp.int32, shape, 1)
    first = lane < (LANES // 2)
    return first, jnp.logical_not(first)


def _log_sigmoid(z):
    return jnp.minimum(z, 0.0) - jnp.log1p(jnp.exp(-jnp.abs(z)))


def _gla_kernel(*refs, t, has_init, want_state):
    refs = list(refs)
    (q_ref, k_ref, v_ref, g_ref, lr_ref, w2_ref, gb_ref, gain_ref,
     cum_ref, incl_ref, nmask_ref) = refs[:11]
    pos = 11
    if has_init:
        s0_ref = refs[pos]
        pos += 1
    o_ref = refs[pos]
    pos += 1
    if want_state:
        so_ref = refs[pos]
        pos += 1
    lg_scr, o_scr, qh_scr, n_scr, dec_scr, st_scr = refs[pos:pos + 6]

    c = GLA_CHUNK
    n_chunks = t // c
    half = LANES // 2
    jj = lax.broadcasted_iota(jnp.int32, (half, LANES), 0)
    ll = lax.broadcasted_iota(jnp.int32, (half, LANES), 1)
    sel = [jnp.where(ll == jj + h * half, 1.0, 0.0).astype(F32) for h in range(2)]

    lr = lr_ref[...].astype(BF16)
    for d in range(2):
        z = jnp.dot(lr, w2_ref[d].astype(BF16), preferred_element_type=F32) + gb_ref[d]
        lg_scr[d] = _log_sigmoid(z) * (1.0 / GK_NORMALIZER)
        for h in range(2):
            if has_init:
                st_scr[d, h * DV_B:(h + 1) * DV_B, :] = _dot_tn(s0_ref[d, h], sel[h], precision=HI)
            else:
                st_scr[d, h * DV_B:(h + 1) * DV_B, :] = jnp.zeros((DV_B, LANES), F32)

    m0, m1 = _head_masks((c, LANES))
    scale = DK_B ** -0.5

    def chunk_terms(i):
        rows_f = _chunk_rows(i, c)
        rows_b = _chunk_rows(n_chunks - 1 - i, c)
        q = jnp.concatenate([q_ref[rows_f, :], q_ref[rows_b, :]], axis=0)
        k = jnp.concatenate([k_ref[rows_f, :], k_ref[rows_b, :]], axis=0)
        g = jnp.concatenate([lg_scr[0, rows_f, :], lg_scr[1, rows_b, :]], axis=0)
        b = _mm_exact_lhs(cum_ref[...], g)
        yield
        bm = jnp.concatenate([jnp.broadcast_to(b[c // 2:c // 2 + 1], (c, LANES)),
                              jnp.broadcast_to(b[c + c // 2:c + c // 2 + 1], (c, LANES))], axis=0)
        bl = jnp.concatenate([jnp.broadcast_to(b[c - 1:c], (c, LANES)),
                              jnp.broadcast_to(b[c:c + 1], (c, LANES))], axis=0)
        e1 = jnp.exp(b - bm)
        e2 = jnp.exp(bm - b)
        qt = q * scale * e1
        kt = k * e2
        qh_scr[i] = qt * jnp.exp(bm)
        kc = kt * jnp.exp(bl - bm)
        qf, qb = qt[0:c], qt[c:2 * c]
        lhs = jnp.concatenate([jnp.where(m0, qf, 0.0), jnp.where(m1, qf, 0.0),
                               jnp.where(m0, qb, 0.0), jnp.where(m1, qb, 0.0)], axis=0)
        rhs = jnp.concatenate([kt[0:c], kt[0:c], kt[c:2 * c], kt[c:2 * c]], axis=0)
        att = _mm(lhs, rhs, "nt", GLA_P_ATT) * incl_ref[...]
        yield
        vf = v_ref[rows_f, :]
        vb = v_ref[rows_b, :]
        vst = jnp.concatenate([vf[:, 0:DV_B], vf[:, DV_B:2 * DV_B], vb[:, 0:DV_B], vb[:, DV_B:2 * DV_B]], axis=0)
        o_scr[i] = _mm(att, vst, "nn", 1)
        yield
        n_scr[i, 0] = _mm(vf, kc[0:c], "tn", 1) * nmask_ref[...]
        yield
        n_scr[i, 1] = _mm(vb, kc[c:2 * c], "tn", 1) * nmask_ref[...]
        dec = jnp.exp(bl)
        dec_scr[i, 0] = dec[0:8]
        dec_scr[i, 1] = dec[c:c + 8]

    _for_each_step_group(chunk_terms, n_chunks, GLA_UNROLL)

    def chunk_scan(i, carry):
        for d in range(2):
            st = st_scr[d]
            oi = _mm(qh_scr[i, d * c:(d + 1) * c, :], st, "nt", 1)
            for h in range(2):
                r0 = (2 * d + h) * c
                o_scr[i, r0:r0 + c, :] = o_scr[i, r0:r0 + c, :] + oi[:, h * DV_B:(h + 1) * DV_B]
            st_scr[d] = st * dec_scr[i, d, 0:1, :] + n_scr[i, d]
        return carry

    lax.fori_loop(0, n_chunks, chunk_scan, 0)

    for h in range(2):
        o = jnp.concatenate([o_scr[j, h * c:(h + 1) * c, :] + o_scr[n_chunks - 1 - j, (2 + h) * c:(3 + h) * c, :]
                             for j in range(n_chunks)], axis=0)
        ms = jnp.mean(o * o, axis=-1, keepdims=True)
        o = o * lax.rsqrt(ms + NORM_EPS) * gain_ref[...]
        gate = g_ref[:, h * DV_B:(h + 1) * DV_B]
        o_ref[:, h * DV_B:(h + 1) * DV_B] = o * (gate * _sigmoid(gate))
        if want_state:
            for d in range(2):
                so_ref[d, h] = _dot_nt(sel[h], st_scr[d, h * DV_B:(h + 1) * DV_B, :], precision=HI)


def _gla(pb, w2pad, gk_b, gain, n_seq, t, row_block0, s0=None, want_state=False):
    has_init = s0 is not None
    pair_w = 2 * DV_B
    state_spec = pl.BlockSpec((None, 2, 2, DK_B, DV_B), lambda s, p: (s, 0, p, 0, 0))
    in_specs = [
        pl.BlockSpec((t, LANES), lambda s, p: (row_block0 + s, p)),
        pl.BlockSpec((t, LANES), lambda s, p: (row_block0 + s, WIDTH_BK // LANES + p)),
        pl.BlockSpec((t, pair_w), lambda s, p: (row_block0 + s, (2 * WIDTH_BK) // pair_w + p)),
        pl.BlockSpec((t, pair_w), lambda s, p: (row_block0 + s, (2 * WIDTH_BK + WIDTH_BV) // pair_w + p)),
        pl.BlockSpec((t, LANES), lambda s, p: (row_block0 + s, SEC_B_MAIN // LANES)),
        pl.BlockSpec((2, LANES, LANES), lambda s, p: (0, 0, p)),
        pl.BlockSpec((2, 1, LANES), lambda s, p: (0, 0, p)),
        pl.BlockSpec((1, DV_B), lambda s, p: (0, 0)),
    ]
    args = [pb, pb, pb, pb, pb, w2pad, gk_b, gain]
    cum, _, incl, _, _ = _rwkv_chunk_constants()
    ri = np.arange(pair_w)[:, None] // DV_B
    li = np.arange(LANES)[None, :] // DK_B
    consts = (cum, incl, jnp.asarray((ri == li).astype(np.float32)))
    in_specs += [pl.BlockSpec(m.shape, lambda s, p: (0, 0)) for m in consts]
    args += list(consts)
    if has_init:
        in_specs.append(state_spec)
        args.append(s0)
    n_steps = t // GLA_CHUNK
    out_specs = [pl.BlockSpec((t, pair_w), lambda s, p: (s, p))]
    out_shape = [jax.ShapeDtypeStruct((n_seq * t, WIDTH_BV), F32)]
    if want_state:
        out_specs.append(state_spec)
        out_shape.append(jax.ShapeDtypeStruct((n_seq, 2, N_HEADS_B, DK_B, DV_B), F32))
    return pl.pallas_call(
        functools.partial(_gla_kernel, t=t, has_init=has_init, want_state=want_state),
        grid=(n_seq, N_HEADS_B // 2),
        in_specs=in_specs,
        out_specs=out_specs,
        out_shape=out_shape,
        scratch_shapes=[
            pltpu.VMEM((2, t, LANES), F32),
            pltpu.VMEM((n_steps, 4 * GLA_CHUNK, DV_B), F32),
            pltpu.VMEM((n_steps, 2 * GLA_CHUNK, LANES), F32),
            pltpu.VMEM((n_steps, 2, pair_w, LANES), F32),
            pltpu.VMEM((n_steps, 2, 8, LANES), F32),
            pltpu.VMEM((2, pair_w, LANES), F32),
        ],
        compiler_params=pltpu.CompilerParams(dimension_semantics=("arbitrary", "arbitrary")),
        name="gla_lat" if has_init else "gla_ctx",
    )(*args)


def _shift_mix(u, mu):
    t = u.shape[0]
    row = lax.broadcasted_iota(jnp.int32, u.shape, 0)
    prev = jnp.where(row == 0, 0.0, pltpu.roll(u, 1, 0))
    nxt = jnp.where(row == t - 1, 0.0, pltpu.roll(u, t - 1, 0))
    return u + mu * (0.5 * (prev + nxt) - u)


_DIMS = {"nn": (((1,), (0,)), ((), ())), "nt": (((1,), (1,)), ((), ())), "tn": (((0,), (0,)), ((), ()))}


def _split_bf16(x):
    hi = x.astype(BF16)
    lo = (x - hi.astype(F32)).astype(BF16)
    return hi, lo


def _mm(a, b, form, passes):
    dims = _DIMS[form]
    if passes == 6:
        return lax.dot_general(a, b, dims, preferred_element_type=F32, precision=HI)
    if passes == 1:
        return lax.dot_general(a.astype(BF16), b.astype(BF16), dims, preferred_element_type=F32)
    a_hi, a_lo = _split_bf16(a)
    b_hi, b_lo = _split_bf16(b)
    ka = dims[0][0][0]
    kb = dims[0][1][0]
    a_cat = jnp.concatenate([a_hi, a_lo, a_hi], axis=ka)
    b_cat = jnp.concatenate([b_hi, b_hi, b_lo], axis=kb)
    return lax.dot_general(a_cat, b_cat, dims, preferred_element_type=F32)


def _mm_exact_lhs(a, x):
    a16 = a.astype(BF16)
    x_hi, x_lo = _split_bf16(x)
    return jnp.dot(jnp.concatenate([a16, a16], axis=1), jnp.concatenate([x_hi, x_lo], axis=0),
                   preferred_element_type=F32)


GLA_P_ATT = 1
GLA_UNROLL = 4

RW_UNROLL = 4


def _half_neumann(n_mat, eye, c):
    sw = n_mat.shape[0]
    p = eye + n_mat
    m = _mm(n_mat, n_mat, "nn", 1)
    yield
    for _ in range(int(math.log2(c)) - 3):
        both = _mm(jnp.concatenate([m, p], axis=0), m, "nn", 1)
        yield
        m = both[0:sw]
        p = p + both[sw:2 * sw]
    return p + _mm(p, m, "nn", 1)


def _rwkv_kernel(*refs, t, has_init, want_state):
    refs = list(refs)
    (r_ref, k_ref, v_ref, wl_ref, al_ref, gl_ref,
     mur_ref, muk_ref, muv_ref, muw_ref, mua_ref, mug_ref,
     w0_ref, w2_ref, a0_ref, a2_ref, g2_ref, kk_ref, ka_ref, rk_ref, lw_ref, lb_ref,
     cum_ref, strict_ref, incl_ref, blk_ref, eye_ref) = refs[:27]
    pos = 27
    if has_init:
        s0_ref = refs[pos]
        pos += 1
    o_ref = refs[pos]
    pos += 1
    if want_state:
        so_ref = refs[pos]
        pos += 1
    r_scr, v_scr, a_scr, dir_scr, y_scr, rp_scr, m_scr, n_scr, dec_scr, st_scr = refs[pos:pos + 10]

    c = RW_CHUNK
    n_chunks = t // c
    half = LANES // 2

    r = _shift_mix(r_ref[...], mur_ref[...])
    kc = _shift_mix(k_ref[...], muk_ref[...])
    vc = _shift_mix(v_ref[...], muv_ref[...])
    wl_in = jnp.tanh(_shift_mix(wl_ref[...], muw_ref[...]))
    al_in = _shift_mix(al_ref[...], mua_ref[...])
    g_in = _sigmoid(_shift_mix(gl_ref[...], mug_ref[...]))

    kkf = kc * kk_ref[...]
    kk = kkf * lax.rsqrt(_segsum(kkf * kkf, HD_C) + 1e-12)
    r_scr[...] = r
    v_scr[...] = vc
    a_scr[...] = -kk
    bonus = jnp.zeros_like(r)
    for d in range(2):
        wl = w0_ref[d] + _bdot(wl_in, w2_ref[d])
        dir_scr[d, 0] = -math.exp(-0.5) * _sigmoid(wl)
        a = _sigmoid(a0_ref[d] + _bdot(al_in, a2_ref[d]))
        kd = kc * (1.0 + (a - 1.0) * ka_ref[...])
        dir_scr[d, 1] = kd
        dir_scr[d, 2] = kk * a
        bonus = bonus + _segsum(r * kd * rk_ref[...], HD_C)
    bonus = bonus * vc
    g_c = _bdot(g_in, g2_ref[...])

    nblk = 2 * (LANES // HD_C)
    sw = nblk * HD_C
    if has_init:
        z = jnp.zeros((HD_C, HD_C), F32)
        blocks = [s0_ref[0, 0], s0_ref[0, 1], s0_ref[1, 0], s0_ref[1, 1]]
        st_scr[...] = jnp.concatenate(
            [jnp.concatenate([blocks[i] if i == j else z for j in range(nblk)], axis=1) for i in range(nblk)], axis=0)
    else:
        st_scr[...] = jnp.zeros((sw, sw), F32)

    m0, m1 = _head_masks((c, LANES))

    def per_head(x):
        xf, xb = x[0:c], x[c:2 * c]
        return jnp.concatenate([jnp.where(m0, xf, 0.0), jnp.where(m1, xf, 0.0),
                                jnp.where(m0, xb, 0.0), jnp.where(m1, xb, 0.0)], axis=0)

    def per_block(x):
        xf, xb = x[0:c], x[c:2 * c]
        return jnp.concatenate([xf, xf, xb, xb], axis=0)

    def side_by_side(x):
        return jnp.concatenate([x[0:c], x[c:2 * c]], axis=1)

    def heads_select(x):
        return jnp.concatenate([jnp.where(m0, x[0:c], x[c:2 * c]), jnp.where(m0, x[2 * c:3 * c], x[3 * c:4 * c])],
                               axis=1)

    def heads_add(x):
        return jnp.concatenate([x[0:c] + x[c:2 * c], x[2 * c:3 * c] + x[3 * c:4 * c]], axis=1)

    def chunk_terms(i):
        rows_f = _chunk_rows(i, c)
        rows_b = _chunk_rows(n_chunks - 1 - i, c)

        def both(ref_f, ref_b):
            return jnp.concatenate([ref_f[rows_f, :], ref_b[rows_b, :]], axis=0)

        rr = both(r_scr, r_scr)
        vv = both(v_scr, v_scr)
        aa = both(a_scr, a_scr)
        lw = both(dir_scr.at[0, 0], dir_scr.at[1, 0])
        kd = both(dir_scr.at[0, 1], dir_scr.at[1, 1])
        bb = both(dir_scr.at[0, 2], dir_scr.at[1, 2])
        cs = _mm_exact_lhs(cum_ref[...], lw)
        yield
        mid = lambda x: jnp.concatenate([jnp.broadcast_to(x[c // 2:c // 2 + 1], (c, LANES)),
                                         jnp.broadcast_to(x[c + c // 2:c + c // 2 + 1], (c, LANES))], axis=0)
        last = lambda x: jnp.concatenate([jnp.broadcast_to(x[c - 1:c], (c, LANES)),
                                          jnp.broadcast_to(x[c:c + 1], (c, LANES))], axis=0)
        cm = mid(cs)
        cl = last(cs)
        e1 = jnp.exp(cs - cm)
        e2 = jnp.exp(cm - cs)
        ecm = jnp.exp(cm)
        ecl = jnp.exp(cl - cm)
        rt = rr * e1
        at = aa * e1 * jnp.exp(-lw)
        bt = bb * e2
        kt = kd * e2
        bc = bt * ecl
        kcl = kt * ecl
        lhs = jnp.concatenate([per_head(at), per_head(rt)], axis=0)
        gram = _mm(lhs, jnp.concatenate([per_block(bt), per_block(kt)], axis=0), "nt", 1)
        yield
        a_ab = gram[0:sw, 0:sw] * strict_ref[...]
        a_rb = gram[sw:2 * sw, 0:sw] * incl_ref[...]
        a_ak = gram[0:sw, sw:2 * sw] * strict_ref[...]
        a_rk = gram[sw:2 * sw, sw:2 * sw] * incl_ref[...]
        p_half = yield from _half_neumann(a_ab, eye_ref[...], c)
        yield
        vst = per_block(vv)
        av = _mm(a_ak, vst, "nn", 1)
        yield
        rhs = jnp.concatenate([av, per_head(at * ecm)], axis=1)
        wa = _mm(p_half, rhs, "nn", 1)
        yield
        res = rhs - wa + _mm(a_ab, wa, "nn", 3)
        yield
        wa = wa + _mm(p_half, res, "nn", 1)
        yield
        ya = _mm(jnp.concatenate([a_rb, a_rk], axis=1),
                 jnp.concatenate([wa, jnp.concatenate([vst, jnp.zeros_like(vst)], axis=1)], axis=0), "nn", 1)
        yield
        rp = per_head(rt * ecm) + ya[:, LANES:2 * LANES]
        w1 = heads_select(wa[:, 0:LANES])
        ap = heads_add(wa[:, LANES:2 * LANES])
        y_scr[i] = heads_select(ya[:, 0:LANES])
        rp_scr[i] = heads_add(rp)
        vv2 = side_by_side(vv)
        lhs_t = jnp.concatenate([jnp.concatenate([ap, w1], axis=1),
                                 jnp.concatenate([jnp.zeros_like(vv2), vv2], axis=1)], axis=0)
        mn = _mm(lhs_t, jnp.concatenate([side_by_side(bc), side_by_side(kcl)], axis=0), "tn", 1)
        m_scr[i] = mn[0:sw] * blk_ref[...]
        n_scr[i] = mn[sw:2 * sw] * blk_ref[...]
        dec_scr[i] = jnp.broadcast_to(side_by_side(jnp.exp(cl))[0:1], (8, 2 * LANES))

    _for_each_step_group(chunk_terms, n_chunks, RW_UNROLL)

    def chunk_scan(i, carry):
        st = st_scr[...]
        y_scr[i] = y_scr[i] + _mm(rp_scr[i], st, "nt", 1)
        st_scr[...] = st * dec_scr[i, 0:1, :] + _mm(st, m_scr[i], "nn", 1) + n_scr[i]
        return carry

    if has_init:
        lax.fori_loop(0, n_chunks, chunk_scan, 0)
    else:
        st_scr[...] = n_scr[0]
        lax.fori_loop(1, n_chunks, chunk_scan, 0)

    y = jnp.concatenate([y_scr[j, :, 0:LANES] + y_scr[n_chunks - 1 - j, :, LANES:2 * LANES]
                         for j in range(n_chunks)], axis=0)
    mean = _segsum(y, HD_C) * (1.0 / HD_C)
    yc = y - mean
    var = _segsum(yc * yc, HD_C) * (1.0 / HD_C)
    gn = yc * lax.rsqrt(var + GN_EPS) * lw_ref[...] + lb_ref[...]
    o_ref[...] = (gn + bonus) * g_c
    if want_state:
        for d in range(2):
            for h in range(2):
                b0 = (2 * d + h) * HD_C
                so_ref[d, h] = st_scr[b0:b0 + HD_C, b0:b0 + HD_C]


def _rwkv_chunk_constants():
    c = RW_CHUNK
    sw = 2 * LANES
    i = np.arange(sw)[:, None]
    j = np.arange(sw)[None, :]
    same = (i // c) == (j // c)
    bwd = i >= LANES
    strict = same & np.where(bwd, j > i, j < i)
    incl = same & np.where(bwd, j >= i, j <= i)
    blk = (i // HD_C) == (j // HD_C)
    ci = np.arange(2 * c)[:, None]
    cj = np.arange(2 * c)[None, :]
    cum = ((ci // c) == (cj // c)) & np.where(ci >= c, cj >= ci, cj <= ci)
    f = lambda m: jnp.asarray(m.astype(np.float32))
    return f(cum), f(strict), f(incl), f(blk), f(np.eye(sw))


def _rwkv(pc, mu_c, w0, w2pad, a0, a2pad, g2, k_k, k_a, r_k, lnx_w, lnx_b, n_seq, t, row_block0,
          s0=None, want_state=False):
    has_init = s0 is not None
    n_pairs = WIDTH_C // LANES
    blk_w = 3 * n_pairs
    blk_a = blk_w + 1
    blk_g = blk_w + 2
    state_spec = pl.BlockSpec((None, 2, 2, HD_C, HD_C), lambda s, p: (s, 0, p, 0, 0))

    def sec(col_fn):
        return pl.BlockSpec((t, LANES), lambda s, p: (row_block0 + s, col_fn(p)))

    def vec(col_fn):
        return pl.BlockSpec((1, LANES), lambda s, p: (0, col_fn(p)))

    col_fns = [lambda p: p, lambda p: n_pairs + p, lambda p: 2 * n_pairs + p,
               lambda p: blk_w, lambda p: blk_a, lambda p: blk_g]
    in_specs = [sec(f) for f in col_fns] + [vec(f) for f in col_fns]
    in_specs += [
        pl.BlockSpec((2, 1, LANES), lambda s, p: (0, 0, p)),
        pl.BlockSpec((2, LANES, LANES), lambda s, p: (0, 0, p)),
        pl.BlockSpec((2, 1, LANES), lambda s, p: (0, 0, p)),
        pl.BlockSpec((2, LANES, LANES), lambda s, p: (0, 0, p)),
        pl.BlockSpec((LANES, LANES), lambda s, p: (0, p)),
    ] + [vec(lambda p: p)] * 5
    args = [pc] * 6 + [mu_c] * 6 + [w0, w2pad, a0, a2pad, g2, k_k, k_a, r_k, lnx_w, lnx_b]
    consts = _rwkv_chunk_constants()
    in_specs += [pl.BlockSpec(m.shape, lambda s, p: (0, 0)) for m in consts]
    args += list(consts)
    if has_init:
        in_specs.append(state_spec)
        args.append(s0)
    out_specs = [pl.BlockSpec((t, LANES), lambda s, p: (s, p))]
    out_shape = [jax.ShapeDtypeStruct((n_seq * t, WIDTH_C), F32)]
    if want_state:
        out_specs.append(state_spec)
        out_shape.append(jax.ShapeDtypeStruct((n_seq, 2, N_HEADS_C, HD_C, HD_C), F32))
    n_steps = t // RW_CHUNK
    sw = 2 * LANES
    return pl.pallas_call(
        functools.partial(_rwkv_kernel, t=t, has_init=has_init, want_state=want_state),
        grid=(n_seq, n_pairs),
        in_specs=in_specs,
        out_specs=out_specs,
        out_shape=out_shape,
        scratch_shapes=[
            pltpu.VMEM((t, LANES), F32),
            pltpu.VMEM((t, LANES), F32),
            pltpu.VMEM((t, LANES), F32),
            pltpu.VMEM((2, 3, t, LANES), F32),
            pltpu.VMEM((n_steps, RW_CHUNK, sw), F32),
            pltpu.VMEM((n_steps, RW_CHUNK, sw), F32),
            pltpu.VMEM((n_steps, sw, sw), F32),
            pltpu.VMEM((n_steps, sw, sw), F32),
            pltpu.VMEM((n_steps, 8, sw), F32),
            pltpu.VMEM((sw, sw), F32),
        ],
        compiler_params=pltpu.CompilerParams(dimension_semantics=("arbitrary", "arbitrary")),
        name="rwkv_lat" if has_init else "rwkv_ctx",
    )(*args)


def _merge_kernel(x_ref, oac_ref, oal_ref, obc_ref, obl_ref, occ_ref, ocl_ref, pg_ref, mod_ref,
                  wa_ref, wb_ref, wc_ref, wo_ref, o_ref, *, ctx_tiles):
    is_ctx = pl.program_id(0) < ctx_tiles
    merged = jnp.zeros(x_ref.shape, F32)
    for j, (c_ref, l_ref, w_ref) in enumerate(((oac_ref, oal_ref, wa_ref), (obc_ref, obl_ref, wb_ref),
                                              (occ_ref, ocl_ref, wc_ref))):
        gate = _sigmoid(pg_ref[:, j * D_MODEL:(j + 1) * D_MODEL])
        branch = jnp.where(is_ctx, c_ref[...], l_ref[...]).astype(BF16)
        merged = merged + gate * jnp.dot(branch, w_ref[...], preferred_element_type=F32)
    y = jnp.dot(merged.astype(BF16), wo_ref[...], preferred_element_type=F32)
    o_ref[...] = x_ref[...] + mod_ref[2] * y


def _merge(li, x, branches, pg, mod, wpa, wpb, wpc, wout):
    tm = TM
    ctx_tiles = N_TOK_CTX // tm
    row = lambda w: pl.BlockSpec((tm, w), lambda i: (i, 0))
    ctx_row = lambda w: pl.BlockSpec((tm, w), lambda i: (jnp.minimum(i, ctx_tiles - 1), 0))
    lat_row = lambda w: pl.BlockSpec((tm, w), lambda i: (jnp.maximum(i - ctx_tiles, 0), 0))
    branch_specs = []
    for w in (WIDTH_A, WIDTH_BV, WIDTH_C):
        branch_specs += [ctx_row(w), lat_row(w)]
    return pl.pallas_call(
        functools.partial(_merge_kernel, ctx_tiles=ctx_tiles),
        grid=(N_TOK // tm,),
        in_specs=[row(D_MODEL)] + branch_specs + [row(SEC_G), _mod_spec(li, tm),
                  _layer_resident(li, (WIDTH_A, D_MODEL)), _layer_resident(li, (WIDTH_BV, D_MODEL)),
                  _layer_resident(li, (WIDTH_C, D_MODEL)), _layer_resident(li, (D_MODEL, D_MODEL))],
        out_specs=row(D_MODEL),
        out_shape=jax.ShapeDtypeStruct((N_TOK, D_MODEL), F32),
        compiler_params=pltpu.CompilerParams(dimension_semantics=("arbitrary",), vmem_limit_bytes=VMEM_LIMIT),
        name="merge",
    )(x, *[b for pair in branches for b in pair], pg, mod, wpa, wpb, wpc, wout)


def _ffn_kernel(x_ref, g_ref, mod_ref, wg_ref, wu_ref, wo_ref, o_ref, h_scr, acc_scr):
    f = pl.program_id(1)

    @pl.when(f == 0)
    def _():
        h_scr[...] = _modnorm(x_ref[...], g_ref[...], mod_ref[3], mod_ref[4]).astype(BF16)
        acc_scr[...] = jnp.zeros_like(acc_scr)

    h = h_scr[...]
    gate = jnp.dot(h, wg_ref[...], preferred_element_type=F32)
    up = jnp.dot(h, wu_ref[...], preferred_element_type=F32)
    act = (gate * _sigmoid(gate) * up).astype(BF16)
    acc_scr[...] += jnp.dot(act, wo_ref[...], preferred_element_type=F32)

    @pl.when(f == pl.num_programs(1) - 1)
    def _():
        o_ref[...] = x_ref[...] + mod_ref[5] * acc_scr[...]


def _ffn(li, x, gain, mod, w_in, w_out):
    tm = TM_FFN
    nf = D_FF // TF
    return pl.pallas_call(
        _ffn_kernel,
        grid=(N_TOK // tm, nf),
        in_specs=[
            pl.BlockSpec((tm, D_MODEL), lambda i, f: (i, 0)),
            pl.BlockSpec((None, 1, D_MODEL), lambda i, f: (li, 0, 0)),
            _mod_spec(li, tm),
            pl.BlockSpec((None, D_MODEL, TF), lambda i, f: (li, 0, f)),
            pl.BlockSpec((None, D_MODEL, TF), lambda i, f: (li, 0, nf + f)),
            pl.BlockSpec((None, TF, D_MODEL), lambda i, f: (li, f, 0)),
        ],
        out_specs=pl.BlockSpec((tm, D_MODEL), lambda i, f: (i, 0)),
        out_shape=jax.ShapeDtypeStruct((N_TOK, D_MODEL), F32),
        scratch_shapes=[pltpu.VMEM((tm, D_MODEL), BF16), pltpu.VMEM((tm, D_MODEL), F32)],
        compiler_params=pltpu.CompilerParams(dimension_semantics=("arbitrary", "arbitrary"),
                                             vmem_limit_bytes=VMEM_LIMIT),
        name="ffn",
    )(x, gain, mod, w_in, w_in, w_out)


def _pad_rows(w, offset, total):
    pads = [(0, 0)] * w.ndim
    pads[-2] = (offset, total - offset - w.shape[-2])
    return jnp.pad(w, pads)


def kernel(x_prompt, x_sample, cache_k, cache_v, state_gla, state_rwkv, c, c_ctx, w_ada, b_ada, g_mix, g_ffn,
           w_in, q_gain, k_gain, gk_w2, gk_b, gla_gain, mu_c, w0, w2, a0, a2, g2, k_k, k_a, r_k, lnx_w, lnx_b,
           w_po_a, w_po_b, w_po_c, w_out, w_ffn_in, w_ffn_out):
    x = jnp.concatenate([x_prompt.reshape(N_TOK_CTX, D_MODEL), x_sample.reshape(N_TOK_LAT, D_MODEL)], axis=0)
    cond = jnp.concatenate([c_ctx[None, :], c, jnp.zeros((COND_ROWS - N_COND, D_MODEL), F32)], axis=0)
    mod = _modulation(cond, w_ada, b_ada)[:, :N_COND].reshape(DEPTH, N_COND, 6, 1, D_MODEL)

    wa = w_in[:, :, OFF_QA:OFF_QB].astype(BF16)
    wb = jnp.pad(w_in[:, :, OFF_QB:OFF_C], ((0, 0), (0, 0), (0, SEC_B - (OFF_C - OFF_QB)))).astype(BF16)
    wc = w_in[:, :, OFF_C:OFF_GATE].astype(BF16)
    wg = w_in[:, :, OFF_GATE:].astype(BF16)
    gk_w2p = jnp.stack([_pad_rows(gk_w2[:, 0], 0, LANES), _pad_rows(gk_w2[:, 1], GK_RANK, LANES)], axis=1)
    w2p = jnp.stack([_pad_rows(w2[:, 0], 0, LANES), _pad_rows(w2[:, 1], W_LORA, LANES)], axis=1)
    a2p = jnp.stack([_pad_rows(a2[:, 0], 0, LANES), _pad_rows(a2[:, 1], A_LORA, LANES)], axis=1)
    wpa, wpb, wpc, wout = (w.astype(BF16) for w in (w_po_a, w_po_b, w_po_c, w_out))
    wfi = w_ffn_in.astype(BF16)
    wfo = w_ffn_out.astype(BF16)
    rope = _rope_tables()
    lat_blk = N_TOK_CTX // DEC_SEQ
    g_mix3 = g_mix.reshape(DEPTH, 1, D_MODEL)
    g_ffn3 = g_ffn.reshape(DEPTH, 1, D_MODEL)

    ks, vs, sgs, srs = [], [], [], []
    for li in range(DEPTH):
        vec = lambda a: a[li].reshape(1, -1)
        pa, pb, pc, pg = _inproj(li, x, g_mix3, mod, wa, wb, wc, wg)

        qg = jnp.tile(vec(q_gain), (1, LANES // HD_A))
        kg = jnp.tile(vec(k_gain), (1, LANES // HD_A))
        oa_c, k_l, v_l = _attention(pa, qg, kg, BATCH, SEQ, 0)
        (oa_l,) = _attention(pa, qg, kg, DEC_BATCH, DEC_SEQ, lat_blk,
                             cache=(cache_k[:, li].reshape(DEC_BATCH, PAST_LEN, KV_A),
                                    cache_v[:, li].reshape(DEC_BATCH, PAST_LEN, KV_A)), rope=rope)

        gkb = gk_b[li].reshape(2, 1, WIDTH_BK)
        ob_c, sg_l = _gla(pb, gk_w2p[li], gkb, vec(gla_gain), BATCH, SEQ, 0, want_state=True)
        (ob_l,) = _gla(pb, gk_w2p[li], gkb, vec(gla_gain), DEC_BATCH, DEC_SEQ, lat_blk, s0=state_gla[:, li])

        rw_args = (pc, vec(mu_c), w0[li].reshape(2, 1, WIDTH_C), w2p[li], a0[li].reshape(2, 1, WIDTH_C), a2p[li],
                   g2[li], vec(k_k), vec(k_a), r_k[li].reshape(1, WIDTH_C), vec(lnx_w), vec(lnx_b))
        oc_c, sr_l = _rwkv(*rw_args, BATCH, SEQ, 0, want_state=True)
        (oc_l,) = _rwkv(*rw_args, DEC_BATCH, DEC_SEQ, lat_blk, s0=state_rwkv[:, li])

        x = _merge(li, x, ((oa_c, oa_l), (ob_c, ob_l), (oc_c, oc_l)), pg, mod, wpa, wpb, wpc, wout)
        x = _ffn(li, x, g_ffn3, mod, wfi, wfo)

        ks.append(k_l.reshape(BATCH, SEQ, N_KV_A, HD_A))
        vs.append(v_l.reshape(BATCH, SEQ, N_KV_A, HD_A))
        sgs.append(sg_l)
        srs.append(sr_l)

    y_prompt = x[:N_TOK_CTX].reshape(BATCH, SEQ, D_MODEL)
    y_sample = x[N_TOK_CTX:].reshape(DEC_BATCH, DEC_SEQ, D_MODEL)
    return (y_prompt, y_sample, jnp.stack(ks, axis=1), jnp.stack(vs, axis=1),
            jnp.stack(sgs, axis=1), jnp.stack(srs, axis=1))
```

```python
import functools
import math

import numpy as np
import jax
import jax.numpy as jnp
from jax import lax
from jax.experimental import pallas as pl
from jax.experimental.pallas import tpu as pltpu

D_MODEL = 1024
BATCH = 16
SEQ = 256
DEPTH = 4
DEC_BATCH = 2
DEC_SEQ = 1024
PAST_LEN = 256
GRID_W = 64
N_HEADS_A = 8
N_KV_A = 2
HD_A = 64
ROPE_THETA = 10000.0
N_HEADS_B = 4
DK_B = 64
DV_B = 128
GK_RANK = 16
GK_NORMALIZER = 16.0
N_HEADS_C = 8
HD_C = 64
W_LORA = 64
A_LORA = 64
G_LORA = 128
GN_EPS = 64e-5
NORM_EPS = 1e-6
WIDTH_A = N_HEADS_A * HD_A
KV_A = N_KV_A * HD_A
WIDTH_BK = N_HEADS_B * DK_B
WIDTH_BV = N_HEADS_B * DV_B
WIDTH_C = N_HEADS_C * HD_C
N_BRANCH = 3
D_FF = ((8 * D_MODEL + 3 * 256 - 1) // (3 * 256)) * 256
OFF_QA = 0
OFF_KA = OFF_QA + WIDTH_A
OFF_VA = OFF_KA + KV_A
OFF_QB = OFF_VA + KV_A
OFF_KB = OFF_QB + WIDTH_BK
OFF_VB = OFF_KB + WIDTH_BK
OFF_GB = OFF_VB + WIDTH_BV
OFF_GKF = OFF_GB + WIDTH_BV
OFF_GKB = OFF_GKF + GK_RANK
OFF_C = OFF_GKB + GK_RANK
C_COLS = 3 * WIDTH_C + 2 * W_LORA + 2 * A_LORA + G_LORA
OFF_GATE = OFF_C + C_COLS
N_IN = OFF_GATE + N_BRANCH * D_MODEL

F32 = jnp.float32
BF16 = jnp.bfloat16

LANES = 128
VMEM_LIMIT = 56 * 1024 * 1024

N_TOK_CTX = BATCH * SEQ
N_TOK_LAT = DEC_BATCH * DEC_SEQ
N_TOK = N_TOK_CTX + N_TOK_LAT
N_COND = 1 + DEC_BATCH
COND_ROWS = 8
SEC_A = OFF_QB
SEC_B_MAIN = OFF_GKF - OFF_QB
SEC_B = SEC_B_MAIN + LANES
SEC_C = C_COLS
SEC_G = N_BRANCH * D_MODEL
TM = 512
TM_FFN = 1024
TF = D_FF // 2
GLA_CHUNK = 64
RW_CHUNK = 64
HI = lax.Precision.HIGHEST


def _tile_group(i, tm):
    ctx_tiles = N_TOK_CTX // tm
    lat_tiles = DEC_SEQ // tm
    return jnp.where(i < ctx_tiles, 0, 1 + (i - ctx_tiles) // lat_tiles)


def _bdot(a, b):
    return jnp.dot(a.astype(BF16), b.astype(BF16), preferred_element_type=F32)


def _dot_nt(a, b, precision=None):
    return lax.dot_general(a, b, (((1,), (1,)), ((), ())), preferred_element_type=F32, precision=precision)


def _dot_tn(a, b, precision=None):
    return lax.dot_general(a, b, (((0,), (0,)), ((), ())), preferred_element_type=F32, precision=precision)


def _sigmoid(x):
    return jax.nn.sigmoid(x)


def _modnorm(x, gain, shift, scale):
    ms = jnp.mean(x * x, axis=-1, keepdims=True)
    y = x * lax.rsqrt(ms + NORM_EPS) * gain
    return y * (1.0 + scale) + shift


def _mod_kernel(c_ref, w_ref, b_ref, o_ref):
    c = c_ref[...]
    s = c * _sigmoid(c)
    o_ref[...] = _bdot(s, w_ref[...]) + b_ref[...]


def _modulation(cond, w_ada, b_ada):
    tn = 1536
    n_out = 6 * D_MODEL
    return pl.pallas_call(
        _mod_kernel,
        grid=(DEPTH, n_out // tn),
        in_specs=[
            pl.BlockSpec((COND_ROWS, D_MODEL), lambda l, j: (0, 0)),
            pl.BlockSpec((None, D_MODEL, tn), lambda l, j: (l, 0, j)),
            pl.BlockSpec((None, 1, tn), lambda l, j: (l, 0, j)),
        ],
        out_specs=pl.BlockSpec((None, COND_ROWS, tn), lambda l, j: (l, 0, j)),
        out_shape=jax.ShapeDtypeStruct((DEPTH, COND_ROWS, n_out), F32),
        compiler_params=pltpu.CompilerParams(dimension_semantics=("arbitrary", "arbitrary")),
        name="modulation",
    )(cond, w_ada, b_ada.reshape(DEPTH, 1, n_out))


def _inproj_kernel(x_ref, g_ref, mod_ref, w_ref, oa_ref, ob_ref, oc_ref, og_ref):
    h = _modnorm(x_ref[...], g_ref[...], mod_ref[0], mod_ref[1]).astype(BF16)
    col = 0
    for o_ref in (oa_ref, ob_ref, oc_ref):
        width = o_ref.shape[1]
        o_ref[...] = jnp.dot(h, w_ref[:, col:col + width], preferred_element_type=F32)
        col += width
    og_ref[...] = _sigmoid(jnp.dot(h, w_ref[:, col:col + SEC_G], preferred_element_type=F32)).astype(BF16)


def _resident(shape):
    nd = len(shape)
    return pl.BlockSpec(shape, lambda *_: (0,) * nd, pipeline_mode=pl.Buffered(1))


def _layer_resident(li, shape):
    nd = len(shape)
    return pl.BlockSpec((None,) + tuple(shape), lambda *_: (li,) + (0,) * nd, pipeline_mode=pl.Buffered(1))


def _mod_spec(li, tm):
    return pl.BlockSpec((None, None, 6, 1, D_MODEL), lambda i, *_: (li, _tile_group(i, tm), 0, 0, 0))


def _inproj(li, x, gain, mod, w_sections):
    tm = TM
    widths = (SEC_A, SEC_B, SEC_C, SEC_G)
    dtypes = (F32, F32, F32, BF16)
    return pl.pallas_call(
        _inproj_kernel,
        grid=(N_TOK // tm,),
        in_specs=[
            pl.BlockSpec((tm, D_MODEL), lambda i: (i, 0)),
            _layer_resident(li, (1, D_MODEL)),
            _mod_spec(li, tm),
            _layer_resident(li, (D_MODEL, sum(widths))),
        ],
        out_specs=[pl.BlockSpec((tm, w), lambda i: (i, 0)) for w in widths],
        out_shape=[jax.ShapeDtypeStruct((N_TOK, w), dt) for w, dt in zip(widths, dtypes)],
        compiler_params=pltpu.CompilerParams(dimension_semantics=("arbitrary",), vmem_limit_bytes=VMEM_LIMIT),
        name="inproj",
    )(x, gain, mod, w_sections)


def _segsum(x, width):
    n = x.shape[-1]
    gi = lax.broadcasted_iota(jnp.int32, (n, n), 0) // width
    gj = lax.broadcasted_iota(jnp.int32, (n, n), 1) // width
    e = jnp.where(gi == gj, 1.0, 0.0).astype(BF16)
    hi = x.astype(BF16)
    lo = (x - hi.astype(F32)).astype(BF16)
    return jnp.dot(jnp.concatenate([hi, lo], axis=1), jnp.concatenate([e, e], axis=0),
                   preferred_element_type=F32)


def _swap_pairs(x):
    lane = lax.broadcasted_iota(jnp.int32, x.shape, 1)
    nxt = pltpu.roll(x, LANES - 1, 1)
    prv = pltpu.roll(x, 1, 1)
    return jnp.where(lane % 2 == 0, nxt, prv)


def _head_norm(x, gain):
    ms = _segsum(x * x, HD_A) * (1.0 / HD_A)
    return x * lax.rsqrt(ms + NORM_EPS) * gain


def _attn_kernel(*refs, t, latent):
    if latent:
        (pa_ref, qg_ref, kg_ref, ck_ref, cv_ref, cos_ref, sin_ref, o_ref, k_scr, v_scr, q_scr) = refs
    else:
        (pa_ref, qg_ref, kg_ref, o_ref, ko_ref, vo_ref, k_scr, v_scr, q_scr) = refs
    tk = k_scr.shape[0]
    past = tk - t
    kn = _head_norm(pa_ref[:, OFF_KA:OFF_VA], kg_ref[...])
    va = pa_ref[:, OFF_VA:OFF_QB]
    if latent:
        cos = cos_ref[...]
        sin = sin_ref[...]
        kn = kn * cos + _swap_pairs(kn) * sin
        k_scr[0:past, :] = ck_ref[...]
        v_scr[0:past, :] = cv_ref[...]
    else:
        ko_ref[...] = kn
        vo_ref[...] = va
    k_scr[past:tk, :] = kn
    v_scr[past:tk, :] = va
    for c in range(WIDTH_A // LANES):
        qn = _head_norm(pa_ref[:, c * LANES:(c + 1) * LANES], qg_ref[...])
        if latent:
            qn = qn * cos + _swap_pairs(qn) * sin
        q_scr[:, c * LANES:(c + 1) * LANES] = qn * (HD_A ** -0.5)

    group = N_HEADS_A // N_KV_A
    tq = min(t, 256)
    for kv in range(N_KV_A):
        kh = k_scr[:, kv * HD_A:(kv + 1) * HD_A].astype(BF16)
        vh = v_scr[:, kv * HD_A:(kv + 1) * HD_A].astype(BF16)
        for qb in range(t // tq):
            outs = [None] * group

            def one_head(g, kv=kv, qb=qb, kh=kh, vh=vh, outs=outs):
                h = kv * group + g
                qh = q_scr[qb * tq:(qb + 1) * tq, h * HD_A:(h + 1) * HD_A].astype(BF16)
                s = _dot_nt(qh, kh)
                yield
                m = jnp.max(s, axis=-1, keepdims=True)
                e = jnp.exp(s - m)
                pv = jnp.dot(e.astype(BF16), vh, preferred_element_type=F32)
                yield
                outs[g] = pv / jnp.sum(e, axis=-1, keepdims=True)

            _trace_in_lockstep(one_head, range(group))
            for j in range(group // 2):
                pair = jnp.concatenate([outs[2 * j], outs[2 * j + 1]], axis=1)
                col = (kv * group + 2 * j) * HD_A
                o_ref[qb * tq:(qb + 1) * tq, col:col + LANES] = pair


def _attention(pa, q_gain, k_gain, n_seq, t, row_block0, cache=None, rope=None):
    latent = cache is not None
    tk = t + (PAST_LEN if latent else 0)
    in_specs = [
        pl.BlockSpec((t, SEC_A), lambda s: (row_block0 + s, 0)),
        _resident((1, LANES)),
        _resident((1, LANES)),
    ]
    args = [pa, q_gain, k_gain]
    out_specs = [pl.BlockSpec((t, WIDTH_A), lambda s: (s, 0))]
    out_shape = [jax.ShapeDtypeStruct((n_seq * t, WIDTH_A), F32)]
    if latent:
        in_specs += [
            pl.BlockSpec((None, PAST_LEN, KV_A), lambda s: (s, 0, 0)),
            pl.BlockSpec((None, PAST_LEN, KV_A), lambda s: (s, 0, 0)),
            _resident((t, LANES)),
            _resident((t, LANES)),
        ]
        args += [cache[0], cache[1], rope[0], rope[1]]
    else:
        out_specs += [pl.BlockSpec((None, t, KV_A), lambda s: (s, 0, 0))] * 2
        out_shape += [jax.ShapeDtypeStruct((n_seq, t, KV_A), F32)] * 2
    return pl.pallas_call(
        functools.partial(_attn_kernel, t=t, latent=latent),
        grid=(n_seq,),
        in_specs=in_specs,
        out_specs=out_specs,
        out_shape=out_shape,
        scratch_shapes=[
            pltpu.VMEM((tk, KV_A), F32),
            pltpu.VMEM((tk, KV_A), F32),
            pltpu.VMEM((t, WIDTH_A), F32),
        ],
        compiler_params=pltpu.CompilerParams(dimension_semantics=("arbitrary",), vmem_limit_bytes=VMEM_LIMIT),
        name="attn_lat" if latent else "attn_ctx",
    )(*args)


def _rope_tables():
    tpos = np.arange(DEC_SEQ)
    row = (tpos // GRID_W).astype(np.float32)
    col = (tpos % GRID_W).astype(np.float32)
    n_pairs = HD_A // 4
    inv = (ROPE_THETA ** (-np.arange(n_pairs, dtype=np.float32) / n_pairs)).astype(np.float32)
    ang = np.concatenate([row[:, None] * inv, col[:, None] * inv], axis=-1)
    cos = np.repeat(np.cos(ang), 2, axis=-1)
    sin = np.repeat(np.sin(ang), 2, axis=-1)
    sign = np.tile(np.array([-1.0, 1.0], np.float32), HD_A // 2)
    reps = LANES // HD_A
    return (jnp.asarray(np.tile(cos, (1, reps)), F32), jnp.asarray(np.tile(sin * sign, (1, reps)), F32))


def _tri(n, reverse, strict):
    i = lax.broadcasted_iota(jnp.int32, (n, n), 0)
    j = lax.broadcasted_iota(jnp.int32, (n, n), 1)
    if reverse:
        keep = (j > i) if strict else (j >= i)
    else:
        keep = (j < i) if strict else (j <= i)
    return keep


_DONE = object()


def _trace_in_lockstep(step_fn, steps):
    live = [step_fn(i) for i in steps]
    while live:
        live = [g for g in live if next(g, _DONE) is not _DONE]


def _for_each_step_group(step_fn, n_steps, group):
    if n_steps <= group:
        _trace_in_lockstep(step_fn, range(n_steps))
        return

    def body(j, carry):
        _trace_in_lockstep(step_fn, [j * group + k for k in range(group)])
        return carry

    lax.fori_loop(0, n_steps // group, body, 0)


def _chunk_rows(index, c):
    if isinstance(index, int):
        return pl.ds(index * c, c)
    return pl.ds(pl.multiple_of(index * c, c), c)


def _head_masks(shape):
    lane = lax.broadcasted_iota(jnp.int32, shape, 1)
    first = lane < (LANES // 2)
    return first, jnp.logical_not(first)


def _log_sigmoid(z):
    return jnp.minimum(z, 0.0) - jnp.log1p(jnp.exp(-jnp.abs(z)))


def _gla_kernel(*refs, t, has_init, want_state):
    refs = list(refs)
    (q_ref, k_ref, v_ref, g_ref, lr_ref, w2_ref, gb_ref, gain_ref,
     cum_ref, incl_ref, nmask_ref) = refs[:11]
    pos = 11
    if has_init:
        s0_ref = refs[pos]
        pos += 1
    o_ref = refs[pos]
    pos += 1
    if want_state:
        so_ref = refs[pos]
        pos += 1
    lg_scr, o_scr, qh_scr, n_scr, dec_scr, st_scr = refs[pos:pos + 6]

    c = GLA_CHUNK
    n_chunks = t // c
    half = LANES // 2
    jj = lax.broadcasted_iota(jnp.int32, (half, LANES), 0)
    ll = lax.broadcasted_iota(jnp.int32, (half, LANES), 1)
    sel = [jnp.where(ll == jj + h * half, 1.0, 0.0).astype(F32) for h in range(2)]

    lr = lr_ref[...].astype(BF16)
    for d in range(2):
        z = jnp.dot(lr, w2_ref[d].astype(BF16), preferred_element_type=F32) + gb_ref[d]
        lg_scr[d] = _log_sigmoid(z) * (1.0 / GK_NORMALIZER)
        for h in range(2):
            if has_init:
                st_scr[d, h * DV_B:(h + 1) * DV_B, :] = _dot_tn(s0_ref[d, h], sel[h], precision=HI)
            else:
                st_scr[d, h * DV_B:(h + 1) * DV_B, :] = jnp.zeros((DV_B, LANES), F32)

    m0, m1 = _head_masks((c, LANES))
    scale = DK_B ** -0.5

    def chunk_terms(i):
        rows_f = _chunk_rows(i, c)
        rows_b = _chunk_rows(n_chunks - 1 - i, c)
        q = jnp.concatenate([q_ref[rows_f, :], q_ref[rows_b, :]], axis=0)
        k = jnp.concatenate([k_ref[rows_f, :], k_ref[rows_b, :]], axis=0)
        g = jnp.concatenate([lg_scr[0, rows_f, :], lg_scr[1, rows_b, :]], axis=0)
        b = _mm_exact_lhs(cum_ref[...], g)
        yield
        bm = jnp.concatenate([jnp.broadcast_to(b[c // 2:c // 2 + 1], (c, LANES)),
                              jnp.broadcast_to(b[c + c // 2:c + c // 2 + 1], (c, LANES))], axis=0)
        bl = jnp.concatenate([jnp.broadcast_to(b[c - 1:c], (c, LANES)),
                              jnp.broadcast_to(b[c:c + 1], (c, LANES))], axis=0)
        e1 = jnp.exp(b - bm)
        e2 = jnp.exp(bm - b)
        qt = q * scale * e1
        kt = k * e2
        qh_scr[i] = qt * jnp.exp(bm)
        kc = kt * jnp.exp(bl - bm)
        qf, qb = qt[0:c], qt[c:2 * c]
        lhs = jnp.concatenate([jnp.where(m0, qf, 0.0), jnp.where(m1, qf, 0.0),
                               jnp.where(m0, qb, 0.0), jnp.where(m1, qb, 0.0)], axis=0)
        rhs = jnp.concatenate([kt[0:c], kt[0:c], kt[c:2 * c], kt[c:2 * c]], axis=0)
        att = _mm(lhs, rhs, "nt", GLA_P_ATT) * incl_ref[...]
        yield
        vf = v_ref[rows_f, :]
        vb = v_ref[rows_b, :]
        vst = jnp.concatenate([vf[:, 0:DV_B], vf[:, DV_B:2 * DV_B], vb[:, 0:DV_B], vb[:, DV_B:2 * DV_B]], axis=0)
        o_scr[i] = _mm(att, vst, "nn", 1)
        yield
        n_scr[i, 0] = _mm(vf, kc[0:c], "tn", 1) * nmask_ref[...]
        yield
        n_scr[i, 1] = _mm(vb, kc[c:2 * c], "tn", 1) * nmask_ref[...]
        dec = jnp.exp(bl)
        dec_scr[i, 0] = dec[0:8]
        dec_scr[i, 1] = dec[c:c + 8]

    _for_each_step_group(chunk_terms, n_chunks, GLA_UNROLL)

    def chunk_scan(i, carry):
        for d in range(2):
            st = st_scr[d]
            oi = _mm(qh_scr[i, d * c:(d + 1) * c, :], st, "nt", 1)
            for h in range(2):
                r0 = (2 * d + h) * c
                o_scr[i, r0:r0 + c, :] = o_scr[i, r0:r0 + c, :] + oi[:, h * DV_B:(h + 1) * DV_B]
            st_scr[d] = st * dec_scr[i, d, 0:1, :] + n_scr[i, d]
        return carry

    lax.fori_loop(0, n_chunks, chunk_scan, 0)

    for h in range(2):
        o = jnp.concatenate([o_scr[j, h * c:(h + 1) * c, :] + o_scr[n_chunks - 1 - j, (2 + h) * c:(3 + h) * c, :]
                             for j in range(n_chunks)], axis=0)
        ms = jnp.mean(o * o, axis=-1, keepdims=True)
        o = o * lax.rsqrt(ms + NORM_EPS) * gain_ref[...]
        gate = g_ref[:, h * DV_B:(h + 1) * DV_B]
        o_ref[:, h * DV_B:(h + 1) * DV_B] = o * (gate * _sigmoid(gate))
        if want_state:
            for d in range(2):
                so_ref[d, h] = _dot_nt(sel[h], st_scr[d, h * DV_B:(h + 1) * DV_B, :], precision=HI)


def _gla(pb, w2pad, gk_b, gain, n_seq, t, row_block0, s0=None, want_state=False):
    has_init = s0 is not None
    pair_w = 2 * DV_B
    state_spec = pl.BlockSpec((None, 2, 2, DK_B, DV_B), lambda s, p: (s, 0, p, 0, 0))
    in_specs = [
        pl.BlockSpec((t, LANES), lambda s, p: (row_block0 + s, p)),
        pl.BlockSpec((t, LANES), lambda s, p: (row_block0 + s, WIDTH_BK // LANES + p)),
        pl.BlockSpec((t, pair_w), lambda s, p: (row_block0 + s, (2 * WIDTH_BK) // pair_w + p)),
        pl.BlockSpec((t, pair_w), lambda s, p: (row_block0 + s, (2 * WIDTH_BK + WIDTH_BV) // pair_w + p)),
        pl.BlockSpec((t, LANES), lambda s, p: (row_block0 + s, SEC_B_MAIN // LANES)),
        pl.BlockSpec((2, LANES, LANES), lambda s, p: (0, 0, p)),
        pl.BlockSpec((2, 1, LANES), lambda s, p: (0, 0, p)),
        pl.BlockSpec((1, DV_B), lambda s, p: (0, 0)),
    ]
    args = [pb, pb, pb, pb, pb, w2pad, gk_b, gain]
    cum, _, incl, _, _ = _rwkv_chunk_constants()
    ri = np.arange(pair_w)[:, None] // DV_B
    li = np.arange(LANES)[None, :] // DK_B
    consts = (cum, incl, jnp.asarray((ri == li).astype(np.float32)))
    in_specs += [pl.BlockSpec(m.shape, lambda s, p: (0, 0)) for m in consts]
    args += list(consts)
    if has_init:
        in_specs.append(state_spec)
        args.append(s0)
    n_steps = t // GLA_CHUNK
    out_specs = [pl.BlockSpec((t, pair_w), lambda s, p: (s, p))]
    out_shape = [jax.ShapeDtypeStruct((n_seq * t, WIDTH_BV), F32)]
    if want_state:
        out_specs.append(state_spec)
        out_shape.append(jax.ShapeDtypeStruct((n_seq, 2, N_HEADS_B, DK_B, DV_B), F32))
    return pl.pallas_call(
        functools.partial(_gla_kernel, t=t, has_init=has_init, want_state=want_state),
        grid=(n_seq, N_HEADS_B // 2),
        in_specs=in_specs,
        out_specs=out_specs,
        out_shape=out_shape,
        scratch_shapes=[
            pltpu.VMEM((2, t, LANES), F32),
            pltpu.VMEM((n_steps, 4 * GLA_CHUNK, DV_B), F32),
            pltpu.VMEM((n_steps, 2 * GLA_CHUNK, LANES), F32),
            pltpu.VMEM((n_steps, 2, pair_w, LANES), F32),
            pltpu.VMEM((n_steps, 2, 8, LANES), F32),
            pltpu.VMEM((2, pair_w, LANES), F32),
        ],
        compiler_params=pltpu.CompilerParams(dimension_semantics=("arbitrary", "arbitrary")),
        name="gla_lat" if has_init else "gla_ctx",
    )(*args)


def _shift_mix(u, mu):
    t = u.shape[0]
    row = lax.broadcasted_iota(jnp.int32, u.shape, 0)
    prev = jnp.where(row == 0, 0.0, pltpu.roll(u, 1, 0))
    nxt = jnp.where(row == t - 1, 0.0, pltpu.roll(u, t - 1, 0))
    return u + mu * (0.5 * (prev + nxt) - u)


_DIMS = {"nn": (((1,), (0,)), ((), ())), "nt": (((1,), (1,)), ((), ())), "tn": (((0,), (0,)), ((), ()))}


def _split_bf16(x):
    hi = x.astype(BF16)
    lo = (x - hi.astype(F32)).astype(BF16)
    return hi, lo


def _mm(a, b, form, passes):
    dims = _DIMS[form]
    if passes == 6:
        return lax.dot_general(a, b, dims, preferred_element_type=F32, precision=HI)
    if passes == 1:
        return lax.dot_general(a.astype(BF16), b.astype(BF16), dims, preferred_element_type=F32)
    a_hi, a_lo = _split_bf16(a)
    b_hi, b_lo = _split_bf16(b)
    ka = dims[0][0][0]
    kb = dims[0][1][0]
    a_cat = jnp.concatenate([a_hi, a_lo, a_hi], axis=ka)
    b_cat = jnp.concatenate([b_hi, b_hi, b_lo], axis=kb)
    return lax.dot_general(a_cat, b_cat, dims, preferred_element_type=F32)


def _mm_exact_lhs(a, x):
    a16 = a.astype(BF16)
    x_hi, x_lo = _split_bf16(x)
    return jnp.dot(jnp.concatenate([a16, a16], axis=1), jnp.concatenate([x_hi, x_lo], axis=0),
                   preferred_element_type=F32)


GLA_P_ATT = 1
GLA_UNROLL = 4

RW_UNROLL = 4


def _half_neumann(n_mat, eye, c):
    sw = n_mat.shape[0]
    p = eye + n_mat
    m = _mm(n_mat, n_mat, "nn", 1)
    yield
    for _ in range(int(math.log2(c)) - 3):
        both = _mm(jnp.concatenate([m, p], axis=0), m, "nn", 1)
        yield
        m = both[0:sw]
        p = p + both[sw:2 * sw]
    return p + _mm(p, m, "nn", 1)


def _rwkv_kernel(*refs, t, has_init, want_state):
    refs = list(refs)
    (r_ref, k_ref, v_ref, wl_ref, al_ref, gl_ref,
     mur_ref, muk_ref, muv_ref, muw_ref, mua_ref, mug_ref,
     w0_ref, w2_ref, a0_ref, a2_ref, g2_ref, kk_ref, ka_ref, rk_ref, lw_ref, lb_ref,
     cum_ref, strict_ref, incl_ref, blk_ref, eye_ref) = refs[:27]
    pos = 27
    if has_init:
        s0_ref = refs[pos]
        pos += 1
    o_ref = refs[pos]
    pos += 1
    if want_state:
        so_ref = refs[pos]
        pos += 1
    r_scr, v_scr, a_scr, dir_scr, y_scr, rp_scr, m_scr, n_scr, dec_scr, st_scr = refs[pos:pos + 10]

    c = RW_CHUNK
    n_chunks = t // c
    half = LANES // 2

    r = _shift_mix(r_ref[...], mur_ref[...])
    kc = _shift_mix(k_ref[...], muk_ref[...])
    vc = _shift_mix(v_ref[...], muv_ref[...])
    wl_in = jnp.tanh(_shift_mix(wl_ref[...], muw_ref[...]))
    al_in = _shift_mix(al_ref[...], mua_ref[...])
    g_in = _sigmoid(_shift_mix(gl_ref[...], mug_ref[...]))

    kkf = kc * kk_ref[...]
    kk = kkf * lax.rsqrt(_segsum(kkf * kkf, HD_C) + 1e-12)
    r_scr[...] = r
    v_scr[...] = vc
    a_scr[...] = -kk
    bonus = jnp.zeros_like(r)
    for d in range(2):
        wl = w0_ref[d] + _bdot(wl_in, w2_ref[d])
        dir_scr[d, 0] = -math.exp(-0.5) * _sigmoid(wl)
        a = _sigmoid(a0_ref[d] + _bdot(al_in, a2_ref[d]))
        kd = kc * (1.0 + (a - 1.0) * ka_ref[...])
        dir_scr[d, 1] = kd
        dir_scr[d, 2] = kk * a
        bonus = bonus + _segsum(r * kd * rk_ref[...], HD_C)
    bonus = bonus * vc
    g_c = _bdot(g_in, g2_ref[...])

    nblk = 2 * (LANES // HD_C)
    sw = nblk * HD_C
    if has_init:
        z = jnp.zeros((HD_C, HD_C), F32)
        blocks = [s0_ref[0, 0], s0_ref[0, 1], s0_ref[1, 0], s0_ref[1, 1]]
        st_scr[...] = jnp.concatenate(
            [jnp.concatenate([blocks[i] if i == j else z for j in range(nblk)], axis=1) for i in range(nblk)], axis=0)
    else:
        st_scr[...] = jnp.zeros((sw, sw), F32)

    m0, m1 = _head_masks((c, LANES))

    def per_head(x):
        xf, xb = x[0:c], x[c:2 * c]
        return jnp.concatenate([jnp.where(m0, xf, 0.0), jnp.where(m1, xf, 0.0),
                                jnp.where(m0, xb, 0.0), jnp.where(m1, xb, 0.0)], axis=0)

    def per_block(x):
        xf, xb = x[0:c], x[c:2 * c]
        return jnp.concatenate([xf, xf, xb, xb], axis=0)

    def side_by_side(x):
        return jnp.concatenate([x[0:c], x[c:2 * c]], axis=1)

    def heads_select(x):
        return jnp.concatenate([jnp.where(m0, x[0:c], x[c:2 * c]), jnp.where(m0, x[2 * c:3 * c], x[3 * c:4 * c])],
                               axis=1)

    def heads_add(x):
        return jnp.concatenate([x[0:c] + x[c:2 * c], x[2 * c:3 * c] + x[3 * c:4 * c]], axis=1)

    def chunk_terms(i):
        rows_f = _chunk_rows(i, c)
        rows_b = _chunk_rows(n_chunks - 1 - i, c)

        def both(ref_f, ref_b):
            return jnp.concatenate([ref_f[rows_f, :], ref_b[rows_b, :]], axis=0)

        rr = both(r_scr, r_scr)
        vv = both(v_scr, v_scr)
        aa = both(a_scr, a_scr)
        lw = both(dir_scr.at[0, 0], dir_scr.at[1, 0])
        kd = both(dir_scr.at[0, 1], dir_scr.at[1, 1])
        bb = both(dir_scr.at[0, 2], dir_scr.at[1, 2])
        cs = _mm_exact_lhs(cum_ref[...], lw)
        yield
        mid = lambda x: jnp.concatenate([jnp.broadcast_to(x[c // 2:c // 2 + 1], (c, LANES)),
                                         jnp.broadcast_to(x[c + c // 2:c + c // 2 + 1], (c, LANES))], axis=0)
        last = lambda x: jnp.concatenate([jnp.broadcast_to(x[c - 1:c], (c, LANES)),
                                          jnp.broadcast_to(x[c:c + 1], (c, LANES))], axis=0)
        cm = mid(cs)
        cl = last(cs)
        e1 = jnp.exp(cs - cm)
        e2 = jnp.exp(cm - cs)
        ecm = jnp.exp(cm)
        ecl = jnp.exp(cl - cm)
        rt = rr * e1
        at = aa * e1 * jnp.exp(-lw)
        bt = bb * e2
        kt = kd * e2
        bc = bt * ecl
        kcl = kt * ecl
        lhs = jnp.concatenate([per_head(at), per_head(rt)], axis=0)
        gram = _mm(lhs, jnp.concatenate([per_block(bt), per_block(kt)], axis=0), "nt", 1)
        yield
        a_ab = gram[0:sw, 0:sw] * strict_ref[...]
        a_rb = gram[sw:2 * sw, 0:sw] * incl_ref[...]
        a_ak = gram[0:sw, sw:2 * sw] * strict_ref[...]
        a_rk = gram[sw:2 * sw, sw:2 * sw] * incl_ref[...]
        p_half = yield from _half_neumann(a_ab, eye_ref[...], c)
        yield
        vst = per_block(vv)
        av = _mm(a_ak, vst, "nn", 1)
        yield
        rhs = jnp.concatenate([av, per_head(at * ecm)], axis=1)
        wa = _mm(p_half, rhs, "nn", 1)
        yield
        res = rhs - wa + _mm(a_ab, wa, "nn", 3)
        yield
        wa = wa + _mm(p_half, res, "nn", 1)
        yield
        ya = _mm(jnp.concatenate([a_rb, a_rk], axis=1),
                 jnp.concatenate([wa, jnp.concatenate([vst, jnp.zeros_like(vst)], axis=1)], axis=0), "nn", 1)
        yield
        rp = per_head(rt * ecm) + ya[:, LANES:2 * LANES]
        w1 = heads_select(wa[:, 0:LANES])
        ap = heads_add(wa[:, LANES:2 * LANES])
        y_scr[i] = heads_select(ya[:, 0:LANES])
        rp_scr[i] = heads_add(rp)
        vv2 = side_by_side(vv)
        lhs_t = jnp.concatenate([jnp.concatenate([ap, w1], axis=1),
                                 jnp.concatenate([jnp.zeros_like(vv2), vv2], axis=1)], axis=0)
        mn = _mm(lhs_t, jnp.concatenate([side_by_side(bc), side_by_side(kcl)], axis=0), "tn", 1)
        m_scr[i] = mn[0:sw] * blk_ref[...]
        n_scr[i] = mn[sw:2 * sw] * blk_ref[...]
        dec_scr[i] = jnp.broadcast_to(side_by_side(jnp.exp(cl))[0:1], (8, 2 * LANES))

    _for_each_step_group(chunk_terms, n_chunks, RW_UNROLL)

    def chunk_scan(i, carry):
        st = st_scr[...]
        y_scr[i] = y_scr[i] + _mm(rp_scr[i], st, "nt", 1)
        st_scr[...] = st * dec_scr[i, 0:1, :] + _mm(st, m_scr[i], "nn", 1) + n_scr[i]
        return carry

    if has_init:
        lax.fori_loop(0, n_chunks, chunk_scan, 0)
    else:
        st_scr[...] = n_scr[0]
        lax.fori_loop(1, n_chunks, chunk_scan, 0)

    y = jnp.concatenate([y_scr[j, :, 0:LANES] + y_scr[n_chunks - 1 - j, :, LANES:2 * LANES]
                         for j in range(n_chunks)], axis=0)
    mean = _segsum(y, HD_C) * (1.0 / HD_C)
    yc = y - mean
    var = _segsum(yc * yc, HD_C) * (1.0 / HD_C)
    gn = yc * lax.rsqrt(var + GN_EPS) * lw_ref[...] + lb_ref[...]
    o_ref[...] = (gn + bonus) * g_c
    if want_state:
        for d in range(2):
            for h in range(2):
                b0 = (2 * d + h) * HD_C
                so_ref[d, h] = st_scr[b0:b0 + HD_C, b0:b0 + HD_C]


def _rwkv_chunk_constants():
    c = RW_CHUNK
    sw = 2 * LANES
    i = np.arange(sw)[:, None]
    j = np.arange(sw)[None, :]
    same = (i // c) == (j // c)
    bwd = i >= LANES
    strict = same & np.where(bwd, j > i, j < i)
    incl = same & np.where(bwd, j >= i, j <= i)
    blk = (i // HD_C) == (j // HD_C)
    ci = np.arange(2 * c)[:, None]
    cj = np.arange(2 * c)[None, :]
    cum = ((ci // c) == (cj // c)) & np.where(ci >= c, cj >= ci, cj <= ci)
    f = lambda m: jnp.asarray(m.astype(np.float32))
    return f(cum), f(strict), f(incl), f(blk), f(np.eye(sw))


def _rwkv(pc, mu_c, w0, w2pad, a0, a2pad, g2, k_k, k_a, r_k, lnx_w, lnx_b, n_seq, t, row_block0,
          s0=None, want_state=False):
    has_init = s0 is not None
    n_pairs = WIDTH_C // LANES
    blk_w = 3 * n_pairs
    blk_a = blk_w + 1
    blk_g = blk_w + 2
    state_spec = pl.BlockSpec((None, 2, 2, HD_C, HD_C), lambda s, p: (s, 0, p, 0, 0))

    def sec(col_fn):
        return pl.BlockSpec((t, LANES), lambda s, p: (row_block0 + s, col_fn(p)))

    def vec(col_fn):
        return pl.BlockSpec((1, LANES), lambda s, p: (0, col_fn(p)))

    col_fns = [lambda p: p, lambda p: n_pairs + p, lambda p: 2 * n_pairs + p,
               lambda p: blk_w, lambda p: blk_a, lambda p: blk_g]
    in_specs = [sec(f) for f in col_fns] + [vec(f) for f in col_fns]
    in_specs += [
        pl.BlockSpec((2, 1, LANES), lambda s, p: (0, 0, p)),
        pl.BlockSpec((2, LANES, LANES), lambda s, p: (0, 0, p)),
        pl.BlockSpec((2, 1, LANES), lambda s, p: (0, 0, p)),
        pl.BlockSpec((2, LANES, LANES), lambda s, p: (0, 0, p)),
        pl.BlockSpec((LANES, LANES), lambda s, p: (0, p)),
    ] + [vec(lambda p: p)] * 5
    args = [pc] * 6 + [mu_c] * 6 + [w0, w2pad, a0, a2pad, g2, k_k, k_a, r_k, lnx_w, lnx_b]
    consts = _rwkv_chunk_constants()
    in_specs += [pl.BlockSpec(m.shape, lambda s, p: (0, 0)) for m in consts]
    args += list(consts)
    if has_init:
        in_specs.append(state_spec)
        args.append(s0)
    out_specs = [pl.BlockSpec((t, LANES), lambda s, p: (s, p))]
    out_shape = [jax.ShapeDtypeStruct((n_seq * t, WIDTH_C), F32)]
    if want_state:
        out_specs.append(state_spec)
        out_shape.append(jax.ShapeDtypeStruct((n_seq, 2, N_HEADS_C, HD_C, HD_C), F32))
    n_steps = t // RW_CHUNK
    sw = 2 * LANES
    return pl.pallas_call(
        functools.partial(_rwkv_kernel, t=t, has_init=has_init, want_state=want_state),
        grid=(n_seq, n_pairs),
        in_specs=in_specs,
        out_specs=out_specs,
        out_shape=out_shape,
        scratch_shapes=[
            pltpu.VMEM((t, LANES), F32),
            pltpu.VMEM((t, LANES), F32),
            pltpu.VMEM((t, LANES), F32),
            pltpu.VMEM((2, 3, t, LANES), F32),
            pltpu.VMEM((n_steps, RW_CHUNK, sw), F32),
            pltpu.VMEM((n_steps, RW_CHUNK, sw), F32),
            pltpu.VMEM((n_steps, sw, sw), F32),
            pltpu.VMEM((n_steps, sw, sw), F32),
            pltpu.VMEM((n_steps, 8, sw), F32),
            pltpu.VMEM((sw, sw), F32),
        ],
        compiler_params=pltpu.CompilerParams(dimension_semantics=("arbitrary", "arbitrary")),
        name="rwkv_lat" if has_init else "rwkv_ctx",
    )(*args)


def _merge_kernel(x_ref, oac_ref, oal_ref, obc_ref, obl_ref, occ_ref, ocl_ref, pg_ref, mod_ref,
                  wa_ref, wb_ref, wc_ref, wo_ref, o_ref, *, ctx_tiles):
    is_ctx = pl.program_id(0) < ctx_tiles
    merged = jnp.zeros(x_ref.shape, F32)
    for j, (c_ref, l_ref, w_ref) in enumerate(((oac_ref, oal_ref, wa_ref), (obc_ref, obl_ref, wb_ref),
                                              (occ_ref, ocl_ref, wc_ref))):
        gate = pg_ref[:, j * D_MODEL:(j + 1) * D_MODEL].astype(F32)
        branch = jnp.where(is_ctx, c_ref[...], l_ref[...]).astype(BF16)
        merged = merged + gate * jnp.dot(branch, w_ref[...], preferred_element_type=F32)
    y = jnp.dot(merged.astype(BF16), wo_ref[...], preferred_element_type=F32)
    o_ref[...] = x_ref[...] + mod_ref[2] * y


def _merge(li, x, branches, pg, mod, wpa, wpb, wpc, wout):
    tm = TM
    ctx_tiles = N_TOK_CTX // tm
    row = lambda w: pl.BlockSpec((tm, w), lambda i: (i, 0))
    ctx_row = lambda w: pl.BlockSpec((tm, w), lambda i: (jnp.minimum(i, ctx_tiles - 1), 0))
    lat_row = lambda w: pl.BlockSpec((tm, w), lambda i: (jnp.maximum(i - ctx_tiles, 0), 0))
    branch_specs = []
    for w in (WIDTH_A, WIDTH_BV, WIDTH_C):
        branch_specs += [ctx_row(w), lat_row(w)]
    return pl.pallas_call(
        functools.partial(_merge_kernel, ctx_tiles=ctx_tiles),
        grid=(N_TOK // tm,),
        in_specs=[row(D_MODEL)] + branch_specs + [row(SEC_G), _mod_spec(li, tm),
                  _layer_resident(li, (WIDTH_A, D_MODEL)), _layer_resident(li, (WIDTH_BV, D_MODEL)),
                  _layer_resident(li, (WIDTH_C, D_MODEL)), _layer_resident(li, (D_MODEL, D_MODEL))],
        out_specs=row(D_MODEL),
        out_shape=jax.ShapeDtypeStruct((N_TOK, D_MODEL), F32),
        compiler_params=pltpu.CompilerParams(dimension_semantics=("arbitrary",), vmem_limit_bytes=VMEM_LIMIT),
        name="merge",
    )(x, *[b for pair in branches for b in pair], pg, mod, wpa, wpb, wpc, wout)


def _ffn_kernel(x_ref, g_ref, mod_ref, wg_ref, wu_ref, wo_ref, o_ref, h_scr, acc_scr):
    f = pl.program_id(1)

    @pl.when(f == 0)
    def _():
        h_scr[...] = _modnorm(x_ref[...], g_ref[...], mod_ref[3], mod_ref[4]).astype(BF16)
        acc_scr[...] = jnp.zeros_like(acc_scr)

    h = h_scr[...]
    gate = jnp.dot(h, wg_ref[...], preferred_element_type=F32)
    up = jnp.dot(h, wu_ref[...], preferred_element_type=F32)
    act = (gate * _sigmoid(gate) * up).astype(BF16)
    acc_scr[...] += jnp.dot(act, wo_ref[...], preferred_element_type=F32)

    @pl.when(f == pl.num_programs(1) - 1)
    def _():
        o_ref[...] = x_ref[...] + mod_ref[5] * acc_scr[...]


def _ffn(li, x, gain, mod, w_in, w_out):
    tm = TM_FFN
    nf = D_FF // TF
    return pl.pallas_call(
        _ffn_kernel,
        grid=(N_TOK // tm, nf),
        in_specs=[
            pl.BlockSpec((tm, D_MODEL), lambda i, f: (i, 0)),
            pl.BlockSpec((None, 1, D_MODEL), lambda i, f: (li, 0, 0)),
            _mod_spec(li, tm),
            pl.BlockSpec((None, D_MODEL, TF), lambda i, f: (li, 0, f)),
            pl.BlockSpec((None, D_MODEL, TF), lambda i, f: (li, 0, nf + f)),
            pl.BlockSpec((None, TF, D_MODEL), lambda i, f: (li, f, 0)),
        ],
        out_specs=pl.BlockSpec((tm, D_MODEL), lambda i, f: (i, 0)),
        out_shape=jax.ShapeDtypeStruct((N_TOK, D_MODEL), F32),
        scratch_shapes=[pltpu.VMEM((tm, D_MODEL), BF16), pltpu.VMEM((tm, D_MODEL), F32)],
        compiler_params=pltpu.CompilerParams(dimension_semantics=("arbitrary", "arbitrary"),
                                             vmem_limit_bytes=VMEM_LIMIT),
        name="ffn",
    )(x, gain, mod, w_in, w_in, w_out)


def _pad_rows(w, offset, total):
    pads = [(0, 0)] * w.ndim
    pads[-2] = (offset, total - offset - w.shape[-2])
    return jnp.pad(w, pads)


def kernel(x_prompt, x_sample, cache_k, cache_v, state_gla, state_rwkv, c, c_ctx, w_ada, b_ada, g_mix, g_ffn,
           w_in, q_gain, k_gain, gk_w2, gk_b, gla_gain, mu_c, w0, w2, a0, a2, g2, k_k, k_a, r_k, lnx_w, lnx_b,
           w_po_a, w_po_b, w_po_c, w_out, w_ffn_in, w_ffn_out):
    x = jnp.concatenate([x_prompt.reshape(N_TOK_CTX, D_MODEL), x_sample.reshape(N_TOK_LAT, D_MODEL)], axis=0)
    cond = jnp.concatenate([c_ctx[None, :], c, jnp.zeros((COND_ROWS - N_COND, D_MODEL), F32)], axis=0)
    mod = _modulation(cond, w_ada, b_ada)[:, :N_COND].reshape(DEPTH, N_COND, 6, 1, D_MODEL)

    gap = jnp.zeros((DEPTH, D_MODEL, SEC_B - (OFF_C - OFF_QB)), BF16)
    w_sections = jnp.concatenate([w_in[:, :, :OFF_C].astype(BF16), gap, w_in[:, :, OFF_C:].astype(BF16)], axis=2)
    gk_w2p = jnp.stack([_pad_rows(gk_w2[:, 0], 0, LANES), _pad_rows(gk_w2[:, 1], GK_RANK, LANES)], axis=1)
    w2p = jnp.stack([_pad_rows(w2[:, 0], 0, LANES), _pad_rows(w2[:, 1], W_LORA, LANES)], axis=1)
    a2p = jnp.stack([_pad_rows(a2[:, 0], 0, LANES), _pad_rows(a2[:, 1], A_LORA, LANES)], axis=1)
    wpa, wpb, wpc, wout = (w.astype(BF16) for w in (w_po_a, w_po_b, w_po_c, w_out))
    wfi = w_ffn_in.astype(BF16)
    wfo = w_ffn_out.astype(BF16)
    rope = _rope_tables()
    lat_blk = N_TOK_CTX // DEC_SEQ
    g_mix3 = g_mix.reshape(DEPTH, 1, D_MODEL)
    g_ffn3 = g_ffn.reshape(DEPTH, 1, D_MODEL)

    ks, vs, sgs, srs = [], [], [], []
    for li in range(DEPTH):
        vec = lambda a: a[li].reshape(1, -1)
        pa, pb, pc, pg = _inproj(li, x, g_mix3, mod, w_sections)

        qg = jnp.tile(vec(q_gain), (1, LANES // HD_A))
        kg = jnp.tile(vec(k_gain), (1, LANES // HD_A))
        oa_c, k_l, v_l = _attention(pa, qg, kg, BATCH, SEQ, 0)
        (oa_l,) = _attention(pa, qg, kg, DEC_BATCH, DEC_SEQ, lat_blk,
                             cache=(cache_k[:, li].reshape(DEC_BATCH, PAST_LEN, KV_A),
                                    cache_v[:, li].reshape(DEC_BATCH, PAST_LEN, KV_A)), rope=rope)

        gkb = gk_b[li].reshape(2, 1, WIDTH_BK)
        ob_c, sg_l = _gla(pb, gk_w2p[li], gkb, vec(gla_gain), BATCH, SEQ, 0, want_state=True)
        (ob_l,) = _gla(pb, gk_w2p[li], gkb, vec(gla_gain), DEC_BATCH, DEC_SEQ, lat_blk, s0=state_gla[:, li])

        rw_args = (pc, vec(mu_c), w0[li].reshape(2, 1, WIDTH_C), w2p[li], a0[li].reshape(2, 1, WIDTH_C), a2p[li],
                   g2[li], vec(k_k), vec(k_a), r_k[li].reshape(1, WIDTH_C), vec(lnx_w), vec(lnx_b))
        oc_c, sr_l = _rwkv(*rw_args, BATCH, SEQ, 0, want_state=True)
        (oc_l,) = _rwkv(*rw_args, DEC_BATCH, DEC_SEQ, lat_blk, s0=state_rwkv[:, li])

        x = _merge(li, x, ((oa_c, oa_l), (ob_c, ob_l), (oc_c, oc_l)), pg, mod, wpa, wpb, wpc, wout)
        x = _ffn(li, x, g_ffn3, mod, wfi, wfo)

        ks.append(k_l.reshape(BATCH, SEQ, N_KV_A, HD_A))
        vs.append(v_l.reshape(BATCH, SEQ, N_KV_A, HD_A))
        sgs.append(sg_l)
        srs.append(sr_l)

    y_prompt = x[:N_TOK_CTX].reshape(BATCH, SEQ, D_MODEL)
    y_sample = x[N_TOK_CTX:].reshape(DEC_BATCH, DEC_SEQ, D_MODEL)
    return (y_prompt, y_sample, jnp.stack(ks, axis=1), jnp.stack(vs, axis=1),
            jnp.stack(sgs, axis=1), jnp.stack(srs, axis=1))
```

```python
import functools
import math

import numpy as np
import jax
import jax.numpy as jnp
from jax import lax
from jax.experimental import pallas as pl
from jax.experimental.pallas import tpu as pltpu

D_MODEL = 1024
BATCH = 16
SEQ = 256
DEPTH = 4
DEC_BATCH = 2
DEC_SEQ = 1024
PAST_LEN = 256
GRID_W = 64
N_HEADS_A = 8
N_KV_A = 2
HD_A = 64
ROPE_THETA = 10000.0
N_HEADS_B = 4
DK_B = 64
DV_B = 128
GK_RANK = 16
GK_NORMALIZER = 16.0
N_HEADS_C = 8
HD_C = 64
W_LORA = 64
A_LORA = 64
G_LORA = 128
GN_EPS = 64e-5
NORM_EPS = 1e-6
WIDTH_A = N_HEADS_A * HD_A
KV_A = N_KV_A * HD_A
WIDTH_BK = N_HEADS_B * DK_B
WIDTH_BV = N_HEADS_B * DV_B
WIDTH_C = N_HEADS_C * HD_C
N_BRANCH = 3
D_FF = ((8 * D_MODEL + 3 * 256 - 1) // (3 * 256)) * 256
OFF_QA = 0
OFF_KA = OFF_QA + WIDTH_A
OFF_VA = OFF_KA + KV_A
OFF_QB = OFF_VA + KV_A
OFF_KB = OFF_QB + WIDTH_BK
OFF_VB = OFF_KB + WIDTH_BK
OFF_GB = OFF_VB + WIDTH_BV
OFF_GKF = OFF_GB + WIDTH_BV
OFF_GKB = OFF_GKF + GK_RANK
OFF_C = OFF_GKB + GK_RANK
C_COLS = 3 * WIDTH_C + 2 * W_LORA + 2 * A_LORA + G_LORA
OFF_GATE = OFF_C + C_COLS
N_IN = OFF_GATE + N_BRANCH * D_MODEL

F32 = jnp.float32
BF16 = jnp.bfloat16

LANES = 128
VMEM_LIMIT = 56 * 1024 * 1024

N_TOK_CTX = BATCH * SEQ
N_TOK_LAT = DEC_BATCH * DEC_SEQ
N_TOK = N_TOK_CTX + N_TOK_LAT
N_COND = 1 + DEC_BATCH
COND_ROWS = 8
SEC_A = OFF_QB
SEC_B_MAIN = OFF_GKF - OFF_QB
SEC_B = SEC_B_MAIN + LANES
SEC_C = C_COLS
SEC_G = N_BRANCH * D_MODEL
TM = 512
TM_FFN = 1024
TM_INPROJ = 256
TF = D_FF // 2
GLA_CHUNK = 64
RW_CHUNK = 64
HI = lax.Precision.HIGHEST


def _tile_group(i, tm):
    ctx_tiles = N_TOK_CTX // tm
    lat_tiles = DEC_SEQ // tm
    return jnp.where(i < ctx_tiles, 0, 1 + (i - ctx_tiles) // lat_tiles)


def _bdot(a, b):
    return jnp.dot(a.astype(BF16), b.astype(BF16), preferred_element_type=F32)


def _dot_nt(a, b, precision=None):
    return lax.dot_general(a, b, (((1,), (1,)), ((), ())), preferred_element_type=F32, precision=precision)


def _dot_tn(a, b, precision=None):
    return lax.dot_general(a, b, (((0,), (0,)), ((), ())), preferred_element_type=F32, precision=precision)


def _sigmoid(x):
    return jax.nn.sigmoid(x)


def _modnorm(x, gain, shift, scale):
    ms = jnp.mean(x * x, axis=-1, keepdims=True)
    y = x * lax.rsqrt(ms + NORM_EPS) * gain
    return y * (1.0 + scale) + shift


def _mod_kernel(c_ref, w_ref, b_ref, o_ref):
    c = c_ref[...]
    s = c * _sigmoid(c)
    o_ref[...] = _bdot(s, w_ref[...]) + b_ref[...]


def _modulation(cond, w_ada, b_ada):
    tn = 1536
    n_out = 6 * D_MODEL
    return pl.pallas_call(
        _mod_kernel,
        grid=(DEPTH, n_out // tn),
        in_specs=[
            pl.BlockSpec((COND_ROWS, D_MODEL), lambda l, j: (0, 0)),
            pl.BlockSpec((None, D_MODEL, tn), lambda l, j: (l, 0, j)),
            pl.BlockSpec((None, 1, tn), lambda l, j: (l, 0, j)),
        ],
        out_specs=pl.BlockSpec((None, COND_ROWS, tn), lambda l, j: (l, 0, j)),
        out_shape=jax.ShapeDtypeStruct((DEPTH, COND_ROWS, n_out), F32),
        compiler_params=pltpu.CompilerParams(dimension_semantics=("arbitrary", "arbitrary")),
        name="modulation",
    )(cond, w_ada, b_ada.reshape(DEPTH, 1, n_out))


def _inproj_kernel(x_ref, g_ref, mod_ref, w_ref, oa_ref, ob_ref, oc_ref, og_ref):
    h = _modnorm(x_ref[...], g_ref[...], mod_ref[0], mod_ref[1]).astype(BF16)
    col = 0
    for o_ref in (oa_ref, ob_ref, oc_ref):
        width = o_ref.shape[1]
        o_ref[...] = jnp.dot(h, w_ref[:, col:col + width], preferred_element_type=F32)
        col += width
    og_ref[...] = _sigmoid(jnp.dot(h, w_ref[:, col:col + SEC_G], preferred_element_type=F32)).astype(BF16)


def _resident(shape):
    nd = len(shape)
    return pl.BlockSpec(shape, lambda *_: (0,) * nd, pipeline_mode=pl.Buffered(1))


def _layer_resident(li, shape):
    nd = len(shape)
    return pl.BlockSpec((None,) + tuple(shape), lambda *_: (li,) + (0,) * nd, pipeline_mode=pl.Buffered(1))


def _mod_spec(li, tm):
    return pl.BlockSpec((None, None, 6, 1, D_MODEL), lambda i, *_: (li, _tile_group(i, tm), 0, 0, 0))


def _inproj(li, x, gain, mod, w_sections):
    tm = TM_INPROJ
    widths = (SEC_A, SEC_B, SEC_C, SEC_G)
    dtypes = (F32, F32, F32, BF16)
    return pl.pallas_call(
        _inproj_kernel,
        grid=(N_TOK // tm,),
        in_specs=[
            pl.BlockSpec((tm, D_MODEL), lambda i: (i, 0)),
            _layer_resident(li, (1, D_MODEL)),
            _mod_spec(li, tm),
            _layer_resident(li, (D_MODEL, sum(widths))),
        ],
        out_specs=[pl.BlockSpec((tm, w), lambda i: (i, 0)) for w in widths],
        out_shape=[jax.ShapeDtypeStruct((N_TOK, w), dt) for w, dt in zip(widths, dtypes)],
        compiler_params=pltpu.CompilerParams(dimension_semantics=("arbitrary",), vmem_limit_bytes=VMEM_LIMIT),
        name="inproj",
    )(x, gain, mod, w_sections)


def _segsum(x, width):
    n = x.shape[-1]
    gi = lax.broadcasted_iota(jnp.int32, (n, n), 0) // width
    gj = lax.broadcasted_iota(jnp.int32, (n, n), 1) // width
    e = jnp.where(gi == gj, 1.0, 0.0).astype(BF16)
    hi = x.astype(BF16)
    lo = (x - hi.astype(F32)).astype(BF16)
    return jnp.dot(jnp.concatenate([hi, lo], axis=1), jnp.concatenate([e, e], axis=0),
                   preferred_element_type=F32)


def _swap_pairs(x):
    lane = lax.broadcasted_iota(jnp.int32, x.shape, 1)
    nxt = pltpu.roll(x, LANES - 1, 1)
    prv = pltpu.roll(x, 1, 1)
    return jnp.where(lane % 2 == 0, nxt, prv)


def _head_norm(x, gain):
    ms = _segsum(x * x, HD_A) * (1.0 / HD_A)
    return x * lax.rsqrt(ms + NORM_EPS) * gain


def _attn_kernel(*refs, t, latent):
    if latent:
        (pa_ref, qg_ref, kg_ref, ck_ref, cv_ref, cos_ref, sin_ref, o_ref, k_scr, v_scr, q_scr) = refs
    else:
        (pa_ref, qg_ref, kg_ref, o_ref, ko_ref, vo_ref, k_scr, v_scr, q_scr) = refs
    tk = k_scr.shape[0]
    past = tk - t
    kn = _head_norm(pa_ref[:, OFF_KA:OFF_VA], kg_ref[...])
    va = pa_ref[:, OFF_VA:OFF_QB]
    if latent:
        cos = cos_ref[...]
        sin = sin_ref[...]
        kn = kn * cos + _swap_pairs(kn) * sin
        k_scr[0:past, :] = ck_ref[...]
        v_scr[0:past, :] = cv_ref[...]
    else:
        ko_ref[...] = kn
        vo_ref[...] = va
    k_scr[past:tk, :] = kn
    v_scr[past:tk, :] = va
    for c in range(WIDTH_A // LANES):
        qn = _head_norm(pa_ref[:, c * LANES:(c + 1) * LANES], qg_ref[...])
        if latent:
            qn = qn * cos + _swap_pairs(qn) * sin
        q_scr[:, c * LANES:(c + 1) * LANES] = qn * (HD_A ** -0.5)

    group = N_HEADS_A // N_KV_A
    tq = min(t, 256)
    for kv in range(N_KV_A):
        kh = k_scr[:, kv * HD_A:(kv + 1) * HD_A].astype(BF16)
        vh = v_scr[:, kv * HD_A:(kv + 1) * HD_A].astype(BF16)
        for qb in range(t // tq):
            outs = [None] * group

            def one_head(g, kv=kv, qb=qb, kh=kh, vh=vh, outs=outs):
                h = kv * group + g
                qh = q_scr[qb * tq:(qb + 1) * tq, h * HD_A:(h + 1) * HD_A].astype(BF16)
                s = _dot_nt(qh, kh)
                yield
                m = jnp.max(s, axis=-1, keepdims=True)
                e = jnp.exp(s - m)
                pv = jnp.dot(e.astype(BF16), vh, preferred_element_type=F32)
                yield
                outs[g] = pv / jnp.sum(e, axis=-1, keepdims=True)

            _trace_in_lockstep(one_head, range(group))
            for j in range(group // 2):
                pair = jnp.concatenate([outs[2 * j], outs[2 * j + 1]], axis=1)
                col = (kv * group + 2 * j) * HD_A
                o_ref[qb * tq:(qb + 1) * tq, col:col + LANES] = pair


def _attention(pa, q_gain, k_gain, n_seq, t, row_block0, cache=None, rope=None):
    latent = cache is not None
    tk = t + (PAST_LEN if latent else 0)
    in_specs = [
        pl.BlockSpec((t, SEC_A), lambda s: (row_block0 + s, 0)),
        _resident((1, LANES)),
        _resident((1, LANES)),
    ]
    args = [pa, q_gain, k_gain]
    out_specs = [pl.BlockSpec((t, WIDTH_A), lambda s: (s, 0))]
    out_shape = [jax.ShapeDtypeStruct((n_seq * t, WIDTH_A), F32)]
    if latent:
        in_specs += [
            pl.BlockSpec((None, PAST_LEN, KV_A), lambda s: (s, 0, 0)),
            pl.BlockSpec((None, PAST_LEN, KV_A), lambda s: (s, 0, 0)),
            _resident((t, LANES)),
            _resident((t, LANES)),
        ]
        args += [cache[0], cache[1], rope[0], rope[1]]
    else:
        out_specs += [pl.BlockSpec((None, t, KV_A), lambda s: (s, 0, 0))] * 2
        out_shape += [jax.ShapeDtypeStruct((n_seq, t, KV_A), F32)] * 2
    return pl.pallas_call(
        functools.partial(_attn_kernel, t=t, latent=latent),
        grid=(n_seq,),
        in_specs=in_specs,
        out_specs=out_specs,
        out_shape=out_shape,
        scratch_shapes=[
            pltpu.VMEM((tk, KV_A), F32),
            pltpu.VMEM((tk, KV_A), F32),
            pltpu.VMEM((t, WIDTH_A), F32),
        ],
        compiler_params=pltpu.CompilerParams(dimension_semantics=("arbitrary",), vmem_limit_bytes=VMEM_LIMIT),
        name="attn_lat" if latent else "attn_ctx",
    )(*args)


def _rope_tables():
    tpos = np.arange(DEC_SEQ)
    row = (tpos // GRID_W).astype(np.float32)
    col = (tpos % GRID_W).astype(np.float32)
    n_pairs = HD_A // 4
    inv = (ROPE_THETA ** (-np.arange(n_pairs, dtype=np.float32) / n_pairs)).astype(np.float32)
    ang = np.concatenate([row[:, None] * inv, col[:, None] * inv], axis=-1)
    cos = np.repeat(np.cos(ang), 2, axis=-1)
    sin = np.repeat(np.sin(ang), 2, axis=-1)
    sign = np.tile(np.array([-1.0, 1.0], np.float32), HD_A // 2)
    reps = LANES // HD_A
    return (jnp.asarray(np.tile(cos, (1, reps)), F32), jnp.asarray(np.tile(sin * sign, (1, reps)), F32))


def _tri(n, reverse, strict):
    i = lax.broadcasted_iota(jnp.int32, (n, n), 0)
    j = lax.broadcasted_iota(jnp.int32, (n, n), 1)
    if reverse:
        keep = (j > i) if strict else (j >= i)
    else:
        keep = (j < i) if strict else (j <= i)
    return keep


_DONE = object()


def _trace_in_lockstep(step_fn, steps):
    _run_lockstep([step_fn(i) for i in steps])


def _run_lockstep(generators):
    live = list(generators)
    while live:
        live = [g for g in live if next(g, _DONE) is not _DONE]


def _for_each_step_group(step_fn, n_steps, group, first=0):
    count = n_steps - first
    if count <= group:
        _trace_in_lockstep(step_fn, range(first, n_steps))
        return
    assert count % group == 0, (n_steps, first, group)

    def body(j, carry):
        _trace_in_lockstep(step_fn, [first + j * group + k for k in range(group)])
        return carry

    lax.fori_loop(0, count // group, body, 0)


def _chunk_rows(index, c):
    if isinstance(index, int):
        return pl.ds(index * c, c)
    return pl.ds(pl.multiple_of(index * c, c), c)


def _head_masks(shape):
    lane = lax.broadcasted_iota(jnp.int32, shape, 1)
    first = lane < (LANES // 2)
    return first, jnp.logical_not(first)


def _log_sigmoid(z):
    return jnp.minimum(z, 0.0) - jnp.log1p(jnp.exp(-jnp.abs(z)))


def _gla_kernel(*refs, t, has_init, want_state):
    refs = list(refs)
    (q_ref, k_ref, v_ref, g_ref, lr_ref, w2_ref, gb_ref, gain_ref,
     cum_ref, incl_ref, nmask_ref) = refs[:11]
    pos = 11
    if has_init:
        s0_ref = refs[pos]
        pos += 1
    o_ref = refs[pos]
    pos += 1
    if want_state:
        so_ref = refs[pos]
        pos += 1
    lg_scr, o_scr, qh_scr, n_scr, dec_scr, st_scr = refs[pos:pos + 6]

    c = GLA_CHUNK
    n_chunks = t // c
    half = LANES // 2
    jj = lax.broadcasted_iota(jnp.int32, (half, LANES), 0)
    ll = lax.broadcasted_iota(jnp.int32, (half, LANES), 1)
    sel = [jnp.where(ll == jj + h * half, 1.0, 0.0).astype(F32) for h in range(2)]

    lr = lr_ref[...].astype(BF16)
    for d in range(2):
        z = jnp.dot(lr, w2_ref[d].astype(BF16), preferred_element_type=F32) + gb_ref[d]
        lg_scr[d] = _log_sigmoid(z) * (1.0 / GK_NORMALIZER)
        for h in range(2):
            if has_init:
                st_scr[d, h * DV_B:(h + 1) * DV_B, :] = _dot_tn(s0_ref[d, h], sel[h], precision=HI)
            else:
                st_scr[d, h * DV_B:(h + 1) * DV_B, :] = jnp.zeros((DV_B, LANES), F32)

    m0, m1 = _head_masks((c, LANES))
    scale = DK_B ** -0.5

    def chunk_terms(i):
        rows_f = _chunk_rows(i, c)
        rows_b = _chunk_rows(n_chunks - 1 - i, c)
        q = jnp.concatenate([q_ref[rows_f, :], q_ref[rows_b, :]], axis=0)
        k = jnp.concatenate([k_ref[rows_f, :], k_ref[rows_b, :]], axis=0)
        g = jnp.concatenate([lg_scr[0, rows_f, :], lg_scr[1, rows_b, :]], axis=0)
        b = _mm_exact_lhs(cum_ref[...], g)
        yield
        bm = jnp.concatenate([jnp.broadcast_to(b[c // 2:c // 2 + 1], (c, LANES)),
                              jnp.broadcast_to(b[c + c // 2:c + c // 2 + 1], (c, LANES))], axis=0)
        bl = jnp.concatenate([jnp.broadcast_to(b[c - 1:c], (c, LANES)),
                              jnp.broadcast_to(b[c:c + 1], (c, LANES))], axis=0)
        e1 = jnp.exp(b - bm)
        e2 = jnp.exp(bm - b)
        qt = q * scale * e1
        kt = k * e2
        qh_scr[i] = qt * jnp.exp(bm)
        kc = kt * jnp.exp(bl - bm)
        qf, qb = qt[0:c], qt[c:2 * c]
        lhs = jnp.concatenate([jnp.where(m0, qf, 0.0), jnp.where(m1, qf, 0.0),
                               jnp.where(m0, qb, 0.0), jnp.where(m1, qb, 0.0)], axis=0)
        rhs = jnp.concatenate([kt[0:c], kt[0:c], kt[c:2 * c], kt[c:2 * c]], axis=0)
        att = _mm(lhs, rhs, "nt", GLA_P_ATT) * incl_ref[...]
        yield
        vf = v_ref[rows_f, :]
        vb = v_ref[rows_b, :]
        vst = jnp.concatenate([vf[:, 0:DV_B], vf[:, DV_B:2 * DV_B], vb[:, 0:DV_B], vb[:, DV_B:2 * DV_B]], axis=0)
        o_scr[i] = _mm(att, vst, "nn", 1)
        yield
        n_scr[i, 0] = _mm(vf, kc[0:c], "tn", 1) * nmask_ref[...]
        yield
        n_scr[i, 1] = _mm(vb, kc[c:2 * c], "tn", 1) * nmask_ref[...]
        dec = jnp.exp(bl)
        dec_scr[i, 0] = dec[0:8]
        dec_scr[i, 1] = dec[c:c + 8]

    _for_each_step_group(chunk_terms, n_chunks, GLA_UNROLL)

    def state_scan(i, carry):
        for d in range(2):
            st = st_scr[d]
            st_scr[d] = st * dec_scr[i, d, 0:1, :] + n_scr[i, d]
            n_scr[i, d] = st
        return carry

    lax.fori_loop(0, n_chunks, state_scan, 0)

    def inter_terms(i):
        for d in range(2):
            oi = _mm(qh_scr[i, d * c:(d + 1) * c, :], n_scr[i, d], "nt", 1)
            yield
            for h in range(2):
                r0 = (2 * d + h) * c
                o_scr[i, r0:r0 + c, :] = o_scr[i, r0:r0 + c, :] + oi[:, h * DV_B:(h + 1) * DV_B]

    _for_each_step_group(inter_terms, n_chunks, GLA_UNROLL, first=0 if has_init else 1)

    for h in range(2):
        o = jnp.concatenate([o_scr[j, h * c:(h + 1) * c, :] + o_scr[n_chunks - 1 - j, (2 + h) * c:(3 + h) * c, :]
                             for j in range(n_chunks)], axis=0)
        ms = jnp.mean(o * o, axis=-1, keepdims=True)
        o = o * lax.rsqrt(ms + NORM_EPS) * gain_ref[...]
        gate = g_ref[:, h * DV_B:(h + 1) * DV_B]
        o_ref[:, h * DV_B:(h + 1) * DV_B] = o * (gate * _sigmoid(gate))
        if want_state:
            for d in range(2):
                so_ref[d, h] = _dot_nt(sel[h], st_scr[d, h * DV_B:(h + 1) * DV_B, :], precision=HI)


def _gla(pb, w2pad, gk_b, gain, n_seq, t, row_block0, s0=None, want_state=False):
    has_init = s0 is not None
    pair_w = 2 * DV_B
    state_spec = pl.BlockSpec((None, 2, 2, DK_B, DV_B), lambda s, p: (s, 0, p, 0, 0))
    in_specs = [
        pl.BlockSpec((t, LANES), lambda s, p: (row_block0 + s, p)),
        pl.BlockSpec((t, LANES), lambda s, p: (row_block0 + s, WIDTH_BK // LANES + p)),
        pl.BlockSpec((t, pair_w), lambda s, p: (row_block0 + s, (2 * WIDTH_BK) // pair_w + p)),
        pl.BlockSpec((t, pair_w), lambda s, p: (row_block0 + s, (2 * WIDTH_BK + WIDTH_BV) // pair_w + p)),
        pl.BlockSpec((t, LANES), lambda s, p: (row_block0 + s, SEC_B_MAIN // LANES)),
        pl.BlockSpec((2, LANES, LANES), lambda s, p: (0, 0, p)),
        pl.BlockSpec((2, 1, LANES), lambda s, p: (0, 0, p)),
        pl.BlockSpec((1, DV_B), lambda s, p: (0, 0)),
    ]
    args = [pb, pb, pb, pb, pb, w2pad, gk_b, gain]
    cum, _, incl, _, _ = _rwkv_chunk_constants()
    ri = np.arange(pair_w)[:, None] // DV_B
    li = np.arange(LANES)[None, :] // DK_B
    consts = (cum, incl, jnp.asarray((ri == li).astype(np.float32)))
    in_specs += [pl.BlockSpec(m.shape, lambda s, p: (0, 0)) for m in consts]
    args += list(consts)
    if has_init:
        in_specs.append(state_spec)
        args.append(s0)
    n_steps = t // GLA_CHUNK
    out_specs = [pl.BlockSpec((t, pair_w), lambda s, p: (s, p))]
    out_shape = [jax.ShapeDtypeStruct((n_seq * t, WIDTH_BV), F32)]
    if want_state:
        out_specs.append(state_spec)
        out_shape.append(jax.ShapeDtypeStruct((n_seq, 2, N_HEADS_B, DK_B, DV_B), F32))
    return pl.pallas_call(
        functools.partial(_gla_kernel, t=t, has_init=has_init, want_state=want_state),
        grid=(n_seq, N_HEADS_B // 2),
        in_specs=in_specs,
        out_specs=out_specs,
        out_shape=out_shape,
        scratch_shapes=[
            pltpu.VMEM((2, t, LANES), F32),
            pltpu.VMEM((n_steps, 4 * GLA_CHUNK, DV_B), F32),
            pltpu.VMEM((n_steps, 2 * GLA_CHUNK, LANES), F32),
            pltpu.VMEM((n_steps, 2, pair_w, LANES), F32),
            pltpu.VMEM((n_steps, 2, 8, LANES), F32),
            pltpu.VMEM((2, pair_w, LANES), F32),
        ],
        compiler_params=pltpu.CompilerParams(dimension_semantics=("arbitrary", "arbitrary")),
        name="gla_lat" if has_init else "gla_ctx",
    )(*args)


def _shift_mix(u, mu):
    t = u.shape[0]
    row = lax.broadcasted_iota(jnp.int32, u.shape, 0)
    prev = jnp.where(row == 0, 0.0, pltpu.roll(u, 1, 0))
    nxt = jnp.where(row == t - 1, 0.0, pltpu.roll(u, t - 1, 0))
    return u + mu * (0.5 * (prev + nxt) - u)


_DIMS = {"nn": (((1,), (0,)), ((), ())), "nt": (((1,), (1,)), ((), ())), "tn": (((0,), (0,)), ((), ()))}


def _split_bf16(x):
    hi = x.astype(BF16)
    lo = (x - hi.astype(F32)).astype(BF16)
    return hi, lo


def _mm(a, b, form, passes):
    dims = _DIMS[form]
    if passes == 6:
        return lax.dot_general(a, b, dims, preferred_element_type=F32, precision=HI)
    if passes == 1:
        return lax.dot_general(a.astype(BF16), b.astype(BF16), dims, preferred_element_type=F32)
    a_hi, a_lo = _split_bf16(a)
    b_hi, b_lo = _split_bf16(b)
    ka = dims[0][0][0]
    kb = dims[0][1][0]
    a_cat = jnp.concatenate([a_hi, a_lo, a_hi], axis=ka)
    b_cat = jnp.concatenate([b_hi, b_hi, b_lo], axis=kb)
    return lax.dot_general(a_cat, b_cat, dims, preferred_element_type=F32)


def _mm_exact_lhs(a, x):
    a16 = a.astype(BF16)
    x_hi, x_lo = _split_bf16(x)
    return jnp.dot(jnp.concatenate([a16, a16], axis=1), jnp.concatenate([x_hi, x_lo], axis=0),
                   preferred_element_type=F32)


GLA_P_ATT = 1
GLA_UNROLL = 4

RW_UNROLL = 4


def _half_neumann(n_mat, eye, c):
    sw = n_mat.shape[0]
    p = eye + n_mat
    m = _mm(n_mat, n_mat, "nn", 1)
    yield
    for _ in range(int(math.log2(c)) - 3):
        both = _mm(jnp.concatenate([m, p], axis=0), m, "nn", 1)
        yield
        m = both[0:sw]
        p = p + both[sw:2 * sw]
    return p + _mm(p, m, "nn", 1)


def _rwkv_kernel(*refs, t, has_init, want_state):
    refs = list(refs)
    (r_ref, k_ref, v_ref, wl_ref, al_ref, gl_ref,
     mur_ref, muk_ref, muv_ref, muw_ref, mua_ref, mug_ref,
     w0_ref, w2_ref, a0_ref, a2_ref, g2_ref, kk_ref, ka_ref, rk_ref, lw_ref, lb_ref,
     cum_ref, strict_ref, incl_ref, blk_ref, eye_ref) = refs[:27]
    pos = 27
    if has_init:
        s0_ref = refs[pos]
        pos += 1
    o_ref = refs[pos]
    pos += 1
    if want_state:
        so_ref = refs[pos]
        pos += 1
    r_scr, v_scr, a_scr, dir_scr, y_scr, rp_scr, m_scr, n_scr, dec_scr, st_scr = refs[pos:pos + 10]

    c = RW_CHUNK
    n_chunks = t // c
    half = LANES // 2

    r = _shift_mix(r_ref[...], mur_ref[...])
    kc = _shift_mix(k_ref[...], muk_ref[...])
    vc = _shift_mix(v_ref[...], muv_ref[...])
    wl_in = jnp.tanh(_shift_mix(wl_ref[...], muw_ref[...]))
    al_in = _shift_mix(al_ref[...], mua_ref[...])
    g_in = _sigmoid(_shift_mix(gl_ref[...], mug_ref[...]))

    kkf = kc * kk_ref[...]
    kk = kkf * lax.rsqrt(_segsum(kkf * kkf, HD_C) + 1e-12)
    r_scr[...] = r
    v_scr[...] = vc
    a_scr[...] = -kk
    bonus = jnp.zeros_like(r)
    for d in range(2):
        wl = w0_ref[d] + _bdot(wl_in, w2_ref[d])
        dir_scr[d, 0] = -math.exp(-0.5) * _sigmoid(wl)
        a = _sigmoid(a0_ref[d] + _bdot(al_in, a2_ref[d]))
        kd = kc * (1.0 + (a - 1.0) * ka_ref[...])
        dir_scr[d, 1] = kd
        dir_scr[d, 2] = kk * a
        bonus = bonus + _segsum(r * kd * rk_ref[...], HD_C)
    bonus = bonus * vc
    g_c = _bdot(g_in, g2_ref[...])

    nblk = 2 * (LANES // HD_C)
    sw = nblk * HD_C
    if has_init:
        z = jnp.zeros((HD_C, HD_C), F32)
        blocks = [s0_ref[0, 0], s0_ref[0, 1], s0_ref[1, 0], s0_ref[1, 1]]
        st_scr[...] = jnp.concatenate(
            [jnp.concatenate([blocks[i] if i == j else z for j in range(nblk)], axis=1) for i in range(nblk)], axis=0)
    else:
        st_scr[...] = jnp.zeros((sw, sw), F32)

    m0, m1 = _head_masks((c, LANES))

    def per_head(x):
        xf, xb = x[0:c], x[c:2 * c]
        return jnp.concatenate([jnp.where(m0, xf, 0.0), jnp.where(m1, xf, 0.0),
                                jnp.where(m0, xb, 0.0), jnp.where(m1, xb, 0.0)], axis=0)

    def per_block(x):
        xf, xb = x[0:c], x[c:2 * c]
        return jnp.concatenate([xf, xf, xb, xb], axis=0)

    def side_by_side(x):
        return jnp.concatenate([x[0:c], x[c:2 * c]], axis=1)

    def heads_select(x):
        return jnp.concatenate([jnp.where(m0, x[0:c], x[c:2 * c]), jnp.where(m0, x[2 * c:3 * c], x[3 * c:4 * c])],
                               axis=1)

    def heads_add(x):
        return jnp.concatenate([x[0:c] + x[c:2 * c], x[2 * c:3 * c] + x[3 * c:4 * c]], axis=1)

    def chunk_terms(i):
        rows_f = _chunk_rows(i, c)
        rows_b = _chunk_rows(n_chunks - 1 - i, c)

        def both(ref_f, ref_b):
            return jnp.concatenate([ref_f[rows_f, :], ref_b[rows_b, :]], axis=0)

        rr = both(r_scr, r_scr)
        vv = both(v_scr, v_scr)
        aa = both(a_scr, a_scr)
        lw = both(dir_scr.at[0, 0], dir_scr.at[1, 0])
        kd = both(dir_scr.at[0, 1], dir_scr.at[1, 1])
        bb = both(dir_scr.at[0, 2], dir_scr.at[1, 2])
        cs = _mm_exact_lhs(cum_ref[...], lw)
        yield
        mid = lambda x: jnp.concatenate([jnp.broadcast_to(x[c // 2:c // 2 + 1], (c, LANES)),
                                         jnp.broadcast_to(x[c + c // 2:c + c // 2 + 1], (c, LANES))], axis=0)
        last = lambda x: jnp.concatenate([jnp.broadcast_to(x[c - 1:c], (c, LANES)),
                                          jnp.broadcast_to(x[c:c + 1], (c, LANES))], axis=0)
        cm = mid(cs)
        cl = last(cs)
        e1 = jnp.exp(cs - cm)
        e2 = jnp.exp(cm - cs)
        ecm = jnp.exp(cm)
        ecl = jnp.exp(cl - cm)
        rt = rr * e1
        at = aa * e1 * jnp.exp(-lw)
        bt = bb * e2
        kt = kd * e2
        bc = bt * ecl
        kcl = kt * ecl
        lhs = jnp.concatenate([per_head(at), per_head(rt)], axis=0)
        gram = _mm(lhs, jnp.concatenate([per_block(bt), per_block(kt)], axis=0), "nt", 1)
        yield
        a_ab = gram[0:sw, 0:sw] * strict_ref[...]
        a_rb = gram[sw:2 * sw, 0:sw] * incl_ref[...]
        a_ak = gram[0:sw, sw:2 * sw] * strict_ref[...]
        a_rk = gram[sw:2 * sw, sw:2 * sw] * incl_ref[...]
        p_half = yield from _half_neumann(a_ab, eye_ref[...], c)
        yield
        vst = per_block(vv)
        av = _mm(a_ak, vst, "nn", 1)
        yield
        rhs = jnp.concatenate([av, per_head(at * ecm)], axis=1)
        wa = _mm(p_half, rhs, "nn", 1)
        yield
        res = rhs - wa + _mm(a_ab, wa, "nn", 3)
        yield
        wa = wa + _mm(p_half, res, "nn", 1)
        yield
        ya = _mm(jnp.concatenate([a_rb, a_rk], axis=1),
                 jnp.concatenate([wa, jnp.concatenate([vst, jnp.zeros_like(vst)], axis=1)], axis=0), "nn", 1)
        yield
        rp = per_head(rt * ecm) + ya[:, LANES:2 * LANES]
        w1 = heads_select(wa[:, 0:LANES])
        ap = heads_add(wa[:, LANES:2 * LANES])
        y_scr[i] = heads_select(ya[:, 0:LANES])
        rp_scr[i] = heads_add(rp)
        vv2 = side_by_side(vv)
        lhs_t = jnp.concatenate([jnp.concatenate([ap, w1], axis=1),
                                 jnp.concatenate([jnp.zeros_like(vv2), vv2], axis=1)], axis=0)
        mn = _mm(lhs_t, jnp.concatenate([side_by_side(bc), side_by_side(kcl)], axis=0), "tn", 1)
        m_scr[i] = mn[0:sw] * blk_ref[...]
        n_scr[i] = mn[sw:2 * sw] * blk_ref[...]
        dec_scr[i] = jnp.broadcast_to(side_by_side(jnp.exp(cl))[0:1], (8, 2 * LANES))

    def scan_steps(steps):
        for i in steps:
            st = st_scr[...]
            y_scr[i] = y_scr[i] + _mm(rp_scr[i], st, "nt", 1)
            yield
            st_scr[...] = st * dec_scr[i, 0:1, :] + _mm(st, m_scr[i], "nn", 1) + n_scr[i]
            yield

    group = RW_UNROLL
    if n_chunks <= group:
        _run_lockstep([chunk_terms(i) for i in range(n_chunks)])
        if has_init:
            _run_lockstep([scan_steps(range(n_chunks))])
        else:
            st_scr[...] = n_scr[0]
            _run_lockstep([scan_steps(range(1, n_chunks))])
    else:
        n_groups = n_chunks // group
        _run_lockstep([chunk_terms(k) for k in range(group)])

        def terms_and_scan(j, carry):
            _run_lockstep([chunk_terms(j * group + k) for k in range(group)]
                          + [scan_steps([(j - 1) * group + k for k in range(group)])])
            return carry

        lax.fori_loop(1, n_groups, terms_and_scan, 0)
        _run_lockstep([scan_steps(range((n_groups - 1) * group, n_chunks))])

    y = jnp.concatenate([y_scr[j, :, 0:LANES] + y_scr[n_chunks - 1 - j, :, LANES:2 * LANES]
                         for j in range(n_chunks)], axis=0)
    mean = _segsum(y, HD_C) * (1.0 / HD_C)
    yc = y - mean
    var = _segsum(yc * yc, HD_C) * (1.0 / HD_C)
    gn = yc * lax.rsqrt(var + GN_EPS) * lw_ref[...] + lb_ref[...]
    o_ref[...] = (gn + bonus) * g_c
    if want_state:
        for d in range(2):
            for h in range(2):
                b0 = (2 * d + h) * HD_C
                so_ref[d, h] = st_scr[b0:b0 + HD_C, b0:b0 + HD_C]


def _rwkv_chunk_constants():
    c = RW_CHUNK
    sw = 2 * LANES
    i = np.arange(sw)[:, None]
    j = np.arange(sw)[None, :]
    same = (i // c) == (j // c)
    bwd = i >= LANES
    strict = same & np.where(bwd, j > i, j < i)
    incl = same & np.where(bwd, j >= i, j <= i)
    blk = (i // HD_C) == (j // HD_C)
    ci = np.arange(2 * c)[:, None]
    cj = np.arange(2 * c)[None, :]
    cum = ((ci // c) == (cj // c)) & np.where(ci >= c, cj >= ci, cj <= ci)
    f = lambda m: jnp.asarray(m.astype(np.float32))
    return f(cum), f(strict), f(incl), f(blk), f(np.eye(sw))


def _rwkv(pc, mu_c, w0, w2pad, a0, a2pad, g2, k_k, k_a, r_k, lnx_w, lnx_b, n_seq, t, row_block0,
          s0=None, want_state=False):
    has_init = s0 is not None
    n_pairs = WIDTH_C // LANES
    blk_w = 3 * n_pairs
    blk_a = blk_w + 1
    blk_g = blk_w + 2
    state_spec = pl.BlockSpec((None, 2, 2, HD_C, HD_C), lambda s, p: (s, 0, p, 0, 0))

    def sec(col_fn):
        return pl.BlockSpec((t, LANES), lambda s, p: (row_block0 + s, col_fn(p)))

    def vec(col_fn):
        return pl.BlockSpec((1, LANES), lambda s, p: (0, col_fn(p)))

    col_fns = [lambda p: p, lambda p: n_pairs + p, lambda p: 2 * n_pairs + p,
               lambda p: blk_w, lambda p: blk_a, lambda p: blk_g]
    in_specs = [sec(f) for f in col_fns] + [vec(f) for f in col_fns]
    in_specs += [
        pl.BlockSpec((2, 1, LANES), lambda s, p: (0, 0, p)),
        pl.BlockSpec((2, LANES, LANES), lambda s, p: (0, 0, p)),
        pl.BlockSpec((2, 1, LANES), lambda s, p: (0, 0, p)),
        pl.BlockSpec((2, LANES, LANES), lambda s, p: (0, 0, p)),
        pl.BlockSpec((LANES, LANES), lambda s, p: (0, p)),
    ] + [vec(lambda p: p)] * 5
    args = [pc] * 6 + [mu_c] * 6 + [w0, w2pad, a0, a2pad, g2, k_k, k_a, r_k, lnx_w, lnx_b]
    consts = _rwkv_chunk_constants()
    in_specs += [pl.BlockSpec(m.shape, lambda s, p: (0, 0)) for m in consts]
    args += list(consts)
    if has_init:
        in_specs.append(state_spec)
        args.append(s0)
    out_specs = [pl.BlockSpec((t, LANES), lambda s, p: (s, p))]
    out_shape = [jax.ShapeDtypeStruct((n_seq * t, WIDTH_C), F32)]
    if want_state:
        out_specs.append(state_spec)
        out_shape.append(jax.ShapeDtypeStruct((n_seq, 2, N_HEADS_C, HD_C, HD_C), F32))
    n_steps = t // RW_CHUNK
    sw = 2 * LANES
    return pl.pallas_call(
        functools.partial(_rwkv_kernel, t=t, has_init=has_init, want_state=want_state),
        grid=(n_seq, n_pairs),
        in_specs=in_specs,
        out_specs=out_specs,
        out_shape=out_shape,
        scratch_shapes=[
            pltpu.VMEM((t, LANES), F32),
            pltpu.VMEM((t, LANES), F32),
            pltpu.VMEM((t, LANES), F32),
            pltpu.VMEM((2, 3, t, LANES), F32),
            pltpu.VMEM((n_steps, RW_CHUNK, sw), F32),
            pltpu.VMEM((n_steps, RW_CHUNK, sw), F32),
            pltpu.VMEM((n_steps, sw, sw), F32),
            pltpu.VMEM((n_steps, sw, sw), F32),
            pltpu.VMEM((n_steps, 8, sw), F32),
            pltpu.VMEM((sw, sw), F32),
        ],
        compiler_params=pltpu.CompilerParams(dimension_semantics=("arbitrary", "arbitrary")),
        name="rwkv_lat" if has_init else "rwkv_ctx",
    )(*args)


def _merge_kernel(x_ref, oac_ref, oal_ref, obc_ref, obl_ref, occ_ref, ocl_ref, pg_ref, mod_ref,
                  wa_ref, wb_ref, wc_ref, wo_ref, o_ref, *, ctx_tiles):
    is_ctx = pl.program_id(0) < ctx_tiles
    merged = jnp.zeros(x_ref.shape, F32)
    for j, (c_ref, l_ref, w_ref) in enumerate(((oac_ref, oal_ref, wa_ref), (obc_ref, obl_ref, wb_ref),
                                              (occ_ref, ocl_ref, wc_ref))):
        gate = pg_ref[:, j * D_MODEL:(j + 1) * D_MODEL].astype(F32)
        branch = jnp.where(is_ctx, c_ref[...], l_ref[...]).astype(BF16)
        merged = merged + gate * jnp.dot(branch, w_ref[...], preferred_element_type=F32)
    y = jnp.dot(merged.astype(BF16), wo_ref[...], preferred_element_type=F32)
    o_ref[...] = x_ref[...] + mod_ref[2] * y


def _merge(li, x, branches, pg, mod, wpa, wpb, wpc, wout):
    tm = TM
    ctx_tiles = N_TOK_CTX // tm
    row = lambda w: pl.BlockSpec((tm, w), lambda i: (i, 0))
    ctx_row = lambda w: pl.BlockSpec((tm, w), lambda i: (jnp.minimum(i, ctx_tiles - 1), 0))
    lat_row = lambda w: pl.BlockSpec((tm, w), lambda i: (jnp.maximum(i - ctx_tiles, 0), 0))
    branch_specs = []
    for w in (WIDTH_A, WIDTH_BV, WIDTH_C):
        branch_specs += [ctx_row(w), lat_row(w)]
    return pl.pallas_call(
        functools.partial(_merge_kernel, ctx_tiles=ctx_tiles),
        grid=(N_TOK // tm,),
        in_specs=[row(D_MODEL)] + branch_specs + [row(SEC_G), _mod_spec(li, tm),
                  _layer_resident(li, (WIDTH_A, D_MODEL)), _layer_resident(li, (WIDTH_BV, D_MODEL)),
                  _layer_resident(li, (WIDTH_C, D_MODEL)), _layer_resident(li, (D_MODEL, D_MODEL))],
        out_specs=row(D_MODEL),
        out_shape=jax.ShapeDtypeStruct((N_TOK, D_MODEL), F32),
        compiler_params=pltpu.CompilerParams(dimension_semantics=("arbitrary",), vmem_limit_bytes=VMEM_LIMIT),
        name="merge",
    )(x, *[b for pair in branches for b in pair], pg, mod, wpa, wpb, wpc, wout)


def _ffn_kernel(x_ref, g_ref, mod_ref, wg_ref, wu_ref, wo_ref, o_ref, h_scr, acc_scr):
    f = pl.program_id(1)

    @pl.when(f == 0)
    def _():
        h_scr[...] = _modnorm(x_ref[...], g_ref[...], mod_ref[3], mod_ref[4]).astype(BF16)
        acc_scr[...] = jnp.zeros_like(acc_scr)

    h = h_scr[...]
    gate = jnp.dot(h, wg_ref[...], preferred_element_type=F32)
    up = jnp.dot(h, wu_ref[...], preferred_element_type=F32)
    act = (gate * _sigmoid(gate) * up).astype(BF16)
    acc_scr[...] += jnp.dot(act, wo_ref[...], preferred_element_type=F32)

    @pl.when(f == pl.num_programs(1) - 1)
    def _():
        o_ref[...] = x_ref[...] + mod_ref[5] * acc_scr[...]


def _ffn(li, x, gain, mod, w_in, w_out):
    tm = TM_FFN
    nf = D_FF // TF
    return pl.pallas_call(
        _ffn_kernel,
        grid=(N_TOK // tm, nf),
        in_specs=[
            pl.BlockSpec((tm, D_MODEL), lambda i, f: (i, 0)),
            pl.BlockSpec((None, 1, D_MODEL), lambda i, f: (li, 0, 0)),
            _mod_spec(li, tm),
            pl.BlockSpec((None, D_MODEL, TF), lambda i, f: (li, 0, f)),
            pl.BlockSpec((None, D_MODEL, TF), lambda i, f: (li, 0, nf + f)),
            pl.BlockSpec((None, TF, D_MODEL), lambda i, f: (li, f, 0)),
        ],
        out_specs=pl.BlockSpec((tm, D_MODEL), lambda i, f: (i, 0)),
        out_shape=jax.ShapeDtypeStruct((N_TOK, D_MODEL), F32),
        scratch_shapes=[pltpu.VMEM((tm, D_MODEL), BF16), pltpu.VMEM((tm, D_MODEL), F32)],
        compiler_params=pltpu.CompilerParams(dimension_semantics=("arbitrary", "arbitrary"),
                                             vmem_limit_bytes=VMEM_LIMIT),
        name="ffn",
    )(x, gain, mod, w_in, w_in, w_out)


def _pad_rows(w, offset, total):
    pads = [(0, 0)] * w.ndim
    pads[-2] = (offset, total - offset - w.shape[-2])
    return jnp.pad(w, pads)


def kernel(x_prompt, x_sample, cache_k, cache_v, state_gla, state_rwkv, c, c_ctx, w_ada, b_ada, g_mix, g_ffn,
           w_in, q_gain, k_gain, gk_w2, gk_b, gla_gain, mu_c, w0, w2, a0, a2, g2, k_k, k_a, r_k, lnx_w, lnx_b,
           w_po_a, w_po_b, w_po_c, w_out, w_ffn_in, w_ffn_out):
    x = jnp.concatenate([x_prompt.reshape(N_TOK_CTX, D_MODEL), x_sample.reshape(N_TOK_LAT, D_MODEL)], axis=0)
    cond = jnp.concatenate([c_ctx[None, :], c, jnp.zeros((COND_ROWS - N_COND, D_MODEL), F32)], axis=0)
    mod = _modulation(cond, w_ada, b_ada)[:, :N_COND].reshape(DEPTH, N_COND, 6, 1, D_MODEL)

    gap = jnp.zeros((DEPTH, D_MODEL, SEC_B - (OFF_C - OFF_QB)), BF16)
    w_sections = jnp.concatenate([w_in[:, :, :OFF_C].astype(BF16), gap, w_in[:, :, OFF_C:].astype(BF16)], axis=2)
    gk_w2p = jnp.stack([_pad_rows(gk_w2[:, 0], 0, LANES), _pad_rows(gk_w2[:, 1], GK_RANK, LANES)], axis=1)
    w2p = jnp.stack([_pad_rows(w2[:, 0], 0, LANES), _pad_rows(w2[:, 1], W_LORA, LANES)], axis=1)
    a2p = jnp.stack([_pad_rows(a2[:, 0], 0, LANES), _pad_rows(a2[:, 1], A_LORA, LANES)], axis=1)
    wpa, wpb, wpc, wout = (w.astype(BF16) for w in (w_po_a, w_po_b, w_po_c, w_out))
    wfi = w_ffn_in.astype(BF16)
    wfo = w_ffn_out.astype(BF16)
    rope = _rope_tables()
    lat_blk = N_TOK_CTX // DEC_SEQ
    g_mix3 = g_mix.reshape(DEPTH, 1, D_MODEL)
    g_ffn3 = g_ffn.reshape(DEPTH, 1, D_MODEL)

    ks, vs, sgs, srs = [], [], [], []
    for li in range(DEPTH):
        vec = lambda a: a[li].reshape(1, -1)
        pa, pb, pc, pg = _inproj(li, x, g_mix3, mod, w_sections)

        qg = jnp.tile(vec(q_gain), (1, LANES // HD_A))
        kg = jnp.tile(vec(k_gain), (1, LANES // HD_A))
        oa_c, k_l, v_l = _attention(pa, qg, kg, BATCH, SEQ, 0)
        (oa_l,) = _attention(pa, qg, kg, DEC_BATCH, DEC_SEQ, lat_blk,
                             cache=(cache_k[:, li].reshape(DEC_BATCH, PAST_LEN, KV_A),
                                    cache_v[:, li].reshape(DEC_BATCH, PAST_LEN, KV_A)), rope=rope)

        gkb = gk_b[li].reshape(2, 1, WIDTH_BK)
        ob_c, sg_l = _gla(pb, gk_w2p[li], gkb, vec(gla_gain), BATCH, SEQ, 0, want_state=True)
        (ob_l,) = _gla(pb, gk_w2p[li], gkb, vec(gla_gain), DEC_BATCH, DEC_SEQ, lat_blk, s0=state_gla[:, li])

        rw_args = (pc, vec(mu_c), w0[li].reshape(2, 1, WIDTH_C), w2p[li], a0[li].reshape(2, 1, WIDTH_C), a2p[li],
                   g2[li], vec(k_k), vec(k_a), r_k[li].reshape(1, WIDTH_C), vec(lnx_w), vec(lnx_b))
        oc_c, sr_l = _rwkv(*rw_args, BATCH, SEQ, 0, want_state=True)
        (oc_l,) = _rwkv(*rw_args, DEC_BATCH, DEC_SEQ, lat_blk, s0=state_rwkv[:, li])

        x = _merge(li, x, ((oa_c, oa_l), (ob_c, ob_l), (oc_c, oc_l)), pg, mod, wpa, wpb, wpc, wout)
        x = _ffn(li, x, g_ffn3, mod, wfi, wfo)

        ks.append(k_l.reshape(BATCH, SEQ, N_KV_A, HD_A))
        vs.append(v_l.reshape(BATCH, SEQ, N_KV_A, HD_A))
        sgs.append(sg_l)
        srs.append(sr_l)

    y_prompt = x[:N_TOK_CTX].reshape(BATCH, SEQ, D_MODEL)
    y_sample = x[N_TOK_CTX:].reshape(DEC_BATCH, DEC_SEQ, D_MODEL)
    return (y_prompt, y_sample, jnp.stack(ks, axis=1), jnp.stack(vs, axis=1),
            jnp.stack(sgs, axis=1), jnp.stack(srs, axis=1))
```

```python
import functools
import math

import numpy as np
import jax
import jax.numpy as jnp
from jax import lax
from jax.experimental import pallas as pl
from jax.experimental.pallas import tpu as pltpu

D_MODEL = 1024
BATCH = 16
SEQ = 256
DEPTH = 4
DEC_BATCH = 2
DEC_SEQ = 1024
PAST_LEN = 256
GRID_W = 64
N_HEADS_A = 8
N_KV_A = 2
HD_A = 64
ROPE_THETA = 10000.0
N_HEADS_B = 4
DK_B = 64
DV_B = 128
GK_RANK = 16
GK_NORMALIZER = 16.0
N_HEADS_C = 8
HD_C = 64
W_LORA = 64
A_LORA = 64
G_LORA = 128
GN_EPS = 64e-5
NORM_EPS = 1e-6
WIDTH_A = N_HEADS_A * HD_A
KV_A = N_KV_A * HD_A
WIDTH_BK = N_HEADS_B * DK_B
WIDTH_BV = N_HEADS_B * DV_B
WIDTH_C = N_HEADS_C * HD_C
N_BRANCH = 3
D_FF = ((8 * D_MODEL + 3 * 256 - 1) // (3 * 256)) * 256
OFF_QA = 0
OFF_KA = OFF_QA + WIDTH_A
OFF_VA = OFF_KA + KV_A
OFF_QB = OFF_VA + KV_A
OFF_KB = OFF_QB + WIDTH_BK
OFF_VB = OFF_KB + WIDTH_BK
OFF_GB = OFF_VB + WIDTH_BV
OFF_GKF = OFF_GB + WIDTH_BV
OFF_GKB = OFF_GKF + GK_RANK
OFF_C = OFF_GKB + GK_RANK
C_COLS = 3 * WIDTH_C + 2 * W_LORA + 2 * A_LORA + G_LORA
OFF_GATE = OFF_C + C_COLS
N_IN = OFF_GATE + N_BRANCH * D_MODEL

F32 = jnp.float32
BF16 = jnp.bfloat16

LANES = 128
VMEM_LIMIT = 56 * 1024 * 1024

N_TOK_CTX = BATCH * SEQ
N_TOK_LAT = DEC_BATCH * DEC_SEQ
N_TOK = N_TOK_CTX + N_TOK_LAT
N_COND = 1 + DEC_BATCH
COND_ROWS = 8
SEC_A = OFF_QB
SEC_B_MAIN = OFF_GKF - OFF_QB
SEC_B = SEC_B_MAIN + LANES
SEC_C = C_COLS
SEC_G = N_BRANCH * D_MODEL
TM = 512
TM_FFN = 1024
TM_INPROJ = 256
TF = D_FF // 2
GLA_CHUNK = 64
RW_CHUNK = 64
HI = lax.Precision.HIGHEST


def _tile_group(i, tm):
    ctx_tiles = N_TOK_CTX // tm
    lat_tiles = DEC_SEQ // tm
    return jnp.where(i < ctx_tiles, 0, 1 + (i - ctx_tiles) // lat_tiles)


def _bdot(a, b):
    return jnp.dot(a.astype(BF16), b.astype(BF16), preferred_element_type=F32)


def _dot_nt(a, b, precision=None):
    return lax.dot_general(a, b, (((1,), (1,)), ((), ())), preferred_element_type=F32, precision=precision)


def _dot_tn(a, b, precision=None):
    return lax.dot_general(a, b, (((0,), (0,)), ((), ())), preferred_element_type=F32, precision=precision)


def _sigmoid(x):
    return jax.nn.sigmoid(x)


def _modnorm(x, gain, shift, scale):
    ms = jnp.mean(x * x, axis=-1, keepdims=True)
    y = x * lax.rsqrt(ms + NORM_EPS) * gain
    return y * (1.0 + scale) + shift


def _mod_kernel(c_ref, w_ref, b_ref, o_ref):
    c = c_ref[...]
    s = c * _sigmoid(c)
    o_ref[...] = _bdot(s, w_ref[...]) + b_ref[...]


def _modulation(cond, w_ada, b_ada):
    tn = 1536
    n_out = 6 * D_MODEL
    return pl.pallas_call(
        _mod_kernel,
        grid=(DEPTH, n_out // tn),
        in_specs=[
            pl.BlockSpec((COND_ROWS, D_MODEL), lambda l, j: (0, 0)),
            pl.BlockSpec((None, D_MODEL, tn), lambda l, j: (l, 0, j)),
            pl.BlockSpec((None, 1, tn), lambda l, j: (l, 0, j)),
        ],
        out_specs=pl.BlockSpec((None, COND_ROWS, tn), lambda l, j: (l, 0, j)),
        out_shape=jax.ShapeDtypeStruct((DEPTH, COND_ROWS, n_out), F32),
        compiler_params=pltpu.CompilerParams(dimension_semantics=("arbitrary", "arbitrary")),
        name="modulation",
    )(cond, w_ada, b_ada.reshape(DEPTH, 1, n_out))


def _inproj_kernel(x_ref, g_ref, mod_ref, wab_ref, wcg_ref, oa_ref, ob_ref, oc_ref, og_ref):
    h = _modnorm(x_ref[...], g_ref[...], mod_ref[0], mod_ref[1]).astype(BF16)
    oa_ref[...] = jnp.dot(h, wab_ref[:, 0:SEC_A], preferred_element_type=F32)
    ob_ref[...] = jnp.dot(h, wab_ref[:, SEC_A:SEC_A + SEC_B], preferred_element_type=F32)
    oc_ref[...] = jnp.dot(h, wcg_ref[:, 0:SEC_C], preferred_element_type=F32)
    og_ref[...] = _sigmoid(jnp.dot(h, wcg_ref[:, SEC_C:SEC_C + SEC_G], preferred_element_type=F32)).astype(BF16)


def _resident(shape):
    nd = len(shape)
    return pl.BlockSpec(shape, lambda *_: (0,) * nd, pipeline_mode=pl.Buffered(1))


def _layer_resident(li, shape):
    nd = len(shape)
    return pl.BlockSpec((None,) + tuple(shape), lambda *_: (li,) + (0,) * nd, pipeline_mode=pl.Buffered(1))


def _mod_spec(li, tm):
    return pl.BlockSpec((None, None, 6, 1, D_MODEL), lambda i, *_: (li, _tile_group(i, tm), 0, 0, 0))


def _inproj(li, x, gain, mod, w_ab, w_cg):
    tm = TM_INPROJ
    widths = (SEC_A, SEC_B, SEC_C, SEC_G)
    dtypes = (F32, F32, F32, BF16)
    return pl.pallas_call(
        _inproj_kernel,
        grid=(N_TOK // tm,),
        in_specs=[
            pl.BlockSpec((tm, D_MODEL), lambda i: (i, 0)),
            _layer_resident(li, (1, D_MODEL)),
            _mod_spec(li, tm),
            _layer_resident(li, (D_MODEL, SEC_A + SEC_B)),
            _layer_resident(li, (D_MODEL, SEC_C + SEC_G)),
        ],
        out_specs=[pl.BlockSpec((tm, w), lambda i: (i, 0)) for w in widths],
        out_shape=[jax.ShapeDtypeStruct((N_TOK, w), dt) for w, dt in zip(widths, dtypes)],
        compiler_params=pltpu.CompilerParams(dimension_semantics=("arbitrary",), vmem_limit_bytes=VMEM_LIMIT),
        name="inproj",
    )(x, gain, mod, w_ab, w_cg)


def _segsum(x, width):
    n = x.shape[-1]
    gi = lax.broadcasted_iota(jnp.int32, (n, n), 0) // width
    gj = lax.broadcasted_iota(jnp.int32, (n, n), 1) // width
    e = jnp.where(gi == gj, 1.0, 0.0).astype(BF16)
    hi = x.astype(BF16)
    lo = (x - hi.astype(F32)).astype(BF16)
    return jnp.dot(jnp.concatenate([hi, lo], axis=1), jnp.concatenate([e, e], axis=0),
                   preferred_element_type=F32)


def _swap_pairs(x):
    lane = lax.broadcasted_iota(jnp.int32, x.shape, 1)
    nxt = pltpu.roll(x, LANES - 1, 1)
    prv = pltpu.roll(x, 1, 1)
    return jnp.where(lane % 2 == 0, nxt, prv)


def _head_norm(x, gain):
    ms = _segsum(x * x, HD_A) * (1.0 / HD_A)
    return x * lax.rsqrt(ms + NORM_EPS) * gain


def _attn_kernel(*refs, t, latent):
    if latent:
        (pa_ref, qg_ref, kg_ref, ck_ref, cv_ref, cos_ref, sin_ref, o_ref, k_scr, v_scr, q_scr) = refs
    else:
        (pa_ref, qg_ref, kg_ref, o_ref, ko_ref, vo_ref, k_scr, v_scr, q_scr) = refs
    tk = k_scr.shape[0]
    past = tk - t
    kn = _head_norm(pa_ref[:, OFF_KA:OFF_VA], kg_ref[...])
    va = pa_ref[:, OFF_VA:OFF_QB]
    if latent:
        cos = cos_ref[...]
        sin = sin_ref[...]
        kn = kn * cos + _swap_pairs(kn) * sin
        k_scr[0:past, :] = ck_ref[...]
        v_scr[0:past, :] = cv_ref[...]
    else:
        ko_ref[...] = kn
        vo_ref[...] = va
    k_scr[past:tk, :] = kn
    v_scr[past:tk, :] = va
    for c in range(WIDTH_A // LANES):
        qn = _head_norm(pa_ref[:, c * LANES:(c + 1) * LANES], qg_ref[...])
        if latent:
            qn = qn * cos + _swap_pairs(qn) * sin
        q_scr[:, c * LANES:(c + 1) * LANES] = qn * (HD_A ** -0.5)

    group = N_HEADS_A // N_KV_A
    tq = min(t, 256)
    for kv in range(N_KV_A):
        kh = k_scr[:, kv * HD_A:(kv + 1) * HD_A].astype(BF16)
        vh = v_scr[:, kv * HD_A:(kv + 1) * HD_A].astype(BF16)
        for qb in range(t // tq):
            outs = [None] * group

            def one_head(g, kv=kv, qb=qb, kh=kh, vh=vh, outs=outs):
                h = kv * group + g
                qh = q_scr[qb * tq:(qb + 1) * tq, h * HD_A:(h + 1) * HD_A].astype(BF16)
                s = _dot_nt(qh, kh)
                yield
                m = jnp.max(s, axis=-1, keepdims=True)
                e = jnp.exp(s - m)
                pv = jnp.dot(e.astype(BF16), vh, preferred_element_type=F32)
                yield
                outs[g] = pv / jnp.sum(e, axis=-1, keepdims=True)

            _trace_in_lockstep(one_head, range(group))
            for j in range(group // 2):
                pair = jnp.concatenate([outs[2 * j], outs[2 * j + 1]], axis=1)
                col = (kv * group + 2 * j) * HD_A
                o_ref[qb * tq:(qb + 1) * tq, col:col + LANES] = pair


def _attention(pa, q_gain, k_gain, n_seq, t, row_block0, cache=None, rope=None):
    latent = cache is not None
    tk = t + (PAST_LEN if latent else 0)
    in_specs = [
        pl.BlockSpec((t, SEC_A), lambda s: (row_block0 + s, 0)),
        _resident((1, LANES)),
        _resident((1, LANES)),
    ]
    args = [pa, q_gain, k_gain]
    out_specs = [pl.BlockSpec((t, WIDTH_A), lambda s: (s, 0))]
    out_shape = [jax.ShapeDtypeStruct((n_seq * t, WIDTH_A), F32)]
    if latent:
        in_specs += [
            pl.BlockSpec((None, PAST_LEN, KV_A), lambda s: (s, 0, 0)),
            pl.BlockSpec((None, PAST_LEN, KV_A), lambda s: (s, 0, 0)),
            _resident((t, LANES)),
            _resident((t, LANES)),
        ]
        args += [cache[0], cache[1], rope[0], rope[1]]
    else:
        out_specs += [pl.BlockSpec((None, t, KV_A), lambda s: (s, 0, 0))] * 2
        out_shape += [jax.ShapeDtypeStruct((n_seq, t, KV_A), F32)] * 2
    return pl.pallas_call(
        functools.partial(_attn_kernel, t=t, latent=latent),
        grid=(n_seq,),
        in_specs=in_specs,
        out_specs=out_specs,
        out_shape=out_shape,
        scratch_shapes=[
            pltpu.VMEM((tk, KV_A), F32),
            pltpu.VMEM((tk, KV_A), F32),
            pltpu.VMEM((t, WIDTH_A), F32),
        ],
        compiler_params=pltpu.CompilerParams(dimension_semantics=("arbitrary",), vmem_limit_bytes=VMEM_LIMIT),
        name="attn_lat" if latent else "attn_ctx",
    )(*args)


def _rope_tables():
    tpos = np.arange(DEC_SEQ)
    row = (tpos // GRID_W).astype(np.float32)
    col = (tpos % GRID_W).astype(np.float32)
    n_pairs = HD_A // 4
    inv = (ROPE_THETA ** (-np.arange(n_pairs, dtype=np.float32) / n_pairs)).astype(np.float32)
    ang = np.concatenate([row[:, None] * inv, col[:, None] * inv], axis=-1)
    cos = np.repeat(np.cos(ang), 2, axis=-1)
    sin = np.repeat(np.sin(ang), 2, axis=-1)
    sign = np.tile(np.array([-1.0, 1.0], np.float32), HD_A // 2)
    reps = LANES // HD_A
    return (jnp.asarray(np.tile(cos, (1, reps)), F32), jnp.asarray(np.tile(sin * sign, (1, reps)), F32))


def _tri(n, reverse, strict):
    i = lax.broadcasted_iota(jnp.int32, (n, n), 0)
    j = lax.broadcasted_iota(jnp.int32, (n, n), 1)
    if reverse:
        keep = (j > i) if strict else (j >= i)
    else:
        keep = (j < i) if strict else (j <= i)
    return keep


_DONE = object()


def _trace_in_lockstep(step_fn, steps):
    _run_lockstep([step_fn(i) for i in steps])


def _run_lockstep(generators):
    live = list(generators)
    while live:
        live = [g for g in live if next(g, _DONE) is not _DONE]


def _for_each_step_group(step_fn, n_steps, group, first=0):
    count = n_steps - first
    if count <= group:
        _trace_in_lockstep(step_fn, range(first, n_steps))
        return
    assert count % group == 0, (n_steps, first, group)

    def body(j, carry):
        _trace_in_lockstep(step_fn, [first + j * group + k for k in range(group)])
        return carry

    lax.fori_loop(0, count // group, body, 0)


def _chunk_rows(index, c):
    if isinstance(index, int):
        return pl.ds(index * c, c)
    return pl.ds(pl.multiple_of(index * c, c), c)


def _head_masks(shape):
    lane = lax.broadcasted_iota(jnp.int32, shape, 1)
    first = lane < (LANES // 2)
    return first, jnp.logical_not(first)


def _log_sigmoid(z):
    return jnp.minimum(z, 0.0) - jnp.log1p(jnp.exp(-jnp.abs(z)))


def _gla_kernel(*refs, t, has_init, want_state):
    refs = list(refs)
    (q_ref, k_ref, v_ref, g_ref, lr_ref, w2_ref, gb_ref, gain_ref,
     cum_ref, incl_ref, nmask_ref) = refs[:11]
    pos = 11
    if has_init:
        s0_ref = refs[pos]
        pos += 1
    o_ref = refs[pos]
    pos += 1
    if want_state:
        so_ref = refs[pos]
        pos += 1
    lg_scr, o_scr, qh_scr, n_scr, dec_scr, st_scr = refs[pos:pos + 6]

    c = GLA_CHUNK
    n_chunks = t // c
    half = LANES // 2
    jj = lax.broadcasted_iota(jnp.int32, (half, LANES), 0)
    ll = lax.broadcasted_iota(jnp.int32, (half, LANES), 1)
    sel = [jnp.where(ll == jj + h * half, 1.0, 0.0).astype(F32) for h in range(2)]

    lr = lr_ref[...].astype(BF16)
    for d in range(2):
        z = jnp.dot(lr, w2_ref[d].astype(BF16), preferred_element_type=F32) + gb_ref[d]
        lg_scr[d] = _log_sigmoid(z) * (1.0 / GK_NORMALIZER)
        for h in range(2):
            if has_init:
                st_scr[d, h * DV_B:(h + 1) * DV_B, :] = _dot_tn(s0_ref[d, h], sel[h], precision=HI)
            else:
                st_scr[d, h * DV_B:(h + 1) * DV_B, :] = jnp.zeros((DV_B, LANES), F32)

    m0, m1 = _head_masks((c, LANES))
    scale = DK_B ** -0.5

    def chunk_terms(i):
        rows_f = _chunk_rows(i, c)
        rows_b = _chunk_rows(n_chunks - 1 - i, c)
        q = jnp.concatenate([q_ref[rows_f, :], q_ref[rows_b, :]], axis=0)
        k = jnp.concatenate([k_ref[rows_f, :], k_ref[rows_b, :]], axis=0)
        g = jnp.concatenate([lg_scr[0, rows_f, :], lg_scr[1, rows_b, :]], axis=0)
        b = _mm_exact_lhs(cum_ref[...], g)
        yield
        bm = jnp.concatenate([jnp.broadcast_to(b[c // 2:c // 2 + 1], (c, LANES)),
                              jnp.broadcast_to(b[c + c // 2:c + c // 2 + 1], (c, LANES))], axis=0)
        bl = jnp.concatenate([jnp.broadcast_to(b[c - 1:c], (c, LANES)),
                              jnp.broadcast_to(b[c:c + 1], (c, LANES))], axis=0)
        e1 = jnp.exp(b - bm)
        e2 = jnp.exp(bm - b)
        qt = q * scale * e1
        kt = k * e2
        qh_scr[i] = qt * jnp.exp(bm)
        kc = kt * jnp.exp(bl - bm)
        qf, qb = qt[0:c], qt[c:2 * c]
        lhs = jnp.concatenate([jnp.where(m0, qf, 0.0), jnp.where(m1, qf, 0.0),
                               jnp.where(m0, qb, 0.0), jnp.where(m1, qb, 0.0)], axis=0)
        rhs = jnp.concatenate([kt[0:c], kt[0:c], kt[c:2 * c], kt[c:2 * c]], axis=0)
        att = _mm(lhs, rhs, "nt", GLA_P_ATT) * incl_ref[...]
        yield
        vf = v_ref[rows_f, :]
        vb = v_ref[rows_b, :]
        vst = jnp.concatenate([vf[:, 0:DV_B], vf[:, DV_B:2 * DV_B], vb[:, 0:DV_B], vb[:, DV_B:2 * DV_B]], axis=0)
        o_scr[i] = _mm(att, vst, "nn", 1)
        yield
        n_scr[i, 0] = _mm(vf, kc[0:c], "tn", 1) * nmask_ref[...]
        yield
        n_scr[i, 1] = _mm(vb, kc[c:2 * c], "tn", 1) * nmask_ref[...]
        dec = jnp.exp(bl)
        dec_scr[i, 0] = dec[0:8]
        dec_scr[i, 1] = dec[c:c + 8]

    _for_each_step_group(chunk_terms, n_chunks, GLA_UNROLL)

    def state_scan(i, carry):
        for d in range(2):
            st = st_scr[d]
            st_scr[d] = st * dec_scr[i, d, 0:1, :] + n_scr[i, d]
            n_scr[i, d] = st
        return carry

    lax.fori_loop(0, n_chunks, state_scan, 0)

    def inter_terms(i):
        for d in range(2):
            oi = _mm(qh_scr[i, d * c:(d + 1) * c, :], n_scr[i, d], "nt", 1)
            yield
            for h in range(2):
                r0 = (2 * d + h) * c
                o_scr[i, r0:r0 + c, :] = o_scr[i, r0:r0 + c, :] + oi[:, h * DV_B:(h + 1) * DV_B]

    _for_each_step_group(inter_terms, n_chunks, GLA_UNROLL, first=0 if has_init else 1)

    for h in range(2):
        o = jnp.concatenate([o_scr[j, h * c:(h + 1) * c, :] + o_scr[n_chunks - 1 - j, (2 + h) * c:(3 + h) * c, :]
                             for j in range(n_chunks)], axis=0)
        ms = jnp.mean(o * o, axis=-1, keepdims=True)
        o = o * lax.rsqrt(ms + NORM_EPS) * gain_ref[...]
        gate = g_ref[:, h * DV_B:(h + 1) * DV_B]
        o_ref[:, h * DV_B:(h + 1) * DV_B] = o * (gate * _sigmoid(gate))
        if want_state:
            for d in range(2):
                so_ref[d, h] = _dot_nt(sel[h], st_scr[d, h * DV_B:(h + 1) * DV_B, :], precision=HI)


def _gla(pb, w2pad, gk_b, gain, n_seq, t, row_block0, s0=None, want_state=False):
    has_init = s0 is not None
    pair_w = 2 * DV_B
    state_spec = pl.BlockSpec((None, 2, 2, DK_B, DV_B), lambda s, p: (s, 0, p, 0, 0))
    in_specs = [
        pl.BlockSpec((t, LANES), lambda s, p: (row_block0 + s, p)),
        pl.BlockSpec((t, LANES), lambda s, p: (row_block0 + s, WIDTH_BK // LANES + p)),
        pl.BlockSpec((t, pair_w), lambda s, p: (row_block0 + s, (2 * WIDTH_BK) // pair_w + p)),
        pl.BlockSpec((t, pair_w), lambda s, p: (row_block0 + s, (2 * WIDTH_BK + WIDTH_BV) // pair_w + p)),
        pl.BlockSpec((t, LANES), lambda s, p: (row_block0 + s, SEC_B_MAIN // LANES)),
        pl.BlockSpec((2, LANES, LANES), lambda s, p: (0, 0, p)),
        pl.BlockSpec((2, 1, LANES), lambda s, p: (0, 0, p)),
        pl.BlockSpec((1, DV_B), lambda s, p: (0, 0)),
    ]
    args = [pb, pb, pb, pb, pb, w2pad, gk_b, gain]
    cum, _, incl, _, _ = _rwkv_chunk_constants()
    ri = np.arange(pair_w)[:, None] // DV_B
    li = np.arange(LANES)[None, :] // DK_B
    consts = (cum, incl, jnp.asarray((ri == li).astype(np.float32)))
    in_specs += [pl.BlockSpec(m.shape, lambda s, p: (0, 0)) for m in consts]
    args += list(consts)
    if has_init:
        in_specs.append(state_spec)
        args.append(s0)
    n_steps = t // GLA_CHUNK
    out_specs = [pl.BlockSpec((t, pair_w), lambda s, p: (s, p))]
    out_shape = [jax.ShapeDtypeStruct((n_seq * t, WIDTH_BV), F32)]
    if want_state:
        out_specs.append(state_spec)
        out_shape.append(jax.ShapeDtypeStruct((n_seq, 2, N_HEADS_B, DK_B, DV_B), F32))
    return pl.pallas_call(
        functools.partial(_gla_kernel, t=t, has_init=has_init, want_state=want_state),
        grid=(n_seq, N_HEADS_B // 2),
        in_specs=in_specs,
        out_specs=out_specs,
        out_shape=out_shape,
        scratch_shapes=[
            pltpu.VMEM((2, t, LANES), F32),
            pltpu.VMEM((n_steps, 4 * GLA_CHUNK, DV_B), F32),
            pltpu.VMEM((n_steps, 2 * GLA_CHUNK, LANES), F32),
            pltpu.VMEM((n_steps, 2, pair_w, LANES), F32),
            pltpu.VMEM((n_steps, 2, 8, LANES), F32),
            pltpu.VMEM((2, pair_w, LANES), F32),
        ],
        compiler_params=pltpu.CompilerParams(dimension_semantics=("arbitrary", "arbitrary")),
        name="gla_lat" if has_init else "gla_ctx",
    )(*args)


def _shift_mix(u, mu):
    t = u.shape[0]
    row = lax.broadcasted_iota(jnp.int32, u.shape, 0)
    prev = jnp.where(row == 0, 0.0, pltpu.roll(u, 1, 0))
    nxt = jnp.where(row == t - 1, 0.0, pltpu.roll(u, t - 1, 0))
    return u + mu * (0.5 * (prev + nxt) - u)


_DIMS = {"nn": (((1,), (0,)), ((), ())), "nt": (((1,), (1,)), ((), ())), "tn": (((0,), (0,)), ((), ()))}


def _split_bf16(x):
    hi = x.astype(BF16)
    lo = (x - hi.astype(F32)).astype(BF16)
    return hi, lo


def _mm(a, b, form, passes):
    dims = _DIMS[form]
    if passes == 6:
        return lax.dot_general(a, b, dims, preferred_element_type=F32, precision=HI)
    if passes == 1:
        return lax.dot_general(a.astype(BF16), b.astype(BF16), dims, preferred_element_type=F32)
    a_hi, a_lo = _split_bf16(a)
    b_hi, b_lo = _split_bf16(b)
    ka = dims[0][0][0]
    kb = dims[0][1][0]
    a_cat = jnp.concatenate([a_hi, a_lo, a_hi], axis=ka)
    b_cat = jnp.concatenate([b_hi, b_hi, b_lo], axis=kb)
    return lax.dot_general(a_cat, b_cat, dims, preferred_element_type=F32)


def _mm_exact_lhs(a, x):
    a16 = a.astype(BF16)
    x_hi, x_lo = _split_bf16(x)
    return jnp.dot(jnp.concatenate([a16, a16], axis=1), jnp.concatenate([x_hi, x_lo], axis=0),
                   preferred_element_type=F32)


GLA_P_ATT = 1
GLA_UNROLL = 4

RW_UNROLL = 4


def _half_neumann(n_mat, eye, c):
    sw = n_mat.shape[0]
    p = eye + n_mat
    m = _mm(n_mat, n_mat, "nn", 1)
    yield
    for _ in range(int(math.log2(c)) - 3):
        both = _mm(jnp.concatenate([m, p], axis=0), m, "nn", 1)
        yield
        m = both[0:sw]
        p = p + both[sw:2 * sw]
    return p + _mm(p, m, "nn", 1)


def _rwkv_kernel(*refs, t, has_init, want_state):
    refs = list(refs)
    (r_ref, k_ref, v_ref, wl_ref, al_ref, gl_ref,
     mur_ref, muk_ref, muv_ref, muw_ref, mua_ref, mug_ref,
     w0_ref, w2_ref, a0_ref, a2_ref, g2_ref, kk_ref, ka_ref, rk_ref, lw_ref, lb_ref,
     cum_ref, strict_ref, incl_ref, blk_ref, eye_ref) = refs[:27]
    pos = 27
    if has_init:
        s0_ref = refs[pos]
        pos += 1
    o_ref = refs[pos]
    pos += 1
    if want_state:
        so_ref = refs[pos]
        pos += 1
    r_scr, v_scr, a_scr, dir_scr, y_scr, rp_scr, m_scr, n_scr, dec_scr, st_scr = refs[pos:pos + 10]

    c = RW_CHUNK
    n_chunks = t // c
    half = LANES // 2

    r = _shift_mix(r_ref[...], mur_ref[...])
    kc = _shift_mix(k_ref[...], muk_ref[...])
    vc = _shift_mix(v_ref[...], muv_ref[...])
    wl_in = jnp.tanh(_shift_mix(wl_ref[...], muw_ref[...]))
    al_in = _shift_mix(al_ref[...], mua_ref[...])
    g_in = _sigmoid(_shift_mix(gl_ref[...], mug_ref[...]))

    kkf = kc * kk_ref[...]
    kk = kkf * lax.rsqrt(_segsum(kkf * kkf, HD_C) + 1e-12)
    r_scr[...] = r
    v_scr[...] = vc
    a_scr[...] = -kk
    bonus = jnp.zeros_like(r)
    for d in range(2):
        wl = w0_ref[d] + _bdot(wl_in, w2_ref[d])
        dir_scr[d, 0] = -math.exp(-0.5) * _sigmoid(wl)
        a = _sigmoid(a0_ref[d] + _bdot(al_in, a2_ref[d]))
        kd = kc * (1.0 + (a - 1.0) * ka_ref[...])
        dir_scr[d, 1] = kd
        dir_scr[d, 2] = kk * a
        bonus = bonus + _segsum(r * kd * rk_ref[...], HD_C)
    bonus = bonus * vc
    g_c = _bdot(g_in, g2_ref[...])

    nblk = 2 * (LANES // HD_C)
    sw = nblk * HD_C
    if has_init:
        z = jnp.zeros((HD_C, HD_C), F32)
        blocks = [s0_ref[0, 0], s0_ref[0, 1], s0_ref[1, 0], s0_ref[1, 1]]
        st_scr[...] = jnp.concatenate(
            [jnp.concatenate([blocks[i] if i == j else z for j in range(nblk)], axis=1) for i in range(nblk)], axis=0)
    else:
        st_scr[...] = jnp.zeros((sw, sw), F32)

    m0, m1 = _head_masks((c, LANES))

    def per_head(x):
        xf, xb = x[0:c], x[c:2 * c]
        return jnp.concatenate([jnp.where(m0, xf, 0.0), jnp.where(m1, xf, 0.0),
                                jnp.where(m0, xb, 0.0), jnp.where(m1, xb, 0.0)], axis=0)

    def per_block(x):
        xf, xb = x[0:c], x[c:2 * c]
        return jnp.concatenate([xf, xf, xb, xb], axis=0)

    def side_by_side(x):
        return jnp.concatenate([x[0:c], x[c:2 * c]], axis=1)

    def heads_select(x):
        return jnp.concatenate([jnp.where(m0, x[0:c], x[c:2 * c]), jnp.where(m0, x[2 * c:3 * c], x[3 * c:4 * c])],
                               axis=1)

    def heads_add(x):
        return jnp.concatenate([x[0:c] + x[c:2 * c], x[2 * c:3 * c] + x[3 * c:4 * c]], axis=1)

    def chunk_terms(i):
        rows_f = _chunk_rows(i, c)
        rows_b = _chunk_rows(n_chunks - 1 - i, c)

        def both(ref_f, ref_b):
            return jnp.concatenate([ref_f[rows_f, :], ref_b[rows_b, :]], axis=0)

        rr = both(r_scr, r_scr)
        vv = both(v_scr, v_scr)
        aa = both(a_scr, a_scr)
        lw = both(dir_scr.at[0, 0], dir_scr.at[1, 0])
        kd = both(dir_scr.at[0, 1], dir_scr.at[1, 1])
        bb = both(dir_scr.at[0, 2], dir_scr.at[1, 2])
        cs = _mm_exact_lhs(cum_ref[...], lw)
        yield
        mid = lambda x: jnp.concatenate([jnp.broadcast_to(x[c // 2:c // 2 + 1], (c, LANES)),
                                         jnp.broadcast_to(x[c + c // 2:c + c // 2 + 1], (c, LANES))], axis=0)
        last = lambda x: jnp.concatenate([jnp.broadcast_to(x[c - 1:c], (c, LANES)),
                                          jnp.broadcast_to(x[c:c + 1], (c, LANES))], axis=0)
        cm = mid(cs)
        cl = last(cs)
        e1 = jnp.exp(cs - cm)
        e2 = jnp.exp(cm - cs)
        ecm = jnp.exp(cm)
        ecl = jnp.exp(cl - cm)
        rt = rr * e1
        at = aa * e1 * jnp.exp(-lw)
        bt = bb * e2
        kt = kd * e2
        bc = bt * ecl
        kcl = kt * ecl
        lhs = jnp.concatenate([per_head(at), per_head(rt)], axis=0)
        gram = _mm(lhs, jnp.concatenate([per_block(bt), per_block(kt)], axis=0), "nt", 1)
        yield
        a_ab = gram[0:sw, 0:sw] * strict_ref[...]
        a_rb = gram[sw:2 * sw, 0:sw] * incl_ref[...]
        a_ak = gram[0:sw, sw:2 * sw] * strict_ref[...]
        a_rk = gram[sw:2 * sw, sw:2 * sw] * incl_ref[...]
        p_half = yield from _half_neumann(a_ab, eye_ref[...], c)
        yield
        vst = per_block(vv)
        av = _mm(a_ak, vst, "nn", 1)
        yield
        rhs = jnp.concatenate([av, per_head(at * ecm)], axis=1)
        wa = _mm(p_half, rhs, "nn", 1)
        yield
        res = rhs - wa + _mm(a_ab, wa, "nn", 3)
        yield
        wa = wa + _mm(p_half, res, "nn", 1)
        yield
        ya = _mm(jnp.concatenate([a_rb, a_rk], axis=1),
                 jnp.concatenate([wa, jnp.concatenate([vst, jnp.zeros_like(vst)], axis=1)], axis=0), "nn", 1)
        yield
        rp = per_head(rt * ecm) + ya[:, LANES:2 * LANES]
        w1 = heads_select(wa[:, 0:LANES])
        ap = heads_add(wa[:, LANES:2 * LANES])
        y_scr[i] = heads_select(ya[:, 0:LANES])
        rp_scr[i] = heads_add(rp)
        vv2 = side_by_side(vv)
        lhs_t = jnp.concatenate([jnp.concatenate([ap, w1], axis=1),
                                 jnp.concatenate([jnp.zeros_like(vv2), vv2], axis=1)], axis=0)
        mn = _mm(lhs_t, jnp.concatenate([side_by_side(bc), side_by_side(kcl)], axis=0), "tn", 1)
        m_scr[i] = mn[0:sw] * blk_ref[...]
        n_scr[i] = mn[sw:2 * sw] * blk_ref[...]
        dec_scr[i] = jnp.broadcast_to(side_by_side(jnp.exp(cl))[0:1], (8, 2 * LANES))

    def scan_steps(steps):
        for i in steps:
            st = st_scr[...]
            y_scr[i] = y_scr[i] + _mm(rp_scr[i], st, "nt", 1)
            yield
            st_scr[...] = st * dec_scr[i, 0:1, :] + _mm(st, m_scr[i], "nn", 1) + n_scr[i]
            yield

    group = RW_UNROLL
    if n_chunks <= group:
        _run_lockstep([chunk_terms(i) for i in range(n_chunks)])
        if has_init:
            _run_lockstep([scan_steps(range(n_chunks))])
        else:
            st_scr[...] = n_scr[0]
            _run_lockstep([scan_steps(range(1, n_chunks))])
    else:
        n_groups = n_chunks // group
        _run_lockstep([chunk_terms(k) for k in range(group)])

        def terms_and_scan(j, carry):
            _run_lockstep([chunk_terms(j * group + k) for k in range(group)]
                          + [scan_steps([(j - 1) * group + k for k in range(group)])])
            return carry

        lax.fori_loop(1, n_groups, terms_and_scan, 0)
        _run_lockstep([scan_steps(range((n_groups - 1) * group, n_chunks))])

    y = jnp.concatenate([y_scr[j, :, 0:LANES] + y_scr[n_chunks - 1 - j, :, LANES:2 * LANES]
                         for j in range(n_chunks)], axis=0)
    mean = _segsum(y, HD_C) * (1.0 / HD_C)
    yc = y - mean
    var = _segsum(yc * yc, HD_C) * (1.0 / HD_C)
    gn = yc * lax.rsqrt(var + GN_EPS) * lw_ref[...] + lb_ref[...]
    o_ref[...] = (gn + bonus) * g_c
    if want_state:
        for d in range(2):
            for h in range(2):
                b0 = (2 * d + h) * HD_C
                so_ref[d, h] = st_scr[b0:b0 + HD_C, b0:b0 + HD_C]


def _rwkv_chunk_constants():
    c = RW_CHUNK
    sw = 2 * LANES
    i = np.arange(sw)[:, None]
    j = np.arange(sw)[None, :]
    same = (i // c) == (j // c)
    bwd = i >= LANES
    strict = same & np.where(bwd, j > i, j < i)
    incl = same & np.where(bwd, j >= i, j <= i)
    blk = (i // HD_C) == (j // HD_C)
    ci = np.arange(2 * c)[:, None]
    cj = np.arange(2 * c)[None, :]
    cum = ((ci // c) == (cj // c)) & np.where(ci >= c, cj >= ci, cj <= ci)
    f = lambda m: jnp.asarray(m.astype(np.float32))
    return f(cum), f(strict), f(incl), f(blk), f(np.eye(sw))


def _rwkv(pc, mu_c, w0, w2pad, a0, a2pad, g2, k_k, k_a, r_k, lnx_w, lnx_b, n_seq, t, row_block0,
          s0=None, want_state=False):
    has_init = s0 is not None
    n_pairs = WIDTH_C // LANES
    blk_w = 3 * n_pairs
    blk_a = blk_w + 1
    blk_g = blk_w + 2
    state_spec = pl.BlockSpec((None, 2, 2, HD_C, HD_C), lambda s, p: (s, 0, p, 0, 0))

    def sec(col_fn):
        return pl.BlockSpec((t, LANES), lambda s, p: (row_block0 + s, col_fn(p)))

    def vec(col_fn):
        return pl.BlockSpec((1, LANES), lambda s, p: (0, col_fn(p)))

    col_fns = [lambda p: p, lambda p: n_pairs + p, lambda p: 2 * n_pairs + p,
               lambda p: blk_w, lambda p: blk_a, lambda p: blk_g]
    in_specs = [sec(f) for f in col_fns] + [vec(f) for f in col_fns]
    in_specs += [
        pl.BlockSpec((2, 1, LANES), lambda s, p: (0, 0, p)),
        pl.BlockSpec((2, LANES, LANES), lambda s, p: (0, 0, p)),
        pl.BlockSpec((2, 1, LANES), lambda s, p: (0, 0, p)),
        pl.BlockSpec((2, LANES, LANES), lambda s, p: (0, 0, p)),
        pl.BlockSpec((LANES, LANES), lambda s, p: (0, p)),
    ] + [vec(lambda p: p)] * 5
    args = [pc] * 6 + [mu_c] * 6 + [w0, w2pad, a0, a2pad, g2, k_k, k_a, r_k, lnx_w, lnx_b]
    consts = _rwkv_chunk_constants()
    in_specs += [pl.BlockSpec(m.shape, lambda s, p: (0, 0)) for m in consts]
    args += list(consts)
    if has_init:
        in_specs.append(state_spec)
        args.append(s0)
    out_specs = [pl.BlockSpec((t, LANES), lambda s, p: (s, p))]
    out_shape = [jax.ShapeDtypeStruct((n_seq * t, WIDTH_C), F32)]
    if want_state:
        out_specs.append(state_spec)
        out_shape.append(jax.ShapeDtypeStruct((n_seq, 2, N_HEADS_C, HD_C, HD_C), F32))
    n_steps = t // RW_CHUNK
    sw = 2 * LANES
    return pl.pallas_call(
        functools.partial(_rwkv_kernel, t=t, has_init=has_init, want_state=want_state),
        grid=(n_seq, n_pairs),
        in_specs=in_specs,
        out_specs=out_specs,
        out_shape=out_shape,
        scratch_shapes=[
            pltpu.VMEM((t, LANES), F32),
            pltpu.VMEM((t, LANES), F32),
            pltpu.VMEM((t, LANES), F32),
            pltpu.VMEM((2, 3, t, LANES), F32),
            pltpu.VMEM((n_steps, RW_CHUNK, sw), F32),
            pltpu.VMEM((n_steps, RW_CHUNK, sw), F32),
            pltpu.VMEM((n_steps, sw, sw), F32),
            pltpu.VMEM((n_steps, sw, sw), F32),
            pltpu.VMEM((n_steps, 8, sw), F32),
            pltpu.VMEM((sw, sw), F32),
        ],
        compiler_params=pltpu.CompilerParams(dimension_semantics=("arbitrary", "arbitrary")),
        name="rwkv_lat" if has_init else "rwkv_ctx",
    )(*args)


def _merge_kernel(x_ref, oac_ref, oal_ref, obc_ref, obl_ref, occ_ref, ocl_ref, pg_ref, mod_ref,
                  wa_ref, wb_ref, wc_ref, wo_ref, o_ref, *, ctx_tiles):
    is_ctx = pl.program_id(0) < ctx_tiles
    merged = jnp.zeros(x_ref.shape, F32)
    for j, (c_ref, l_ref, w_ref) in enumerate(((oac_ref, oal_ref, wa_ref), (obc_ref, obl_ref, wb_ref),
                                              (occ_ref, ocl_ref, wc_ref))):
        gate = pg_ref[:, j * D_MODEL:(j + 1) * D_MODEL].astype(F32)
        branch = jnp.where(is_ctx, c_ref[...], l_ref[...]).astype(BF16)
        merged = merged + gate * jnp.dot(branch, w_ref[...], preferred_element_type=F32)
    y = jnp.dot(merged.astype(BF16), wo_ref[...], preferred_element_type=F32)
    o_ref[...] = x_ref[...] + mod_ref[2] * y


def _merge(li, x, branches, pg, mod, wpa, wpb, wpc, wout):
    tm = TM
    ctx_tiles = N_TOK_CTX // tm
    row = lambda w: pl.BlockSpec((tm, w), lambda i: (i, 0))
    ctx_row = lambda w: pl.BlockSpec((tm, w), lambda i: (jnp.minimum(i, ctx_tiles - 1), 0))
    lat_row = lambda w: pl.BlockSpec((tm, w), lambda i: (jnp.maximum(i - ctx_tiles, 0), 0))
    branch_specs = []
    for w in (WIDTH_A, WIDTH_BV, WIDTH_C):
        branch_specs += [ctx_row(w), lat_row(w)]
    return pl.pallas_call(
        functools.partial(_merge_kernel, ctx_tiles=ctx_tiles),
        grid=(N_TOK // tm,),
        in_specs=[row(D_MODEL)] + branch_specs + [row(SEC_G), _mod_spec(li, tm),
                  _layer_resident(li, (WIDTH_A, D_MODEL)), _layer_resident(li, (WIDTH_BV, D_MODEL)),
                  _layer_resident(li, (WIDTH_C, D_MODEL)), _layer_resident(li, (D_MODEL, D_MODEL))],
        out_specs=row(D_MODEL),
        out_shape=jax.ShapeDtypeStruct((N_TOK, D_MODEL), F32),
        compiler_params=pltpu.CompilerParams(dimension_semantics=("arbitrary",), vmem_limit_bytes=VMEM_LIMIT),
        name="merge",
    )(x, *[b for pair in branches for b in pair], pg, mod, wpa, wpb, wpc, wout)


def _ffn_kernel(x_ref, g_ref, mod_ref, wg_ref, wu_ref, wo_ref, o_ref, h_scr, acc_scr):
    f = pl.program_id(1)

    @pl.when(f == 0)
    def _():
        h_scr[...] = _modnorm(x_ref[...], g_ref[...], mod_ref[3], mod_ref[4]).astype(BF16)
        acc_scr[...] = jnp.zeros_like(acc_scr)

    h = h_scr[...]
    gate = jnp.dot(h, wg_ref[...], preferred_element_type=F32)
    up = jnp.dot(h, wu_ref[...], preferred_element_type=F32)
    act = (gate * _sigmoid(gate) * up).astype(BF16)
    acc_scr[...] += jnp.dot(act, wo_ref[...], preferred_element_type=F32)

    @pl.when(f == pl.num_programs(1) - 1)
    def _():
        o_ref[...] = x_ref[...] + mod_ref[5] * acc_scr[...]


def _ffn(li, x, gain, mod, w_in, w_out):
    tm = TM_FFN
    nf = D_FF // TF
    return pl.pallas_call(
        _ffn_kernel,
        grid=(N_TOK // tm, nf),
        in_specs=[
            pl.BlockSpec((tm, D_MODEL), lambda i, f: (i, 0)),
            pl.BlockSpec((None, 1, D_MODEL), lambda i, f: (li, 0, 0)),
            _mod_spec(li, tm),
            pl.BlockSpec((None, D_MODEL, TF), lambda i, f: (li, 0, f)),
            pl.BlockSpec((None, D_MODEL, TF), lambda i, f: (li, 0, nf + f)),
            pl.BlockSpec((None, TF, D_MODEL), lambda i, f: (li, f, 0)),
        ],
        out_specs=pl.BlockSpec((tm, D_MODEL), lambda i, f: (i, 0)),
        out_shape=jax.ShapeDtypeStruct((N_TOK, D_MODEL), F32),
        scratch_shapes=[pltpu.VMEM((tm, D_MODEL), BF16), pltpu.VMEM((tm, D_MODEL), F32)],
        compiler_params=pltpu.CompilerParams(dimension_semantics=("arbitrary", "arbitrary"),
                                             vmem_limit_bytes=VMEM_LIMIT),
        name="ffn",
    )(x, gain, mod, w_in, w_in, w_out)


def _pad_rows(w, offset, total):
    pads = [(0, 0)] * w.ndim
    pads[-2] = (offset, total - offset - w.shape[-2])
    return jnp.pad(w, pads)


def kernel(x_prompt, x_sample, cache_k, cache_v, state_gla, state_rwkv, c, c_ctx, w_ada, b_ada, g_mix, g_ffn,
           w_in, q_gain, k_gain, gk_w2, gk_b, gla_gain, mu_c, w0, w2, a0, a2, g2, k_k, k_a, r_k, lnx_w, lnx_b,
           w_po_a, w_po_b, w_po_c, w_out, w_ffn_in, w_ffn_out):
    x = jnp.concatenate([x_prompt.reshape(N_TOK_CTX, D_MODEL), x_sample.reshape(N_TOK_LAT, D_MODEL)], axis=0)
    cond = jnp.concatenate([c_ctx[None, :], c, jnp.zeros((COND_ROWS - N_COND, D_MODEL), F32)], axis=0)
    mod = _modulation(cond, w_ada, b_ada)[:, :N_COND].reshape(DEPTH, N_COND, 6, 1, D_MODEL)

    w_ab = w_in[:, :, :SEC_A + SEC_B].astype(BF16)
    w_cg = w_in[:, :, OFF_C:].astype(BF16)
    gk_w2p = jnp.stack([_pad_rows(gk_w2[:, 0], 0, LANES), _pad_rows(gk_w2[:, 1], GK_RANK, LANES)], axis=1)
    w2p = jnp.stack([_pad_rows(w2[:, 0], 0, LANES), _pad_rows(w2[:, 1], W_LORA, LANES)], axis=1)
    a2p = jnp.stack([_pad_rows(a2[:, 0], 0, LANES), _pad_rows(a2[:, 1], A_LORA, LANES)], axis=1)
    wpa, wpb, wpc, wout = (w.astype(BF16) for w in (w_po_a, w_po_b, w_po_c, w_out))
    wfi = w_ffn_in.astype(BF16)
    wfo = w_ffn_out.astype(BF16)
    rope = _rope_tables()
    lat_blk = N_TOK_CTX // DEC_SEQ
    g_mix3 = g_mix.reshape(DEPTH, 1, D_MODEL)
    g_ffn3 = g_ffn.reshape(DEPTH, 1, D_MODEL)

    ks, vs, sgs, srs = [], [], [], []
    for li in range(DEPTH):
        vec = lambda a: a[li].reshape(1, -1)
        pa, pb, pc, pg = _inproj(li, x, g_mix3, mod, w_ab, w_cg)

        qg = jnp.tile(vec(q_gain), (1, LANES // HD_A))
        kg = jnp.tile(vec(k_gain), (1, LANES // HD_A))
        oa_c, k_l, v_l = _attention(pa, qg, kg, BATCH, SEQ, 0)
        (oa_l,) = _attention(pa, qg, kg, DEC_BATCH, DEC_SEQ, lat_blk,
                             cache=(cache_k[:, li].reshape(DEC_BATCH, PAST_LEN, KV_A),
                                    cache_v[:, li].reshape(DEC_BATCH, PAST_LEN, KV_A)), rope=rope)

        gkb = gk_b[li].reshape(2, 1, WIDTH_BK)
        ob_c, sg_l = _gla(pb, gk_w2p[li], gkb, vec(gla_gain), BATCH, SEQ, 0, want_state=True)
        (ob_l,) = _gla(pb, gk_w2p[li], gkb, vec(gla_gain), DEC_BATCH, DEC_SEQ, lat_blk, s0=state_gla[:, li])

        rw_args = (pc, vec(mu_c), w0[li].reshape(2, 1, WIDTH_C), w2p[li], a0[li].reshape(2, 1, WIDTH_C), a2p[li],
                   g2[li], vec(k_k), vec(k_a), r_k[li].reshape(1, WIDTH_C), vec(lnx_w), vec(lnx_b))
        oc_c, sr_l = _rwkv(*rw_args, BATCH, SEQ, 0, want_state=True)
        (oc_l,) = _rwkv(*rw_args, DEC_BATCH, DEC_SEQ, lat_blk, s0=state_rwkv[:, li])

        x = _merge(li, x, ((oa_c, oa_l), (ob_c, ob_l), (oc_c, oc_l)), pg, mod, wpa, wpb, wpc, wout)
        x = _ffn(li, x, g_ffn3, mod, wfi, wfo)

        ks.append(k_l.reshape(BATCH, SEQ, N_KV_A, HD_A))
        vs.append(v_l.reshape(BATCH, SEQ, N_KV_A, HD_A))
        sgs.append(sg_l)
        srs.append(sr_l)

    y_prompt = x[:N_TOK_CTX].reshape(BATCH, SEQ, D_MODEL)
    y_sample = x[N_TOK_CTX:].reshape(DEC_BATCH, DEC_SEQ, D_MODEL)
    return (y_prompt, y_sample, jnp.stack(ks, axis=1), jnp.stack(vs, axis=1),
            jnp.stack(sgs, axis=1), jnp.stack(srs, axis=1))
```

```python
import functools
import math

import numpy as np
import jax
import jax.numpy as jnp
from jax import lax
from jax.experimental import pallas as pl
from jax.experimental.pallas import tpu as pltpu

D_MODEL = 1024
BATCH = 16
SEQ = 256
DEPTH = 4
DEC_BATCH = 2
DEC_SEQ = 1024
PAST_LEN = 256
GRID_W = 64
N_HEADS_A = 8
N_KV_A = 2
HD_A = 64
ROPE_THETA = 10000.0
N_HEADS_B = 4
DK_B = 64
DV_B = 128
GK_RANK = 16
GK_NORMALIZER = 16.0
N_HEADS_C = 8
HD_C = 64
W_LORA = 64
A_LORA = 64
G_LORA = 128
GN_EPS = 64e-5
NORM_EPS = 1e-6
WIDTH_A = N_HEADS_A * HD_A
KV_A = N_KV_A * HD_A
WIDTH_BK = N_HEADS_B * DK_B
WIDTH_BV = N_HEADS_B * DV_B
WIDTH_C = N_HEADS_C * HD_C
N_BRANCH = 3
D_FF = ((8 * D_MODEL + 3 * 256 - 1) // (3 * 256)) * 256
OFF_QA = 0
OFF_KA = OFF_QA + WIDTH_A
OFF_VA = OFF_KA + KV_A
OFF_QB = OFF_VA + KV_A
OFF_KB = OFF_QB + WIDTH_BK
OFF_VB = OFF_KB + WIDTH_BK
OFF_GB = OFF_VB + WIDTH_BV
OFF_GKF = OFF_GB + WIDTH_BV
OFF_GKB = OFF_GKF + GK_RANK
OFF_C = OFF_GKB + GK_RANK
C_COLS = 3 * WIDTH_C + 2 * W_LORA + 2 * A_LORA + G_LORA
OFF_GATE = OFF_C + C_COLS
N_IN = OFF_GATE + N_BRANCH * D_MODEL

F32 = jnp.float32
BF16 = jnp.bfloat16

LANES = 128
VMEM_LIMIT = 56 * 1024 * 1024

N_TOK_CTX = BATCH * SEQ
N_TOK_LAT = DEC_BATCH * DEC_SEQ
N_TOK = N_TOK_CTX + N_TOK_LAT
N_COND = 1 + DEC_BATCH
COND_ROWS = 8
SEC_A = OFF_QB
SEC_B_MAIN = OFF_GKF - OFF_QB
SEC_B = SEC_B_MAIN + LANES
SEC_C = C_COLS
SEC_G = N_BRANCH * D_MODEL
TM = 512
TM_FFN = 1024
TM_INPROJ = 256
TF = D_FF // 2
GLA_CHUNK = 64
RW_CHUNK = 64
HI = lax.Precision.HIGHEST


def _tile_group(i, tm):
    ctx_tiles = N_TOK_CTX // tm
    lat_tiles = DEC_SEQ // tm
    return jnp.where(i < ctx_tiles, 0, 1 + (i - ctx_tiles) // lat_tiles)


def _bdot(a, b):
    return jnp.dot(a.astype(BF16), b.astype(BF16), preferred_element_type=F32)


def _dot_nt(a, b, precision=None):
    return lax.dot_general(a, b, (((1,), (1,)), ((), ())), preferred_element_type=F32, precision=precision)


def _dot_tn(a, b, precision=None):
    return lax.dot_general(a, b, (((0,), (0,)), ((), ())), preferred_element_type=F32, precision=precision)


def _sigmoid(x):
    return jax.nn.sigmoid(x)


def _modnorm(x, gain, shift, scale):
    ms = jnp.mean(x * x, axis=-1, keepdims=True)
    y = x * lax.rsqrt(ms + NORM_EPS) * gain
    return y * (1.0 + scale) + shift


def _mod_kernel(c_ref, w_ref, b_ref, o_ref):
    c = c_ref[...]
    s = c * _sigmoid(c)
    o_ref[...] = _bdot(s, w_ref[...]) + b_ref[...]


def _modulation(cond, w_ada, b_ada):
    tn = 1536
    n_out = 6 * D_MODEL
    return pl.pallas_call(
        _mod_kernel,
        grid=(DEPTH, n_out // tn),
        in_specs=[
            pl.BlockSpec((COND_ROWS, D_MODEL), lambda l, j: (0, 0)),
            pl.BlockSpec((None, D_MODEL, tn), lambda l, j: (l, 0, j)),
            pl.BlockSpec((None, 1, tn), lambda l, j: (l, 0, j)),
        ],
        out_specs=pl.BlockSpec((None, COND_ROWS, tn), lambda l, j: (l, 0, j)),
        out_shape=jax.ShapeDtypeStruct((DEPTH, COND_ROWS, n_out), F32),
        compiler_params=pltpu.CompilerParams(dimension_semantics=("arbitrary", "arbitrary")),
        name="modulation",
    )(cond, w_ada, b_ada.reshape(DEPTH, 1, n_out))


def _inproj_kernel(x_ref, g_ref, mod_ref, wab_ref, wcg_ref, fi_ref, fo_ref,
                   oa_ref, ob_ref, oc_ref, og_ref, fi16_ref, fo16_ref):
    fi16_ref[...] = fi_ref[...].astype(BF16)
    fo16_ref[...] = fo_ref[...].astype(BF16)
    h = _modnorm(x_ref[...], g_ref[...], mod_ref[0], mod_ref[1]).astype(BF16)
    oa_ref[...] = jnp.dot(h, wab_ref[:, 0:SEC_A], preferred_element_type=F32)
    ob_ref[...] = jnp.dot(h, wab_ref[:, SEC_A:SEC_A + SEC_B], preferred_element_type=F32)
    oc_ref[...] = jnp.dot(h, wcg_ref[:, 0:SEC_C], preferred_element_type=F32)
    og_ref[...] = _sigmoid(jnp.dot(h, wcg_ref[:, SEC_C:SEC_C + SEC_G], preferred_element_type=F32)).astype(BF16)


def _resident(shape):
    nd = len(shape)
    return pl.BlockSpec(shape, lambda *_: (0,) * nd, pipeline_mode=pl.Buffered(1))


def _layer_resident(li, shape):
    nd = len(shape)
    return pl.BlockSpec((None,) + tuple(shape), lambda *_: (li,) + (0,) * nd, pipeline_mode=pl.Buffered(1))


def _mod_spec(li, tm):
    return pl.BlockSpec((None, None, 6, 1, D_MODEL), lambda i, *_: (li, _tile_group(i, tm), 0, 0, 0))


def _inproj(li, x, gain, mod, w_ab, w_cg, w_ffn_in, w_ffn_out):
    tm = TM_INPROJ
    n_steps = N_TOK // tm
    widths = (SEC_A, SEC_B, SEC_C, SEC_G)
    dtypes = (F32, F32, F32, BF16)
    slabs = D_FF // LANES
    in_cols = (2 * D_FF) // slabs
    assert slabs <= n_steps and in_cols % LANES == 0
    slab = lambda i: jnp.minimum(i, slabs - 1)
    return pl.pallas_call(
        _inproj_kernel,
        grid=(n_steps,),
        in_specs=[
            pl.BlockSpec((tm, D_MODEL), lambda i: (i, 0)),
            _layer_resident(li, (1, D_MODEL)),
            _mod_spec(li, tm),
            _layer_resident(li, (D_MODEL, SEC_A + SEC_B)),
            _layer_resident(li, (D_MODEL, SEC_C + SEC_G)),
            pl.BlockSpec((None, D_MODEL, in_cols), lambda i: (li, 0, slab(i))),
            pl.BlockSpec((None, LANES, D_MODEL), lambda i: (li, slab(i), 0)),
        ],
        out_specs=[pl.BlockSpec((tm, w), lambda i: (i, 0)) for w in widths] + [
            pl.BlockSpec((D_MODEL, in_cols), lambda i: (0, slab(i))),
            pl.BlockSpec((LANES, D_MODEL), lambda i: (slab(i), 0)),
        ],
        out_shape=[jax.ShapeDtypeStruct((N_TOK, w), dt) for w, dt in zip(widths, dtypes)] + [
            jax.ShapeDtypeStruct((D_MODEL, 2 * D_FF), BF16),
            jax.ShapeDtypeStruct((D_FF, D_MODEL), BF16),
        ],
        compiler_params=pltpu.CompilerParams(dimension_semantics=("arbitrary",), vmem_limit_bytes=VMEM_LIMIT),
        name="inproj",
    )(x, gain, mod, w_ab, w_cg, w_ffn_in, w_ffn_out)


def _segsum(x, width):
    n = x.shape[-1]
    gi = lax.broadcasted_iota(jnp.int32, (n, n), 0) // width
    gj = lax.broadcasted_iota(jnp.int32, (n, n), 1) // width
    e = jnp.where(gi == gj, 1.0, 0.0).astype(BF16)
    hi = x.astype(BF16)
    lo = (x - hi.astype(F32)).astype(BF16)
    return jnp.dot(jnp.concatenate([hi, lo], axis=1), jnp.concatenate([e, e], axis=0),
                   preferred_element_type=F32)


def _swap_pairs(x):
    lane = lax.broadcasted_iota(jnp.int32, x.shape, 1)
    nxt = pltpu.roll(x, LANES - 1, 1)
    prv = pltpu.roll(x, 1, 1)
    return jnp.where(lane % 2 == 0, nxt, prv)


def _head_norm(x, gain):
    ms = _segsum(x * x, HD_A) * (1.0 / HD_A)
    return x * lax.rsqrt(ms + NORM_EPS) * gain


def _attn_kernel(*refs, t, latent):
    if latent:
        (pa_ref, qg_ref, kg_ref, ck_ref, cv_ref, cos_ref, sin_ref, o_ref, k_scr, v_scr, q_scr) = refs
    else:
        (pa_ref, qg_ref, kg_ref, o_ref, ko_ref, vo_ref, k_scr, v_scr, q_scr) = refs
    tk = k_scr.shape[0]
    past = tk - t
    kn = _head_norm(pa_ref[:, OFF_KA:OFF_VA], kg_ref[...])
    va = pa_ref[:, OFF_VA:OFF_QB]
    if latent:
        cos = cos_ref[...]
        sin = sin_ref[...]
        kn = kn * cos + _swap_pairs(kn) * sin
        k_scr[0:past, :] = ck_ref[...]
        v_scr[0:past, :] = cv_ref[...]
    else:
        ko_ref[...] = kn
        vo_ref[...] = va
    k_scr[past:tk, :] = kn
    v_scr[past:tk, :] = va
    for c in range(WIDTH_A // LANES):
        qn = _head_norm(pa_ref[:, c * LANES:(c + 1) * LANES], qg_ref[...])
        if latent:
            qn = qn * cos + _swap_pairs(qn) * sin
        q_scr[:, c * LANES:(c + 1) * LANES] = qn * (HD_A ** -0.5)

    group = N_HEADS_A // N_KV_A
    tq = min(t, 256)
    for kv in range(N_KV_A):
        kh = k_scr[:, kv * HD_A:(kv + 1) * HD_A].astype(BF16)
        vh = v_scr[:, kv * HD_A:(kv + 1) * HD_A].astype(BF16)
        for qb in range(t // tq):
            outs = [None] * group

            def one_head(g, kv=kv, qb=qb, kh=kh, vh=vh, outs=outs):
                h = kv * group + g
                qh = q_scr[qb * tq:(qb + 1) * tq, h * HD_A:(h + 1) * HD_A].astype(BF16)
                s = _dot_nt(qh, kh)
                yield
                m = jnp.max(s, axis=-1, keepdims=True)
                e = jnp.exp(s - m)
                pv = jnp.dot(e.astype(BF16), vh, preferred_element_type=F32)
                yield
                outs[g] = pv / jnp.sum(e, axis=-1, keepdims=True)

            _trace_in_lockstep(one_head, range(group))
            for j in range(group // 2):
                pair = jnp.concatenate([outs[2 * j], outs[2 * j + 1]], axis=1)
                col = (kv * group + 2 * j) * HD_A
                o_ref[qb * tq:(qb + 1) * tq, col:col + LANES] = pair


def _attention(pa, q_gain, k_gain, n_seq, t, row_block0, cache=None, rope=None):
    latent = cache is not None
    tk = t + (PAST_LEN if latent else 0)
    in_specs = [
        pl.BlockSpec((t, SEC_A), lambda s: (row_block0 + s, 0)),
        _resident((1, LANES)),
        _resident((1, LANES)),
    ]
    args = [pa, q_gain, k_gain]
    out_specs = [pl.BlockSpec((t, WIDTH_A), lambda s: (s, 0))]
    out_shape = [jax.ShapeDtypeStruct((n_seq * t, WIDTH_A), F32)]
    if latent:
        in_specs += [
            pl.BlockSpec((None, PAST_LEN, KV_A), lambda s: (s, 0, 0)),
            pl.BlockSpec((None, PAST_LEN, KV_A), lambda s: (s, 0, 0)),
            _resident((t, LANES)),
            _resident((t, LANES)),
        ]
        args += [cache[0], cache[1], rope[0], rope[1]]
    else:
        out_specs += [pl.BlockSpec((None, t, KV_A), lambda s: (s, 0, 0))] * 2
        out_shape += [jax.ShapeDtypeStruct((n_seq, t, KV_A), F32)] * 2
    return pl.pallas_call(
        functools.partial(_attn_kernel, t=t, latent=latent),
        grid=(n_seq,),
        in_specs=in_specs,
        out_specs=out_specs,
        out_shape=out_shape,
        scratch_shapes=[
            pltpu.VMEM((tk, KV_A), F32),
            pltpu.VMEM((tk, KV_A), F32),
            pltpu.VMEM((t, WIDTH_A), F32),
        ],
        compiler_params=pltpu.CompilerParams(dimension_semantics=("arbitrary",), vmem_limit_bytes=VMEM_LIMIT),
        name="attn_lat" if latent else "attn_ctx",
    )(*args)


def _rope_tables():
    tpos = np.arange(DEC_SEQ)
    row = (tpos // GRID_W).astype(np.float32)
    col = (tpos % GRID_W).astype(np.float32)
    n_pairs = HD_A // 4
    inv = (ROPE_THETA ** (-np.arange(n_pairs, dtype=np.float32) / n_pairs)).astype(np.float32)
    ang = np.concatenate([row[:, None] * inv, col[:, None] * inv], axis=-1)
    cos = np.repeat(np.cos(ang), 2, axis=-1)
    sin = np.repeat(np.sin(ang), 2, axis=-1)
    sign = np.tile(np.array([-1.0, 1.0], np.float32), HD_A // 2)
    reps = LANES // HD_A
    return (jnp.asarray(np.tile(cos, (1, reps)), F32), jnp.asarray(np.tile(sin * sign, (1, reps)), F32))


def _tri(n, reverse, strict):
    i = lax.broadcasted_iota(jnp.int32, (n, n), 0)
    j = lax.broadcasted_iota(jnp.int32, (n, n), 1)
    if reverse:
        keep = (j > i) if strict else (j >= i)
    else:
        keep = (j < i) if strict else (j <= i)
    return keep


_DONE = object()


def _trace_in_lockstep(step_fn, steps):
    _run_lockstep([step_fn(i) for i in steps])


def _run_lockstep(generators):
    live = list(generators)
    while live:
        live = [g for g in live if next(g, _DONE) is not _DONE]


def _for_each_step_group(step_fn, n_steps, group, first=0):
    count = n_steps - first
    if count <= group:
        _trace_in_lockstep(step_fn, range(first, n_steps))
        return
    assert count % group == 0, (n_steps, first, group)

    def body(j, carry):
        _trace_in_lockstep(step_fn, [first + j * group + k for k in range(group)])
        return carry

    lax.fori_loop(0, count // group, body, 0)


def _chunk_rows(index, c):
    if isinstance(index, int):
        return pl.ds(index * c, c)
    return pl.ds(pl.multiple_of(index * c, c), c)


def _head_masks(shape):
    lane = lax.broadcasted_iota(jnp.int32, shape, 1)
    first = lane < (LANES // 2)
    return first, jnp.logical_not(first)


def _log_sigmoid(z):
    return jnp.minimum(z, 0.0) - jnp.log1p(jnp.exp(-jnp.abs(z)))


def _gla_kernel(*refs, t, has_init, want_state):
    refs = list(refs)
    (q_ref, k_ref, v_ref, g_ref, lr_ref, w2_ref, gb_ref, gain_ref,
     cum_ref, incl_ref, nmask_ref) = refs[:11]
    pos = 11
    if has_init:
        s0_ref = refs[pos]
        pos += 1
    o_ref = refs[pos]
    pos += 1
    if want_state:
        so_ref = refs[pos]
        pos += 1
    lg_scr, o_scr, qh_scr, n_scr, dec_scr, st_scr = refs[pos:pos + 6]

    c = GLA_CHUNK
    n_chunks = t // c
    half = LANES // 2
    jj = lax.broadcasted_iota(jnp.int32, (half, LANES), 0)
    ll = lax.broadcasted_iota(jnp.int32, (half, LANES), 1)
    sel = [jnp.where(ll == jj + h * half, 1.0, 0.0).astype(F32) for h in range(2)]

    lr = lr_ref[...].astype(BF16)
    for d in range(2):
        z = jnp.dot(lr, w2_ref[d].astype(BF16), preferred_element_type=F32) + gb_ref[d]
        lg_scr[d] = _log_sigmoid(z) * (1.0 / GK_NORMALIZER)
        for h in range(2):
            if has_init:
                st_scr[d, h * DV_B:(h + 1) * DV_B, :] = _dot_tn(s0_ref[d, h], sel[h], precision=HI)
            else:
                st_scr[d, h * DV_B:(h + 1) * DV_B, :] = jnp.zeros((DV_B, LANES), F32)

    m0, m1 = _head_masks((c, LANES))
    scale = DK_B ** -0.5

    def chunk_terms(i):
        rows_f = _chunk_rows(i, c)
        rows_b = _chunk_rows(n_chunks - 1 - i, c)
        q = jnp.concatenate([q_ref[rows_f, :], q_ref[rows_b, :]], axis=0)
        k = jnp.concatenate([k_ref[rows_f, :], k_ref[rows_b, :]], axis=0)
        g = jnp.concatenate([lg_scr[0, rows_f, :], lg_scr[1, rows_b, :]], axis=0)
        b = _mm_exact_lhs(cum_ref[...], g)
        yield
        bm = jnp.concatenate([jnp.broadcast_to(b[c // 2:c // 2 + 1], (c, LANES)),
                              jnp.broadcast_to(b[c + c // 2:c + c // 2 + 1], (c, LANES))], axis=0)
        bl = jnp.concatenate([jnp.broadcast_to(b[c - 1:c], (c, LANES)),
                              jnp.broadcast_to(b[c:c + 1], (c, LANES))], axis=0)
        e1 = jnp.exp(b - bm)
        e2 = jnp.exp(bm - b)
        qt = q * scale * e1
        kt = k * e2
        qh_scr[i] = qt * jnp.exp(bm)
        kc = kt * jnp.exp(bl - bm)
        qf, qb = qt[0:c], qt[c:2 * c]
        lhs = jnp.concatenate([jnp.where(m0, qf, 0.0), jnp.where(m1, qf, 0.0),
                               jnp.where(m0, qb, 0.0), jnp.where(m1, qb, 0.0)], axis=0)
        rhs = jnp.concatenate([kt[0:c], kt[0:c], kt[c:2 * c], kt[c:2 * c]], axis=0)
        att = _mm(lhs, rhs, "nt", GLA_P_ATT) * incl_ref[...]
        yield
        vf = v_ref[rows_f, :]
        vb = v_ref[rows_b, :]
        vst = jnp.concatenate([vf[:, 0:DV_B], vf[:, DV_B:2 * DV_B], vb[:, 0:DV_B], vb[:, DV_B:2 * DV_B]], axis=0)
        o_scr[i] = _mm(att, vst, "nn", 1)
        yield
        n_scr[i, 0] = _mm(vf, kc[0:c], "tn", 1) * nmask_ref[...]
        yield
        n_scr[i, 1] = _mm(vb, kc[c:2 * c], "tn", 1) * nmask_ref[...]
        dec = jnp.exp(bl)
        dec_scr[i, 0] = dec[0:8]
        dec_scr[i, 1] = dec[c:c + 8]

    _for_each_step_group(chunk_terms, n_chunks, GLA_UNROLL)

    def state_scan(i, carry):
        for d in range(2):
            st = st_scr[d]
            st_scr[d] = st * dec_scr[i, d, 0:1, :] + n_scr[i, d]
            n_scr[i, d] = st
        return carry

    lax.fori_loop(0, n_chunks, state_scan, 0)

    def inter_terms(i):
        for d in range(2):
            oi = _mm(qh_scr[i, d * c:(d + 1) * c, :], n_scr[i, d], "nt", 1)
            yield
            for h in range(2):
                r0 = (2 * d + h) * c
                o_scr[i, r0:r0 + c, :] = o_scr[i, r0:r0 + c, :] + oi[:, h * DV_B:(h + 1) * DV_B]

    _for_each_step_group(inter_terms, n_chunks, GLA_UNROLL, first=0 if has_init else 1)

    for h in range(2):
        o = jnp.concatenate([o_scr[j, h * c:(h + 1) * c, :] + o_scr[n_chunks - 1 - j, (2 + h) * c:(3 + h) * c, :]
                             for j in range(n_chunks)], axis=0)
        ms = jnp.mean(o * o, axis=-1, keepdims=True)
        o = o * lax.rsqrt(ms + NORM_EPS) * gain_ref[...]
        gate = g_ref[:, h * DV_B:(h + 1) * DV_B]
        o_ref[:, h * DV_B:(h + 1) * DV_B] = o * (gate * _sigmoid(gate))
        if want_state:
            for d in range(2):
                so_ref[d, h] = _dot_nt(sel[h], st_scr[d, h * DV_B:(h + 1) * DV_B, :], precision=HI)


def _gla(pb, w2pad, gk_b, gain, n_seq, t, row_block0, s0=None, want_state=False):
    has_init = s0 is not None
    pair_w = 2 * DV_B
    state_spec = pl.BlockSpec((None, 2, 2, DK_B, DV_B), lambda s, p: (s, 0, p, 0, 0))
    in_specs = [
        pl.BlockSpec((t, LANES), lambda s, p: (row_block0 + s, p)),
        pl.BlockSpec((t, LANES), lambda s, p: (row_block0 + s, WIDTH_BK // LANES + p)),
        pl.BlockSpec((t, pair_w), lambda s, p: (row_block0 + s, (2 * WIDTH_BK) // pair_w + p)),
        pl.BlockSpec((t, pair_w), lambda s, p: (row_block0 + s, (2 * WIDTH_BK + WIDTH_BV) // pair_w + p)),
        pl.BlockSpec((t, LANES), lambda s, p: (row_block0 + s, SEC_B_MAIN // LANES)),
        pl.BlockSpec((2, LANES, LANES), lambda s, p: (0, 0, p)),
        pl.BlockSpec((2, 1, LANES), lambda s, p: (0, 0, p)),
        pl.BlockSpec((1, DV_B), lambda s, p: (0, 0)),
    ]
    args = [pb, pb, pb, pb, pb, w2pad, gk_b, gain]
    cum, _, incl, _, _ = _rwkv_chunk_constants()
    ri = np.arange(pair_w)[:, None] // DV_B
    li = np.arange(LANES)[None, :] // DK_B
    consts = (cum, incl, jnp.asarray((ri == li).astype(np.float32)))
    in_specs += [pl.BlockSpec(m.shape, lambda s, p: (0, 0)) for m in consts]
    args += list(consts)
    if has_init:
        in_specs.append(state_spec)
        args.append(s0)
    n_steps = t // GLA_CHUNK
    out_specs = [pl.BlockSpec((t, pair_w), lambda s, p: (s, p))]
    out_shape = [jax.ShapeDtypeStruct((n_seq * t, WIDTH_BV), F32)]
    if want_state:
        out_specs.append(state_spec)
        out_shape.append(jax.ShapeDtypeStruct((n_seq, 2, N_HEADS_B, DK_B, DV_B), F32))
    return pl.pallas_call(
        functools.partial(_gla_kernel, t=t, has_init=has_init, want_state=want_state),
        grid=(n_seq, N_HEADS_B // 2),
        in_specs=in_specs,
        out_specs=out_specs,
        out_shape=out_shape,
        scratch_shapes=[
            pltpu.VMEM((2, t, LANES), F32),
            pltpu.VMEM((n_steps, 4 * GLA_CHUNK, DV_B), F32),
            pltpu.VMEM((n_steps, 2 * GLA_CHUNK, LANES), F32),
            pltpu.VMEM((n_steps, 2, pair_w, LANES), F32),
            pltpu.VMEM((n_steps, 2, 8, LANES), F32),
            pltpu.VMEM((2, pair_w, LANES), F32),
        ],
        compiler_params=pltpu.CompilerParams(dimension_semantics=("arbitrary", "arbitrary")),
        name="gla_lat" if has_init else "gla_ctx",
    )(*args)


def _shift_mix(u, mu):
    t = u.shape[0]
    row = lax.broadcasted_iota(jnp.int32, u.shape, 0)
    prev = jnp.where(row == 0, 0.0, pltpu.roll(u, 1, 0))
    nxt = jnp.where(row == t - 1, 0.0, pltpu.roll(u, t - 1, 0))
    return u + mu * (0.5 * (prev + nxt) - u)


_DIMS = {"nn": (((1,), (0,)), ((), ())), "nt": (((1,), (1,)), ((), ())), "tn": (((0,), (0,)), ((), ()))}


def _split_bf16(x):
    hi = x.astype(BF16)
    lo = (x - hi.astype(F32)).astype(BF16)
    return hi, lo


def _mm(a, b, form, passes):
    dims = _DIMS[form]
    if passes == 6:
        return lax.dot_general(a, b, dims, preferred_element_type=F32, precision=HI)
    if passes == 1:
        return lax.dot_general(a.astype(BF16), b.astype(BF16), dims, preferred_element_type=F32)
    a_hi, a_lo = _split_bf16(a)
    b_hi, b_lo = _split_bf16(b)
    ka = dims[0][0][0]
    kb = dims[0][1][0]
    a_cat = jnp.concatenate([a_hi, a_lo, a_hi], axis=ka)
    b_cat = jnp.concatenate([b_hi, b_hi, b_lo], axis=kb)
    return lax.dot_general(a_cat, b_cat, dims, preferred_element_type=F32)


def _mm_exact_lhs(a, x):
    a16 = a.astype(BF16)
    x_hi, x_lo = _split_bf16(x)
    return jnp.dot(jnp.concatenate([a16, a16], axis=1), jnp.concatenate([x_hi, x_lo], axis=0),
                   preferred_element_type=F32)


GLA_P_ATT = 1
GLA_UNROLL = 4

RW_UNROLL = 4


def _half_neumann(n_mat, eye, c):
    sw = n_mat.shape[0]
    p = eye + n_mat
    m = _mm(n_mat, n_mat, "nn", 1)
    yield
    for _ in range(int(math.log2(c)) - 3):
        both = _mm(jnp.concatenate([m, p], axis=0), m, "nn", 1)
        yield
        m = both[0:sw]
        p = p + both[sw:2 * sw]
    return p + _mm(p, m, "nn", 1)


def _rwkv_kernel(*refs, t, has_init, want_state):
    refs = list(refs)
    (r_ref, k_ref, v_ref, wl_ref, al_ref, gl_ref,
     mur_ref, muk_ref, muv_ref, muw_ref, mua_ref, mug_ref,
     w0_ref, w2_ref, a0_ref, a2_ref, g2_ref, kk_ref, ka_ref, rk_ref, lw_ref, lb_ref,
     cum_ref, strict_ref, incl_ref, blk_ref, eye_ref) = refs[:27]
    pos = 27
    if has_init:
        s0_ref = refs[pos]
        pos += 1
    o_ref = refs[pos]
    pos += 1
    if want_state:
        so_ref = refs[pos]
        pos += 1
    r_scr, v_scr, a_scr, dir_scr, y_scr, rp_scr, m_scr, n_scr, dec_scr, st_scr = refs[pos:pos + 10]

    c = RW_CHUNK
    n_chunks = t // c
    half = LANES // 2

    r = _shift_mix(r_ref[...], mur_ref[...])
    kc = _shift_mix(k_ref[...], muk_ref[...])
    vc = _shift_mix(v_ref[...], muv_ref[...])
    wl_in = jnp.tanh(_shift_mix(wl_ref[...], muw_ref[...]))
    al_in = _shift_mix(al_ref[...], mua_ref[...])
    g_in = _sigmoid(_shift_mix(gl_ref[...], mug_ref[...]))

    kkf = kc * kk_ref[...]
    kk = kkf * lax.rsqrt(_segsum(kkf * kkf, HD_C) + 1e-12)
    r_scr[...] = r
    v_scr[...] = vc
    a_scr[...] = -kk
    bonus = jnp.zeros_like(r)
    for d in range(2):
        wl = w0_ref[d] + _bdot(wl_in, w2_ref[d])
        dir_scr[d, 0] = -math.exp(-0.5) * _sigmoid(wl)
        a = _sigmoid(a0_ref[d] + _bdot(al_in, a2_ref[d]))
        kd = kc * (1.0 + (a - 1.0) * ka_ref[...])
        dir_scr[d, 1] = kd
        dir_scr[d, 2] = kk * a
        bonus = bonus + _segsum(r * kd * rk_ref[...], HD_C)
    bonus = bonus * vc
    g_c = _bdot(g_in, g2_ref[...])

    nblk = 2 * (LANES // HD_C)
    sw = nblk * HD_C
    if has_init:
        z = jnp.zeros((HD_C, HD_C), F32)
        blocks = [s0_ref[0, 0], s0_ref[0, 1], s0_ref[1, 0], s0_ref[1, 1]]
        st_scr[...] = jnp.concatenate(
            [jnp.concatenate([blocks[i] if i == j else z for j in range(nblk)], axis=1) for i in range(nblk)], axis=0)
    else:
        st_scr[...] = jnp.zeros((sw, sw), F32)

    m0, m1 = _head_masks((c, LANES))

    def per_head(x):
        xf, xb = x[0:c], x[c:2 * c]
        return jnp.concatenate([jnp.where(m0, xf, 0.0), jnp.where(m1, xf, 0.0),
                                jnp.where(m0, xb, 0.0), jnp.where(m1, xb, 0.0)], axis=0)

    def per_block(x):
        xf, xb = x[0:c], x[c:2 * c]
        return jnp.concatenate([xf, xf, xb, xb], axis=0)

    def side_by_side(x):
        return jnp.concatenate([x[0:c], x[c:2 * c]], axis=1)

    def heads_select(x):
        return jnp.concatenate([jnp.where(m0, x[0:c], x[c:2 * c]), jnp.where(m0, x[2 * c:3 * c], x[3 * c:4 * c])],
                               axis=1)

    def heads_add(x):
        return jnp.concatenate([x[0:c] + x[c:2 * c], x[2 * c:3 * c] + x[3 * c:4 * c]], axis=1)

    def chunk_terms(i):
        rows_f = _chunk_rows(i, c)
        rows_b = _chunk_rows(n_chunks - 1 - i, c)

        def both(ref_f, ref_b):
            return jnp.concatenate([ref_f[rows_f, :], ref_b[rows_b, :]], axis=0)

        rr = both(r_scr, r_scr)
        vv = both(v_scr, v_scr)
        aa = both(a_scr, a_scr)
        lw = both(dir_scr.at[0, 0], dir_scr.at[1, 0])
        kd = both(dir_scr.at[0, 1], dir_scr.at[1, 1])
        bb = both(dir_scr.at[0, 2], dir_scr.at[1, 2])
        cs = _mm_exact_lhs(cum_ref[...], lw)
        yield
        mid = lambda x: jnp.concatenate([jnp.broadcast_to(x[c // 2:c // 2 + 1], (c, LANES)),
                                         jnp.broadcast_to(x[c + c // 2:c + c // 2 + 1], (c, LANES))], axis=0)
        last = lambda x: jnp.concatenate([jnp.broadcast_to(x[c - 1:c], (c, LANES)),
                                          jnp.broadcast_to(x[c:c + 1], (c, LANES))], axis=0)
        cm = mid(cs)
        cl = last(cs)
        e1 = jnp.exp(cs - cm)
        e2 = jnp.exp(cm - cs)
        ecm = jnp.exp(cm)
        ecl = jnp.exp(cl - cm)
        rt = rr * e1
        at = aa * e1 * jnp.exp(-lw)
        bt = bb * e2
        kt = kd * e2
        bc = bt * ecl
        kcl = kt * ecl
        lhs = jnp.concatenate([per_head(at), per_head(rt)], axis=0)
        gram = _mm(lhs, jnp.concatenate([per_block(bt), per_block(kt)], axis=0), "nt", 1)
        yield
        a_ab = gram[0:sw, 0:sw] * strict_ref[...]
        a_rb = gram[sw:2 * sw, 0:sw] * incl_ref[...]
        a_ak = gram[0:sw, sw:2 * sw] * strict_ref[...]
        a_rk = gram[sw:2 * sw, sw:2 * sw] * incl_ref[...]
        p_half = yield from _half_neumann(a_ab, eye_ref[...], c)
        yield
        vst = per_block(vv)
        av = _mm(a_ak, vst, "nn", 1)
        yield
        rhs = jnp.concatenate([av, per_head(at * ecm)], axis=1)
        wa = _mm(p_half, rhs, "nn", 1)
        yield
        res = rhs - wa + _mm(a_ab, wa, "nn", 3)
        yield
        wa = wa + _mm(p_half, res, "nn", 1)
        yield
        ya = _mm(jnp.concatenate([a_rb, a_rk], axis=1),
                 jnp.concatenate([wa, jnp.concatenate([vst, jnp.zeros_like(vst)], axis=1)], axis=0), "nn", 1)
        yield
        rp = per_head(rt * ecm) + ya[:, LANES:2 * LANES]
        w1 = heads_select(wa[:, 0:LANES])
        ap = heads_add(wa[:, LANES:2 * LANES])
        y_scr[i] = heads_select(ya[:, 0:LANES])
        rp_scr[i] = heads_add(rp)
        vv2 = side_by_side(vv)
        lhs_t = jnp.concatenate([jnp.concatenate([ap, w1], axis=1),
                                 jnp.concatenate([jnp.zeros_like(vv2), vv2], axis=1)], axis=0)
        mn = _mm(lhs_t, jnp.concatenate([side_by_side(bc), side_by_side(kcl)], axis=0), "tn", 1)
        m_scr[i] = mn[0:sw] * blk_ref[...]
        n_scr[i] = mn[sw:2 * sw] * blk_ref[...]
        dec_scr[i] = jnp.broadcast_to(side_by_side(jnp.exp(cl))[0:1], (8, 2 * LANES))

    def scan_steps(steps):
        for i in steps:
            st = st_scr[...]
            y_scr[i] = y_scr[i] + _mm(rp_scr[i], st, "nt", 1)
            yield
            st_scr[...] = st * dec_scr[i, 0:1, :] + _mm(st, m_scr[i], "nn", 1) + n_scr[i]
            yield

    group = RW_UNROLL
    if n_chunks <= group:
        _run_lockstep([chunk_terms(i) for i in range(n_chunks)])
        if has_init:
            _run_lockstep([scan_steps(range(n_chunks))])
        else:
            st_scr[...] = n_scr[0]
            _run_lockstep([scan_steps(range(1, n_chunks))])
    else:
        n_groups = n_chunks // group
        _run_lockstep([chunk_terms(k) for k in range(group)])

        def terms_and_scan(j, carry):
            _run_lockstep([chunk_terms(j * group + k) for k in range(group)]
                          + [scan_steps([(j - 1) * group + k for k in range(group)])])
            return carry

        lax.fori_loop(1, n_groups, terms_and_scan, 0)
        _run_lockstep([scan_steps(range((n_groups - 1) * group, n_chunks))])

    y = jnp.concatenate([y_scr[j, :, 0:LANES] + y_scr[n_chunks - 1 - j, :, LANES:2 * LANES]
                         for j in range(n_chunks)], axis=0)
    mean = _segsum(y, HD_C) * (1.0 / HD_C)
    yc = y - mean
    var = _segsum(yc * yc, HD_C) * (1.0 / HD_C)
    gn = yc * lax.rsqrt(var + GN_EPS) * lw_ref[...] + lb_ref[...]
    o_ref[...] = (gn + bonus) * g_c
    if want_state:
        for d in range(2):
            for h in range(2):
                b0 = (2 * d + h) * HD_C
                so_ref[d, h] = st_scr[b0:b0 + HD_C, b0:b0 + HD_C]


def _rwkv_chunk_constants():
    c = RW_CHUNK
    sw = 2 * LANES
    i = np.arange(sw)[:, None]
    j = np.arange(sw)[None, :]
    same = (i // c) == (j // c)
    bwd = i >= LANES
    strict = same & np.where(bwd, j > i, j < i)
    incl = same & np.where(bwd, j >= i, j <= i)
    blk = (i // HD_C) == (j // HD_C)
    ci = np.arange(2 * c)[:, None]
    cj = np.arange(2 * c)[None, :]
    cum = ((ci // c) == (cj // c)) & np.where(ci >= c, cj >= ci, cj <= ci)
    f = lambda m: jnp.asarray(m.astype(np.float32))
    return f(cum), f(strict), f(incl), f(blk), f(np.eye(sw))


def _rwkv(pc, mu_c, w0, w2pad, a0, a2pad, g2, k_k, k_a, r_k, lnx_w, lnx_b, n_seq, t, row_block0,
          s0=None, want_state=False):
    has_init = s0 is not None
    n_pairs = WIDTH_C // LANES
    blk_w = 3 * n_pairs
    blk_a = blk_w + 1
    blk_g = blk_w + 2
    state_spec = pl.BlockSpec((None, 2, 2, HD_C, HD_C), lambda s, p: (s, 0, p, 0, 0))

    def sec(col_fn):
        return pl.BlockSpec((t, LANES), lambda s, p: (row_block0 + s, col_fn(p)))

    def vec(col_fn):
        return pl.BlockSpec((1, LANES), lambda s, p: (0, col_fn(p)))

    col_fns = [lambda p: p, lambda p: n_pairs + p, lambda p: 2 * n_pairs + p,
               lambda p: blk_w, lambda p: blk_a, lambda p: blk_g]
    in_specs = [sec(f) for f in col_fns] + [vec(f) for f in col_fns]
    in_specs += [
        pl.BlockSpec((2, 1, LANES), lambda s, p: (0, 0, p)),
        pl.BlockSpec((2, LANES, LANES), lambda s, p: (0, 0, p)),
        pl.BlockSpec((2, 1, LANES), lambda s, p: (0, 0, p)),
        pl.BlockSpec((2, LANES, LANES), lambda s, p: (0, 0, p)),
        pl.BlockSpec((LANES, LANES), lambda s, p: (0, p)),
    ] + [vec(lambda p: p)] * 5
    args = [pc] * 6 + [mu_c] * 6 + [w0, w2pad, a0, a2pad, g2, k_k, k_a, r_k, lnx_w, lnx_b]
    consts = _rwkv_chunk_constants()
    in_specs += [pl.BlockSpec(m.shape, lambda s, p: (0, 0)) for m in consts]
    args += list(consts)
    if has_init:
        in_specs.append(state_spec)
        args.append(s0)
    out_specs = [pl.BlockSpec((t, LANES), lambda s, p: (s, p))]
    out_shape = [jax.ShapeDtypeStruct((n_seq * t, WIDTH_C), F32)]
    if want_state:
        out_specs.append(state_spec)
        out_shape.append(jax.ShapeDtypeStruct((n_seq, 2, N_HEADS_C, HD_C, HD_C), F32))
    n_steps = t // RW_CHUNK
    sw = 2 * LANES
    return pl.pallas_call(
        functools.partial(_rwkv_kernel, t=t, has_init=has_init, want_state=want_state),
        grid=(n_seq, n_pairs),
        in_specs=in_specs,
        out_specs=out_specs,
        out_shape=out_shape,
        scratch_shapes=[
            pltpu.VMEM((t, LANES), F32),
            pltpu.VMEM((t, LANES), F32),
            pltpu.VMEM((t, LANES), F32),
            pltpu.VMEM((2, 3, t, LANES), F32),
            pltpu.VMEM((n_steps, RW_CHUNK, sw), F32),
            pltpu.VMEM((n_steps, RW_CHUNK, sw), F32),
            pltpu.VMEM((n_steps, sw, sw), F32),
            pltpu.VMEM((n_steps, sw, sw), F32),
            pltpu.VMEM((n_steps, 8, sw), F32),
            pltpu.VMEM((sw, sw), F32),
        ],
        compiler_params=pltpu.CompilerParams(dimension_semantics=("arbitrary", "arbitrary")),
        name="rwkv_lat" if has_init else "rwkv_ctx",
    )(*args)


def _merge_kernel(x_ref, oac_ref, oal_ref, obc_ref, obl_ref, occ_ref, ocl_ref, pg_ref, mod_ref,
                  wa_ref, wb_ref, wc_ref, wo_ref, o_ref, *, ctx_tiles):
    is_ctx = pl.program_id(0) < ctx_tiles
    merged = jnp.zeros(x_ref.shape, F32)
    for j, (c_ref, l_ref, w_ref) in enumerate(((oac_ref, oal_ref, wa_ref), (obc_ref, obl_ref, wb_ref),
                                              (occ_ref, ocl_ref, wc_ref))):
        gate = pg_ref[:, j * D_MODEL:(j + 1) * D_MODEL].astype(F32)
        branch = jnp.where(is_ctx, c_ref[...], l_ref[...]).astype(BF16)
        merged = merged + gate * jnp.dot(branch, w_ref[...], preferred_element_type=F32)
    y = jnp.dot(merged.astype(BF16), wo_ref[...], preferred_element_type=F32)
    o_ref[...] = x_ref[...] + mod_ref[2] * y


def _merge(li, x, branches, pg, mod, wpa, wpb, wpc, wout):
    tm = TM
    ctx_tiles = N_TOK_CTX // tm
    row = lambda w: pl.BlockSpec((tm, w), lambda i: (i, 0))
    ctx_row = lambda w: pl.BlockSpec((tm, w), lambda i: (jnp.minimum(i, ctx_tiles - 1), 0))
    lat_row = lambda w: pl.BlockSpec((tm, w), lambda i: (jnp.maximum(i - ctx_tiles, 0), 0))
    branch_specs = []
    for w in (WIDTH_A, WIDTH_BV, WIDTH_C):
        branch_specs += [ctx_row(w), lat_row(w)]
    return pl.pallas_call(
        functools.partial(_merge_kernel, ctx_tiles=ctx_tiles),
        grid=(N_TOK // tm,),
        in_specs=[row(D_MODEL)] + branch_specs + [row(SEC_G), _mod_spec(li, tm),
                  _layer_resident(li, (WIDTH_A, D_MODEL)), _layer_resident(li, (WIDTH_BV, D_MODEL)),
                  _layer_resident(li, (WIDTH_C, D_MODEL)), _layer_resident(li, (D_MODEL, D_MODEL))],
        out_specs=row(D_MODEL),
        out_shape=jax.ShapeDtypeStruct((N_TOK, D_MODEL), F32),
        compiler_params=pltpu.CompilerParams(dimension_semantics=("arbitrary",), vmem_limit_bytes=VMEM_LIMIT),
        name="merge",
    )(x, *[b for pair in branches for b in pair], pg, mod, wpa, wpb, wpc, wout)


def _ffn_kernel(x_ref, g_ref, mod_ref, wg_ref, wu_ref, wo_ref, o_ref, h_scr, acc_scr):
    f = pl.program_id(1)

    @pl.when(f == 0)
    def _():
        h_scr[...] = _modnorm(x_ref[...], g_ref[...], mod_ref[3], mod_ref[4]).astype(BF16)
        acc_scr[...] = jnp.zeros_like(acc_scr)

    h = h_scr[...]
    gate = jnp.dot(h, wg_ref[...], preferred_element_type=F32)
    up = jnp.dot(h, wu_ref[...], preferred_element_type=F32)
    act = (gate * _sigmoid(gate) * up).astype(BF16)
    acc_scr[...] += jnp.dot(act, wo_ref[...], preferred_element_type=F32)

    @pl.when(f == pl.num_programs(1) - 1)
    def _():
        o_ref[...] = x_ref[...] + mod_ref[5] * acc_scr[...]


def _ffn(li, x, gain, mod, w_in, w_out):
    tm = TM_FFN
    nf = D_FF // TF
    return pl.pallas_call(
        _ffn_kernel,
        grid=(N_TOK // tm, nf),
        in_specs=[
            pl.BlockSpec((tm, D_MODEL), lambda i, f: (i, 0)),
            pl.BlockSpec((None, 1, D_MODEL), lambda i, f: (li, 0, 0)),
            _mod_spec(li, tm),
            pl.BlockSpec((D_MODEL, TF), lambda i, f: (0, f)),
            pl.BlockSpec((D_MODEL, TF), lambda i, f: (0, nf + f)),
            pl.BlockSpec((TF, D_MODEL), lambda i, f: (f, 0)),
        ],
        out_specs=pl.BlockSpec((tm, D_MODEL), lambda i, f: (i, 0)),
        out_shape=jax.ShapeDtypeStruct((N_TOK, D_MODEL), F32),
        scratch_shapes=[pltpu.VMEM((tm, D_MODEL), BF16), pltpu.VMEM((tm, D_MODEL), F32)],
        compiler_params=pltpu.CompilerParams(dimension_semantics=("arbitrary", "arbitrary"),
                                             vmem_limit_bytes=VMEM_LIMIT),
        name="ffn",
    )(x, gain, mod, w_in, w_in, w_out)


def _pad_rows(w, offset, total):
    pads = [(0, 0)] * w.ndim
    pads[-2] = (offset, total - offset - w.shape[-2])
    return jnp.pad(w, pads)


def kernel(x_prompt, x_sample, cache_k, cache_v, state_gla, state_rwkv, c, c_ctx, w_ada, b_ada, g_mix, g_ffn,
           w_in, q_gain, k_gain, gk_w2, gk_b, gla_gain, mu_c, w0, w2, a0, a2, g2, k_k, k_a, r_k, lnx_w, lnx_b,
           w_po_a, w_po_b, w_po_c, w_out, w_ffn_in, w_ffn_out):
    x = jnp.concatenate([x_prompt.reshape(N_TOK_CTX, D_MODEL), x_sample.reshape(N_TOK_LAT, D_MODEL)], axis=0)
    cond = jnp.concatenate([c_ctx[None, :], c, jnp.zeros((COND_ROWS - N_COND, D_MODEL), F32)], axis=0)
    mod = _modulation(cond, w_ada, b_ada)[:, :N_COND].reshape(DEPTH, N_COND, 6, 1, D_MODEL)

    w_ab = w_in[:, :, :SEC_A + SEC_B].astype(BF16)
    w_cg = w_in[:, :, OFF_C:].astype(BF16)
    gk_w2p = jnp.stack([_pad_rows(gk_w2[:, 0], 0, LANES), _pad_rows(gk_w2[:, 1], GK_RANK, LANES)], axis=1)
    w2p = jnp.stack([_pad_rows(w2[:, 0], 0, LANES), _pad_rows(w2[:, 1], W_LORA, LANES)], axis=1)
    a2p = jnp.stack([_pad_rows(a2[:, 0], 0, LANES), _pad_rows(a2[:, 1], A_LORA, LANES)], axis=1)
    wpa, wpb, wpc, wout = (w.astype(BF16) for w in (w_po_a, w_po_b, w_po_c, w_out))
    rope = _rope_tables()
    lat_blk = N_TOK_CTX // DEC_SEQ
    g_mix3 = g_mix.reshape(DEPTH, 1, D_MODEL)
    g_ffn3 = g_ffn.reshape(DEPTH, 1, D_MODEL)

    ks, vs, sgs, srs = [], [], [], []
    for li in range(DEPTH):
        vec = lambda a: a[li].reshape(1, -1)
        pa, pb, pc, pg, wfi, wfo = _inproj(li, x, g_mix3, mod, w_ab, w_cg, w_ffn_in, w_ffn_out)

        qg = jnp.tile(vec(q_gain), (1, LANES // HD_A))
        kg = jnp.tile(vec(k_gain), (1, LANES // HD_A))
        oa_c, k_l, v_l = _attention(pa, qg, kg, BATCH, SEQ, 0)
        (oa_l,) = _attention(pa, qg, kg, DEC_BATCH, DEC_SEQ, lat_blk,
                             cache=(cache_k[:, li].reshape(DEC_BATCH, PAST_LEN, KV_A),
                                    cache_v[:, li].reshape(DEC_BATCH, PAST_LEN, KV_A)), rope=rope)

        gkb = gk_b[li].reshape(2, 1, WIDTH_BK)
        ob_c, sg_l = _gla(pb, gk_w2p[li], gkb, vec(gla_gain), BATCH, SEQ, 0, want_state=True)
        (ob_l,) = _gla(pb, gk_w2p[li], gkb, vec(gla_gain), DEC_BATCH, DEC_SEQ, lat_blk, s0=state_gla[:, li])

        rw_args = (pc, vec(mu_c), w0[li].reshape(2, 1, WIDTH_C), w2p[li], a0[li].reshape(2, 1, WIDTH_C), a2p[li],
                   g2[li], vec(k_k), vec(k_a), r_k[li].reshape(1, WIDTH_C), vec(lnx_w), vec(lnx_b))
        oc_c, sr_l = _rwkv(*rw_args, BATCH, SEQ, 0, want_state=True)
        (oc_l,) = _rwkv(*rw_args, DEC_BATCH, DEC_SEQ, lat_blk, s0=state_rwkv[:, li])

        x = _merge(li, x, ((oa_c, oa_l), (ob_c, ob_l), (oc_c, oc_l)), pg, mod, wpa, wpb, wpc, wout)
        x = _ffn(li, x, g_ffn3, mod, wfi, wfo)

        ks.append(k_l.reshape(BATCH, SEQ, N_KV_A, HD_A))
        vs.append(v_l.reshape(BATCH, SEQ, N_KV_A, HD_A))
        sgs.append(sg_l)
        srs.append(sr_l)

    y_prompt = x[:N_TOK_CTX].reshape(BATCH, SEQ, D_MODEL)
    y_sample = x[N_TOK_CTX:].reshape(DEC_BATCH, DEC_SEQ, D_MODEL)
    return (y_prompt, y_sample, jnp.stack(ks, axis=1), jnp.stack(vs, axis=1),
            jnp.stack(sgs, axis=1), jnp.stack(srs, axis=1))
```

```python
import functools
import math

import numpy as np
import jax
import jax.numpy as jnp
from jax import lax
from jax.experimental import pallas as pl
from jax.experimental.pallas import tpu as pltpu

D_MODEL = 1024
BATCH = 16
SEQ = 256
DEPTH = 4
DEC_BATCH = 2
DEC_SEQ = 1024
PAST_LEN = 256
GRID_W = 64
N_HEADS_A = 8
N_KV_A = 2
HD_A = 64
ROPE_THETA = 10000.0
N_HEADS_B = 4
DK_B = 64
DV_B = 128
GK_RANK = 16
GK_NORMALIZER = 16.0
N_HEADS_C = 8
HD_C = 64
W_LORA = 64
A_LORA = 64
G_LORA = 128
GN_EPS = 64e-5
NORM_EPS = 1e-6
WIDTH_A = N_HEADS_A * HD_A
KV_A = N_KV_A * HD_A
WIDTH_BK = N_HEADS_B * DK_B
WIDTH_BV = N_HEADS_B * DV_B
WIDTH_C = N_HEADS_C * HD_C
N_BRANCH = 3
D_FF = ((8 * D_MODEL + 3 * 256 - 1) // (3 * 256)) * 256
OFF_QA = 0
OFF_KA = OFF_QA + WIDTH_A
OFF_VA = OFF_KA + KV_A
OFF_QB = OFF_VA + KV_A
OFF_KB = OFF_QB + WIDTH_BK
OFF_VB = OFF_KB + WIDTH_BK
OFF_GB = OFF_VB + WIDTH_BV
OFF_GKF = OFF_GB + WIDTH_BV
OFF_GKB = OFF_GKF + GK_RANK
OFF_C = OFF_GKB + GK_RANK
C_COLS = 3 * WIDTH_C + 2 * W_LORA + 2 * A_LORA + G_LORA
OFF_GATE = OFF_C + C_COLS
N_IN = OFF_GATE + N_BRANCH * D_MODEL

F32 = jnp.float32
BF16 = jnp.bfloat16

LANES = 128
VMEM_LIMIT = 56 * 1024 * 1024

N_TOK_CTX = BATCH * SEQ
N_TOK_LAT = DEC_BATCH * DEC_SEQ
N_TOK = N_TOK_CTX + N_TOK_LAT
N_COND = 1 + DEC_BATCH
COND_ROWS = 8
SEC_A = OFF_QB
SEC_B_MAIN = OFF_GKF - OFF_QB
SEC_B = SEC_B_MAIN + LANES
SEC_C = C_COLS
SEC_G = N_BRANCH * D_MODEL
TM = 512
TM_FFN = 1024
TM_INPROJ = 256
TF = D_FF // 2
GLA_CHUNK = 64
RW_CHUNK = 64
HI = lax.Precision.HIGHEST


def _tile_group(i, tm):
    ctx_tiles = N_TOK_CTX // tm
    lat_tiles = DEC_SEQ // tm
    return jnp.where(i < ctx_tiles, 0, 1 + (i - ctx_tiles) // lat_tiles)


def _bdot(a, b):
    return jnp.dot(a.astype(BF16), b.astype(BF16), preferred_element_type=F32)


def _dot_nt(a, b, precision=None):
    return lax.dot_general(a, b, (((1,), (1,)), ((), ())), preferred_element_type=F32, precision=precision)


def _dot_tn(a, b, precision=None):
    return lax.dot_general(a, b, (((0,), (0,)), ((), ())), preferred_element_type=F32, precision=precision)


def _sigmoid(x):
    return jax.nn.sigmoid(x)


def _modnorm(x, gain, shift, scale):
    ms = jnp.mean(x * x, axis=-1, keepdims=True)
    y = x * lax.rsqrt(ms + NORM_EPS) * gain
    return y * (1.0 + scale) + shift


def _mod_kernel(c_ref, w_ref, b_ref, o_ref):
    c = c_ref[...]
    s = c * _sigmoid(c)
    o_ref[...] = _bdot(s, w_ref[...]) + b_ref[...]


def _modulation(cond, w_ada, b_ada):
    tn = 1536
    n_out = 6 * D_MODEL
    return pl.pallas_call(
        _mod_kernel,
        grid=(DEPTH, n_out // tn),
        in_specs=[
            pl.BlockSpec((COND_ROWS, D_MODEL), lambda l, j: (0, 0)),
            pl.BlockSpec((None, D_MODEL, tn), lambda l, j: (l, 0, j)),
            pl.BlockSpec((None, 1, tn), lambda l, j: (l, 0, j)),
        ],
        out_specs=pl.BlockSpec((None, COND_ROWS, tn), lambda l, j: (l, 0, j)),
        out_shape=jax.ShapeDtypeStruct((DEPTH, COND_ROWS, n_out), F32),
        compiler_params=pltpu.CompilerParams(dimension_semantics=("arbitrary", "arbitrary")),
        name="modulation",
    )(cond, w_ada, b_ada.reshape(DEPTH, 1, n_out))


def _inproj_kernel(x_ref, g_ref, mod_ref, wab_ref, wcg_ref, fi_ref, fo_ref,
                   oa_ref, ob_ref, oc_ref, og_ref, fi16_ref, fo16_ref):
    fi16_ref[...] = fi_ref[...].astype(BF16)
    fo16_ref[...] = fo_ref[...].astype(BF16)
    h = _modnorm(x_ref[...], g_ref[...], mod_ref[0], mod_ref[1]).astype(BF16)
    oa_ref[...] = jnp.dot(h, wab_ref[:, 0:SEC_A], preferred_element_type=F32)
    ob_ref[...] = jnp.dot(h, wab_ref[:, SEC_A:SEC_A + SEC_B], preferred_element_type=F32)
    oc_ref[...] = jnp.dot(h, wcg_ref[:, 0:SEC_C], preferred_element_type=F32)
    og_ref[...] = _sigmoid(jnp.dot(h, wcg_ref[:, SEC_C:SEC_C + SEC_G], preferred_element_type=F32)).astype(BF16)


def _resident(shape):
    nd = len(shape)
    return pl.BlockSpec(shape, lambda *_: (0,) * nd, pipeline_mode=pl.Buffered(1))


def _layer_resident(li, shape):
    nd = len(shape)
    return pl.BlockSpec((None,) + tuple(shape), lambda *_: (li,) + (0,) * nd, pipeline_mode=pl.Buffered(1))


def _mod_spec(li, tm):
    return pl.BlockSpec((None, None, 6, 1, D_MODEL), lambda i, *_: (li, _tile_group(i, tm), 0, 0, 0))


def _inproj(li, x, gain, mod, w_ab, w_cg, w_ffn_in, w_ffn_out):
    tm = TM_INPROJ
    n_steps = N_TOK // tm
    widths = (SEC_A, SEC_B, SEC_C, SEC_G)
    dtypes = (F32, F32, F32, BF16)
    slabs = D_FF // LANES
    in_cols = (2 * D_FF) // slabs
    assert slabs <= n_steps and in_cols % LANES == 0
    slab = lambda i: jnp.minimum(i, slabs - 1)
    return pl.pallas_call(
        _inproj_kernel,
        grid=(n_steps,),
        in_specs=[
            pl.BlockSpec((tm, D_MODEL), lambda i: (i, 0)),
            _layer_resident(li, (1, D_MODEL)),
            _mod_spec(li, tm),
            _layer_resident(li, (D_MODEL, SEC_A + SEC_B)),
            _layer_resident(li, (D_MODEL, SEC_C + SEC_G)),
            pl.BlockSpec((None, D_MODEL, in_cols), lambda i: (li, 0, slab(i))),
            pl.BlockSpec((None, LANES, D_MODEL), lambda i: (li, slab(i), 0)),
        ],
        out_specs=[pl.BlockSpec((tm, w), lambda i: (i, 0)) for w in widths] + [
            pl.BlockSpec((D_MODEL, in_cols), lambda i: (0, slab(i))),
            pl.BlockSpec((LANES, D_MODEL), lambda i: (slab(i), 0)),
        ],
        out_shape=[jax.ShapeDtypeStruct((N_TOK, w), dt) for w, dt in zip(widths, dtypes)] + [
            jax.ShapeDtypeStruct((D_MODEL, 2 * D_FF), BF16),
            jax.ShapeDtypeStruct((D_FF, D_MODEL), BF16),
        ],
        compiler_params=pltpu.CompilerParams(dimension_semantics=("arbitrary",), vmem_limit_bytes=VMEM_LIMIT),
        name="inproj",
    )(x, gain, mod, w_ab, w_cg, w_ffn_in, w_ffn_out)


def _segsum(x, width):
    n = x.shape[-1]
    gi = lax.broadcasted_iota(jnp.int32, (n, n), 0) // width
    gj = lax.broadcasted_iota(jnp.int32, (n, n), 1) // width
    e = jnp.where(gi == gj, 1.0, 0.0).astype(BF16)
    hi = x.astype(BF16)
    lo = (x - hi.astype(F32)).astype(BF16)
    return jnp.dot(jnp.concatenate([hi, lo], axis=1), jnp.concatenate([e, e], axis=0),
                   preferred_element_type=F32)


def _swap_pairs(x):
    lane = lax.broadcasted_iota(jnp.int32, x.shape, 1)
    nxt = pltpu.roll(x, LANES - 1, 1)
    prv = pltpu.roll(x, 1, 1)
    return jnp.where(lane % 2 == 0, nxt, prv)


def _head_norm(x, gain):
    ms = _segsum(x * x, HD_A) * (1.0 / HD_A)
    return x * lax.rsqrt(ms + NORM_EPS) * gain


def _attn_kernel(*refs, t, latent):
    if latent:
        (pa_ref, qg_ref, kg_ref, ck_ref, cv_ref, cos_ref, sin_ref, o_ref, k_scr, v_scr, q_scr) = refs
    else:
        (pa_ref, qg_ref, kg_ref, o_ref, ko_ref, vo_ref, k_scr, v_scr, q_scr) = refs
    tk = k_scr.shape[0]
    past = tk - t
    kn = _head_norm(pa_ref[:, OFF_KA:OFF_VA], kg_ref[...])
    va = pa_ref[:, OFF_VA:OFF_QB]
    if latent:
        cos = cos_ref[...]
        sin = sin_ref[...]
        kn = kn * cos + _swap_pairs(kn) * sin
        k_scr[0:past, :] = ck_ref[...]
        v_scr[0:past, :] = cv_ref[...]
    else:
        ko_ref[...] = kn
        vo_ref[...] = va
    k_scr[past:tk, :] = kn
    v_scr[past:tk, :] = va
    for c in range(WIDTH_A // LANES):
        qn = _head_norm(pa_ref[:, c * LANES:(c + 1) * LANES], qg_ref[...])
        if latent:
            qn = qn * cos + _swap_pairs(qn) * sin
        q_scr[:, c * LANES:(c + 1) * LANES] = qn * (HD_A ** -0.5)

    group = N_HEADS_A // N_KV_A
    tq = min(t, 256)
    kvs = [(k_scr[:, kv * HD_A:(kv + 1) * HD_A].astype(BF16), v_scr[:, kv * HD_A:(kv + 1) * HD_A].astype(BF16))
           for kv in range(N_KV_A)]
    for qb in range(t // tq):
        rows = slice(qb * tq, (qb + 1) * tq)

        def one_group(kv, rows=rows):
            kh, vh = kvs[kv]
            qs = jnp.concatenate([q_scr[rows, (kv * group + g) * HD_A:(kv * group + g + 1) * HD_A]
                                  for g in range(group)], axis=0).astype(BF16)
            s = _dot_nt(qs, kh)
            yield
            m = jnp.max(s, axis=-1, keepdims=True)
            e = jnp.exp(s - m)
            pv = jnp.dot(e.astype(BF16), vh, preferred_element_type=F32)
            yield
            out = pv / jnp.sum(e, axis=-1, keepdims=True)
            for j in range(group // 2):
                pair = jnp.concatenate([out[2 * j * tq:(2 * j + 1) * tq], out[(2 * j + 1) * tq:(2 * j + 2) * tq]],
                                       axis=1)
                col = (kv * group + 2 * j) * HD_A
                o_ref[rows, col:col + LANES] = pair

        _trace_in_lockstep(one_group, range(N_KV_A))


def _attention(pa, q_gain, k_gain, n_seq, t, row_block0, cache=None, rope=None):
    latent = cache is not None
    tk = t + (PAST_LEN if latent else 0)
    in_specs = [
        pl.BlockSpec((t, SEC_A), lambda s: (row_block0 + s, 0)),
        _resident((1, LANES)),
        _resident((1, LANES)),
    ]
    args = [pa, q_gain, k_gain]
    out_specs = [pl.BlockSpec((t, WIDTH_A), lambda s: (s, 0))]
    out_shape = [jax.ShapeDtypeStruct((n_seq * t, WIDTH_A), F32)]
    if latent:
        in_specs += [
            pl.BlockSpec((None, PAST_LEN, KV_A), lambda s: (s, 0, 0)),
            pl.BlockSpec((None, PAST_LEN, KV_A), lambda s: (s, 0, 0)),
            _resident((t, LANES)),
            _resident((t, LANES)),
        ]
        args += [cache[0], cache[1], rope[0], rope[1]]
    else:
        out_specs += [pl.BlockSpec((None, t, KV_A), lambda s: (s, 0, 0))] * 2
        out_shape += [jax.ShapeDtypeStruct((n_seq, t, KV_A), F32)] * 2
    return pl.pallas_call(
        functools.partial(_attn_kernel, t=t, latent=latent),
        grid=(n_seq,),
        in_specs=in_specs,
        out_specs=out_specs,
        out_shape=out_shape,
        scratch_shapes=[
            pltpu.VMEM((tk, KV_A), F32),
            pltpu.VMEM((tk, KV_A), F32),
            pltpu.VMEM((t, WIDTH_A), F32),
        ],
        compiler_params=pltpu.CompilerParams(dimension_semantics=("arbitrary",), vmem_limit_bytes=VMEM_LIMIT),
        name="attn_lat" if latent else "attn_ctx",
    )(*args)


def _rope_tables():
    tpos = np.arange(DEC_SEQ)
    row = (tpos // GRID_W).astype(np.float32)
    col = (tpos % GRID_W).astype(np.float32)
    n_pairs = HD_A // 4
    inv = (ROPE_THETA ** (-np.arange(n_pairs, dtype=np.float32) / n_pairs)).astype(np.float32)
    ang = np.concatenate([row[:, None] * inv, col[:, None] * inv], axis=-1)
    cos = np.repeat(np.cos(ang), 2, axis=-1)
    sin = np.repeat(np.sin(ang), 2, axis=-1)
    sign = np.tile(np.array([-1.0, 1.0], np.float32), HD_A // 2)
    reps = LANES // HD_A
    return (jnp.asarray(np.tile(cos, (1, reps)), F32), jnp.asarray(np.tile(sin * sign, (1, reps)), F32))


def _tri(n, reverse, strict):
    i = lax.broadcasted_iota(jnp.int32, (n, n), 0)
    j = lax.broadcasted_iota(jnp.int32, (n, n), 1)
    if reverse:
        keep = (j > i) if strict else (j >= i)
    else:
        keep = (j < i) if strict else (j <= i)
    return keep


_DONE = object()


def _trace_in_lockstep(step_fn, steps):
    _run_lockstep([step_fn(i) for i in steps])


def _run_lockstep(generators):
    live = list(generators)
    while live:
        live = [g for g in live if next(g, _DONE) is not _DONE]


def _for_each_step_group(step_fn, n_steps, group, first=0):
    count = n_steps - first
    if count <= group:
        _trace_in_lockstep(step_fn, range(first, n_steps))
        return
    assert count % group == 0, (n_steps, first, group)

    def body(j, carry):
        _trace_in_lockstep(step_fn, [first + j * group + k for k in range(group)])
        return carry

    lax.fori_loop(0, count // group, body, 0)


def _chunk_rows(index, c):
    if isinstance(index, int):
        return pl.ds(index * c, c)
    return pl.ds(pl.multiple_of(index * c, c), c)


def _head_masks(shape):
    lane = lax.broadcasted_iota(jnp.int32, shape, 1)
    first = lane < (LANES // 2)
    return first, jnp.logical_not(first)


def _log_sigmoid(z):
    return jnp.minimum(z, 0.0) - jnp.log1p(jnp.exp(-jnp.abs(z)))


def _gla_kernel(*refs, t, has_init, want_state):
    refs = list(refs)
    (q_ref, k_ref, v_ref, g_ref, lr_ref, w2_ref, gb_ref, gain_ref,
     cum_ref, incl_ref, nmask_ref) = refs[:11]
    pos = 11
    if has_init:
        s0_ref = refs[pos]
        pos += 1
    o_ref = refs[pos]
    pos += 1
    if want_state:
        so_ref = refs[pos]
        pos += 1
    lg_scr, o_scr, qh_scr, n_scr, dec_scr, st_scr = refs[pos:pos + 6]

    c = GLA_CHUNK
    n_chunks = t // c
    half = LANES // 2
    jj = lax.broadcasted_iota(jnp.int32, (half, LANES), 0)
    ll = lax.broadcasted_iota(jnp.int32, (half, LANES), 1)
    sel = [jnp.where(ll == jj + h * half, 1.0, 0.0).astype(F32) for h in range(2)]

    lr = lr_ref[...].astype(BF16)
    for d in range(2):
        z = jnp.dot(lr, w2_ref[d].astype(BF16), preferred_element_type=F32) + gb_ref[d]
        lg_scr[d] = _log_sigmoid(z) * (1.0 / GK_NORMALIZER)
        for h in range(2):
            if has_init:
                st_scr[d, h * DV_B:(h + 1) * DV_B, :] = _dot_tn(s0_ref[d, h], sel[h], precision=HI)
            else:
                st_scr[d, h * DV_B:(h + 1) * DV_B, :] = jnp.zeros((DV_B, LANES), F32)

    m0, m1 = _head_masks((c, LANES))
    scale = DK_B ** -0.5

    def chunk_terms(i):
        rows_f = _chunk_rows(i, c)
        rows_b = _chunk_rows(n_chunks - 1 - i, c)
        q = jnp.concatenate([q_ref[rows_f, :], q_ref[rows_b, :]], axis=0)
        k = jnp.concatenate([k_ref[rows_f, :], k_ref[rows_b, :]], axis=0)
        g = jnp.concatenate([lg_scr[0, rows_f, :], lg_scr[1, rows_b, :]], axis=0)
        b = _mm_exact_lhs(cum_ref[...], g)
        yield
        bm = jnp.concatenate([jnp.broadcast_to(b[c // 2:c // 2 + 1], (c, LANES)),
                              jnp.broadcast_to(b[c + c // 2:c + c // 2 + 1], (c, LANES))], axis=0)
        bl = jnp.concatenate([jnp.broadcast_to(b[c - 1:c], (c, LANES)),
                              jnp.broadcast_to(b[c:c + 1], (c, LANES))], axis=0)
        e1 = jnp.exp(b - bm)
        e2 = jnp.exp(bm - b)
        qt = q * scale * e1
        kt = k * e2
        qh_scr[i] = qt * jnp.exp(bm)
        kc = kt * jnp.exp(bl - bm)
        qf, qb = qt[0:c], qt[c:2 * c]
        lhs = jnp.concatenate([jnp.where(m0, qf, 0.0), jnp.where(m1, qf, 0.0),
                               jnp.where(m0, qb, 0.0), jnp.where(m1, qb, 0.0)], axis=0)
        rhs = jnp.concatenate([kt[0:c], kt[0:c], kt[c:2 * c], kt[c:2 * c]], axis=0)
        att = _mm(lhs, rhs, "nt", GLA_P_ATT) * incl_ref[...]
        yield
        vf = v_ref[rows_f, :]
        vb = v_ref[rows_b, :]
        vst = jnp.concatenate([vf[:, 0:DV_B], vf[:, DV_B:2 * DV_B], vb[:, 0:DV_B], vb[:, DV_B:2 * DV_B]], axis=0)
        o_scr[i] = _mm(att, vst, "nn", 1)
        yield
        n_scr[i, 0] = _mm(vf, kc[0:c], "tn", 1) * nmask_ref[...]
        yield
        n_scr[i, 1] = _mm(vb, kc[c:2 * c], "tn", 1) * nmask_ref[...]
        dec = jnp.exp(bl)
        dec_scr[i, 0] = dec[0:8]
        dec_scr[i, 1] = dec[c:c + 8]

    _for_each_step_group(chunk_terms, n_chunks, GLA_UNROLL)

    def state_scan(i, carry):
        for d in range(2):
            st = st_scr[d]
            st_scr[d] = st * dec_scr[i, d, 0:1, :] + n_scr[i, d]
            n_scr[i, d] = st
        return carry

    lax.fori_loop(0, n_chunks, state_scan, 0)

    def inter_terms(i):
        for d in range(2):
            oi = _mm(qh_scr[i, d * c:(d + 1) * c, :], n_scr[i, d], "nt", 1)
            yield
            for h in range(2):
                r0 = (2 * d + h) * c
                o_scr[i, r0:r0 + c, :] = o_scr[i, r0:r0 + c, :] + oi[:, h * DV_B:(h + 1) * DV_B]

    _for_each_step_group(inter_terms, n_chunks, GLA_UNROLL, first=0 if has_init else 1)

    for h in range(2):
        o = jnp.concatenate([o_scr[j, h * c:(h + 1) * c, :] + o_scr[n_chunks - 1 - j, (2 + h) * c:(3 + h) * c, :]
                             for j in range(n_chunks)], axis=0)
        ms = jnp.mean(o * o, axis=-1, keepdims=True)
        o = o * lax.rsqrt(ms + NORM_EPS) * gain_ref[...]
        gate = g_ref[:, h * DV_B:(h + 1) * DV_B]
        o_ref[:, h * DV_B:(h + 1) * DV_B] = o * (gate * _sigmoid(gate))
        if want_state:
            for d in range(2):
                so_ref[d, h] = _dot_nt(sel[h], st_scr[d, h * DV_B:(h + 1) * DV_B, :], precision=HI)


def _gla(pb, w2pad, gk_b, gain, n_seq, t, row_block0, s0=None, want_state=False):
    has_init = s0 is not None
    pair_w = 2 * DV_B
    state_spec = pl.BlockSpec((None, 2, 2, DK_B, DV_B), lambda s, p: (s, 0, p, 0, 0))
    in_specs = [
        pl.BlockSpec((t, LANES), lambda s, p: (row_block0 + s, p)),
        pl.BlockSpec((t, LANES), lambda s, p: (row_block0 + s, WIDTH_BK // LANES + p)),
        pl.BlockSpec((t, pair_w), lambda s, p: (row_block0 + s, (2 * WIDTH_BK) // pair_w + p)),
        pl.BlockSpec((t, pair_w), lambda s, p: (row_block0 + s, (2 * WIDTH_BK + WIDTH_BV) // pair_w + p)),
        pl.BlockSpec((t, LANES), lambda s, p: (row_block0 + s, SEC_B_MAIN // LANES)),
        pl.BlockSpec((2, LANES, LANES), lambda s, p: (0, 0, p)),
        pl.BlockSpec((2, 1, LANES), lambda s, p: (0, 0, p)),
        pl.BlockSpec((1, DV_B), lambda s, p: (0, 0)),
    ]
    args = [pb, pb, pb, pb, pb, w2pad, gk_b, gain]
    cum, _, incl, _, _ = _rwkv_chunk_constants()
    ri = np.arange(pair_w)[:, None] // DV_B
    li = np.arange(LANES)[None, :] // DK_B
    consts = (cum, incl, jnp.asarray((ri == li).astype(np.float32)))
    in_specs += [pl.BlockSpec(m.shape, lambda s, p: (0, 0)) for m in consts]
    args += list(consts)
    if has_init:
        in_specs.append(state_spec)
        args.append(s0)
    n_steps = t // GLA_CHUNK
    out_specs = [pl.BlockSpec((t, pair_w), lambda s, p: (s, p))]
    out_shape = [jax.ShapeDtypeStruct((n_seq * t, WIDTH_BV), F32)]
    if want_state:
        out_specs.append(state_spec)
        out_shape.append(jax.ShapeDtypeStruct((n_seq, 2, N_HEADS_B, DK_B, DV_B), F32))
    return pl.pallas_call(
        functools.partial(_gla_kernel, t=t, has_init=has_init, want_state=want_state),
        grid=(n_seq, N_HEADS_B // 2),
        in_specs=in_specs,
        out_specs=out_specs,
        out_shape=out_shape,
        scratch_shapes=[
            pltpu.VMEM((2, t, LANES), F32),
            pltpu.VMEM((n_steps, 4 * GLA_CHUNK, DV_B), F32),
            pltpu.VMEM((n_steps, 2 * GLA_CHUNK, LANES), F32),
            pltpu.VMEM((n_steps, 2, pair_w, LANES), F32),
            pltpu.VMEM((n_steps, 2, 8, LANES), F32),
            pltpu.VMEM((2, pair_w, LANES), F32),
        ],
        compiler_params=pltpu.CompilerParams(dimension_semantics=("arbitrary", "arbitrary")),
        name="gla_lat" if has_init else "gla_ctx",
    )(*args)


def _shift_mix(u, mu):
    t = u.shape[0]
    row = lax.broadcasted_iota(jnp.int32, u.shape, 0)
    prev = jnp.where(row == 0, 0.0, pltpu.roll(u, 1, 0))
    nxt = jnp.where(row == t - 1, 0.0, pltpu.roll(u, t - 1, 0))
    return u + mu * (0.5 * (prev + nxt) - u)


_DIMS = {"nn": (((1,), (0,)), ((), ())), "nt": (((1,), (1,)), ((), ())), "tn": (((0,), (0,)), ((), ()))}


def _split_bf16(x):
    hi = x.astype(BF16)
    lo = (x - hi.astype(F32)).astype(BF16)
    return hi, lo


def _mm(a, b, form, passes):
    dims = _DIMS[form]
    if passes == 6:
        return lax.dot_general(a, b, dims, preferred_element_type=F32, precision=HI)
    if passes == 1:
        return lax.dot_general(a.astype(BF16), b.astype(BF16), dims, preferred_element_type=F32)
    a_hi, a_lo = _split_bf16(a)
    b_hi, b_lo = _split_bf16(b)
    ka = dims[0][0][0]
    kb = dims[0][1][0]
    a_cat = jnp.concatenate([a_hi, a_lo, a_hi], axis=ka)
    b_cat = jnp.concatenate([b_hi, b_hi, b_lo], axis=kb)
    return lax.dot_general(a_cat, b_cat, dims, preferred_element_type=F32)


def _mm_exact_lhs(a, x):
    a16 = a.astype(BF16)
    x_hi, x_lo = _split_bf16(x)
    return jnp.dot(jnp.concatenate([a16, a16], axis=1), jnp.concatenate([x_hi, x_lo], axis=0),
                   preferred_element_type=F32)


GLA_P_ATT = 1
GLA_UNROLL = 4

RW_UNROLL = 4


def _half_neumann(n_mat, eye, c):
    sw = n_mat.shape[0]
    p = eye + n_mat
    m = _mm(n_mat, n_mat, "nn", 1)
    yield
    for _ in range(int(math.log2(c)) - 3):
        both = _mm(jnp.concatenate([m, p], axis=0), m, "nn", 1)
        yield
        m = both[0:sw]
        p = p + both[sw:2 * sw]
    return p + _mm(p, m, "nn", 1)


def _rwkv_kernel(*refs, t, has_init, want_state):
    refs = list(refs)
    (r_ref, k_ref, v_ref, wl_ref, al_ref, gl_ref,
     mur_ref, muk_ref, muv_ref, muw_ref, mua_ref, mug_ref,
     w0_ref, w2_ref, a0_ref, a2_ref, g2_ref, kk_ref, ka_ref, rk_ref, lw_ref, lb_ref,
     cum_ref, strict_ref, incl_ref, blk_ref, eye_ref) = refs[:27]
    pos = 27
    if has_init:
        s0_ref = refs[pos]
        pos += 1
    o_ref = refs[pos]
    pos += 1
    if want_state:
        so_ref = refs[pos]
        pos += 1
    r_scr, v_scr, a_scr, dir_scr, y_scr, rp_scr, m_scr, n_scr, dec_scr, st_scr = refs[pos:pos + 10]

    c = RW_CHUNK
    n_chunks = t // c
    half = LANES // 2

    r = _shift_mix(r_ref[...], mur_ref[...])
    kc = _shift_mix(k_ref[...], muk_ref[...])
    vc = _shift_mix(v_ref[...], muv_ref[...])
    wl_in = jnp.tanh(_shift_mix(wl_ref[...], muw_ref[...]))
    al_in = _shift_mix(al_ref[...], mua_ref[...])
    g_in = _sigmoid(_shift_mix(gl_ref[...], mug_ref[...]))

    kkf = kc * kk_ref[...]
    kk = kkf * lax.rsqrt(_segsum(kkf * kkf, HD_C) + 1e-12)
    r_scr[...] = r
    v_scr[...] = vc
    a_scr[...] = -kk
    bonus = jnp.zeros_like(r)
    for d in range(2):
        wl = w0_ref[d] + _bdot(wl_in, w2_ref[d])
        dir_scr[d, 0] = -math.exp(-0.5) * _sigmoid(wl)
        a = _sigmoid(a0_ref[d] + _bdot(al_in, a2_ref[d]))
        kd = kc * (1.0 + (a - 1.0) * ka_ref[...])
        dir_scr[d, 1] = kd
        dir_scr[d, 2] = kk * a
        bonus = bonus + _segsum(r * kd * rk_ref[...], HD_C)
    bonus = bonus * vc
    g_c = _bdot(g_in, g2_ref[...])

    nblk = 2 * (LANES // HD_C)
    sw = nblk * HD_C
    if has_init:
        z = jnp.zeros((HD_C, HD_C), F32)
        blocks = [s0_ref[0, 0], s0_ref[0, 1], s0_ref[1, 0], s0_ref[1, 1]]
        st_scr[...] = jnp.concatenate(
            [jnp.concatenate([blocks[i] if i == j else z for j in range(nblk)], axis=1) for i in range(nblk)], axis=0)
    else:
        st_scr[...] = jnp.zeros((sw, sw), F32)

    m0, m1 = _head_masks((c, LANES))

    def per_head(x):
        xf, xb = x[0:c], x[c:2 * c]
        return jnp.concatenate([jnp.where(m0, xf, 0.0), jnp.where(m1, xf, 0.0),
                                jnp.where(m0, xb, 0.0), jnp.where(m1, xb, 0.0)], axis=0)

    def per_block(x):
        xf, xb = x[0:c], x[c:2 * c]
        return jnp.concatenate([xf, xf, xb, xb], axis=0)

    def side_by_side(x):
        return jnp.concatenate([x[0:c], x[c:2 * c]], axis=1)

    def heads_select(x):
        return jnp.concatenate([jnp.where(m0, x[0:c], x[c:2 * c]), jnp.where(m0, x[2 * c:3 * c], x[3 * c:4 * c])],
                               axis=1)

    def heads_add(x):
        return jnp.concatenate([x[0:c] + x[c:2 * c], x[2 * c:3 * c] + x[3 * c:4 * c]], axis=1)

    def chunk_terms(i):
        rows_f = _chunk_rows(i, c)
        rows_b = _chunk_rows(n_chunks - 1 - i, c)

        def both(ref_f, ref_b):
            return jnp.concatenate([ref_f[rows_f, :], ref_b[rows_b, :]], axis=0)

        rr = both(r_scr, r_scr)
        vv = both(v_scr, v_scr)
        aa = both(a_scr, a_scr)
        lw = both(dir_scr.at[0, 0], dir_scr.at[1, 0])
        kd = both(dir_scr.at[0, 1], dir_scr.at[1, 1])
        bb = both(dir_scr.at[0, 2], dir_scr.at[1, 2])
        cs = _mm_exact_lhs(cum_ref[...], lw)
        yield
        mid = lambda x: jnp.concatenate([jnp.broadcast_to(x[c // 2:c // 2 + 1], (c, LANES)),
                                         jnp.broadcast_to(x[c + c // 2:c + c // 2 + 1], (c, LANES))], axis=0)
        last = lambda x: jnp.concatenate([jnp.broadcast_to(x[c - 1:c], (c, LANES)),
                                          jnp.broadcast_to(x[c:c + 1], (c, LANES))], axis=0)
        cm = mid(cs)
        cl = last(cs)
        e1 = jnp.exp(cs - cm)
        e2 = jnp.exp(cm - cs)
        ecm = jnp.exp(cm)
        ecl = jnp.exp(cl - cm)
        rt = rr * e1
        at = aa * e1 * jnp.exp(-lw)
        bt = bb * e2
        kt = kd * e2
        bc = bt * ecl
        kcl = kt * ecl
        lhs = jnp.concatenate([per_head(at), per_head(rt)], axis=0)
        gram = _mm(lhs, jnp.concatenate([per_block(bt), per_block(kt)], axis=0), "nt", 1)
        yield
        a_ab = gram[0:sw, 0:sw] * strict_ref[...]
        a_rb = gram[sw:2 * sw, 0:sw] * incl_ref[...]
        a_ak = gram[0:sw, sw:2 * sw] * strict_ref[...]
        a_rk = gram[sw:2 * sw, sw:2 * sw] * incl_ref[...]
        p_half = yield from _half_neumann(a_ab, eye_ref[...], c)
        yield
        vst = per_block(vv)
        av = _mm(a_ak, vst, "nn", 1)
        yield
        rhs = jnp.concatenate([av, per_head(at * ecm)], axis=1)
        wa = _mm(p_half, rhs, "nn", 1)
        yield
        res = rhs - wa + _mm(a_ab, wa, "nn", 3)
        yield
        wa = wa + _mm(p_half, res, "nn", 1)
        yield
        ya = _mm(jnp.concatenate([a_rb, a_rk], axis=1),
                 jnp.concatenate([wa, jnp.concatenate([vst, jnp.zeros_like(vst)], axis=1)], axis=0), "nn", 1)
        yield
        rp = per_head(rt * ecm) + ya[:, LANES:2 * LANES]
        w1 = heads_select(wa[:, 0:LANES])
        ap = heads_add(wa[:, LANES:2 * LANES])
        y_scr[i] = heads_select(ya[:, 0:LANES])
        rp_scr[i] = heads_add(rp)
        vv2 = side_by_side(vv)
        lhs_t = jnp.concatenate([jnp.concatenate([ap, w1], axis=1),
                                 jnp.concatenate([jnp.zeros_like(vv2), vv2], axis=1)], axis=0)
        mn = _mm(lhs_t, jnp.concatenate([side_by_side(bc), side_by_side(kcl)], axis=0), "tn", 1)
        m_scr[i] = mn[0:sw] * blk_ref[...]
        n_scr[i] = mn[sw:2 * sw] * blk_ref[...]
        dec_scr[i] = jnp.broadcast_to(side_by_side(jnp.exp(cl))[0:1], (8, 2 * LANES))

    def scan_steps(steps):
        for i in steps:
            st = st_scr[...]
            y_scr[i] = y_scr[i] + _mm(rp_scr[i], st, "nt", 1)
            yield
            st_scr[...] = st * dec_scr[i, 0:1, :] + _mm(st, m_scr[i], "nn", 1) + n_scr[i]
            yield

    group = RW_UNROLL
    if n_chunks <= group:
        _run_lockstep([chunk_terms(i) for i in range(n_chunks)])
        if has_init:
            _run_lockstep([scan_steps(range(n_chunks))])
        else:
            st_scr[...] = n_scr[0]
            _run_lockstep([scan_steps(range(1, n_chunks))])
    else:
        n_groups = n_chunks // group
        _run_lockstep([chunk_terms(k) for k in range(group)])

        def terms_and_scan(j, carry):
            _run_lockstep([chunk_terms(j * group + k) for k in range(group)]
                          + [scan_steps([(j - 1) * group + k for k in range(group)])])
            return carry

        lax.fori_loop(1, n_groups, terms_and_scan, 0)
        _run_lockstep([scan_steps(range((n_groups - 1) * group, n_chunks))])

    y = jnp.concatenate([y_scr[j, :, 0:LANES] + y_scr[n_chunks - 1 - j, :, LANES:2 * LANES]
                         for j in range(n_chunks)], axis=0)
    mean = _segsum(y, HD_C) * (1.0 / HD_C)
    yc = y - mean
    var = _segsum(yc * yc, HD_C) * (1.0 / HD_C)
    gn = yc * lax.rsqrt(var + GN_EPS) * lw_ref[...] + lb_ref[...]
    o_ref[...] = (gn + bonus) * g_c
    if want_state:
        for d in range(2):
            for h in range(2):
                b0 = (2 * d + h) * HD_C
                so_ref[d, h] = st_scr[b0:b0 + HD_C, b0:b0 + HD_C]


def _rwkv_chunk_constants():
    c = RW_CHUNK
    sw = 2 * LANES
    i = np.arange(sw)[:, None]
    j = np.arange(sw)[None, :]
    same = (i // c) == (j // c)
    bwd = i >= LANES
    strict = same & np.where(bwd, j > i, j < i)
    incl = same & np.where(bwd, j >= i, j <= i)
    blk = (i // HD_C) == (j // HD_C)
    ci = np.arange(2 * c)[:, None]
    cj = np.arange(2 * c)[None, :]
    cum = ((ci // c) == (cj // c)) & np.where(ci >= c, cj >= ci, cj <= ci)
    f = lambda m: jnp.asarray(m.astype(np.float32))
    return f(cum), f(strict), f(incl), f(blk), f(np.eye(sw))


def _rwkv(pc, mu_c, w0, w2pad, a0, a2pad, g2, k_k, k_a, r_k, lnx_w, lnx_b, n_seq, t, row_block0,
          s0=None, want_state=False):
    has_init = s0 is not None
    n_pairs = WIDTH_C // LANES
    blk_w = 3 * n_pairs
    blk_a = blk_w + 1
    blk_g = blk_w + 2
    state_spec = pl.BlockSpec((None, 2, 2, HD_C, HD_C), lambda s, p: (s, 0, p, 0, 0))

    def sec(col_fn):
        return pl.BlockSpec((t, LANES), lambda s, p: (row_block0 + s, col_fn(p)))

    def vec(col_fn):
        return pl.BlockSpec((1, LANES), lambda s, p: (0, col_fn(p)))

    col_fns = [lambda p: p, lambda p: n_pairs + p, lambda p: 2 * n_pairs + p,
               lambda p: blk_w, lambda p: blk_a, lambda p: blk_g]
    in_specs = [sec(f) for f in col_fns] + [vec(f) for f in col_fns]
    in_specs += [
        pl.BlockSpec((2, 1, LANES), lambda s, p: (0, 0, p)),
        pl.BlockSpec((2, LANES, LANES), lambda s, p: (0, 0, p)),
        pl.BlockSpec((2, 1, LANES), lambda s, p: (0, 0, p)),
        pl.BlockSpec((2, LANES, LANES), lambda s, p: (0, 0, p)),
        pl.BlockSpec((LANES, LANES), lambda s, p: (0, p)),
    ] + [vec(lambda p: p)] * 5
    args = [pc] * 6 + [mu_c] * 6 + [w0, w2pad, a0, a2pad, g2, k_k, k_a, r_k, lnx_w, lnx_b]
    consts = _rwkv_chunk_constants()
    in_specs += [pl.BlockSpec(m.shape, lambda s, p: (0, 0)) for m in consts]
    args += list(consts)
    if has_init:
        in_specs.append(state_spec)
        args.append(s0)
    out_specs = [pl.BlockSpec((t, LANES), lambda s, p: (s, p))]
    out_shape = [jax.ShapeDtypeStruct((n_seq * t, WIDTH_C), F32)]
    if want_state:
        out_specs.append(state_spec)
        out_shape.append(jax.ShapeDtypeStruct((n_seq, 2, N_HEADS_C, HD_C, HD_C), F32))
    n_steps = t // RW_CHUNK
    sw = 2 * LANES
    return pl.pallas_call(
        functools.partial(_rwkv_kernel, t=t, has_init=has_init, want_state=want_state),
        grid=(n_seq, n_pairs),
        in_specs=in_specs,
        out_specs=out_specs,
        out_shape=out_shape,
        scratch_shapes=[
            pltpu.VMEM((t, LANES), F32),
            pltpu.VMEM((t, LANES), F32),
            pltpu.VMEM((t, LANES), F32),
            pltpu.VMEM((2, 3, t, LANES), F32),
            pltpu.VMEM((n_steps, RW_CHUNK, sw), F32),
            pltpu.VMEM((n_steps, RW_CHUNK, sw), F32),
            pltpu.VMEM((n_steps, sw, sw), F32),
            pltpu.VMEM((n_steps, sw, sw), F32),
            pltpu.VMEM((n_steps, 8, sw), F32),
            pltpu.VMEM((sw, sw), F32),
        ],
        compiler_params=pltpu.CompilerParams(dimension_semantics=("arbitrary", "arbitrary")),
        name="rwkv_lat" if has_init else "rwkv_ctx",
    )(*args)


def _merge_kernel(x_ref, oac_ref, oal_ref, obc_ref, obl_ref, occ_ref, ocl_ref, pg_ref, mod_ref,
                  wa_ref, wb_ref, wc_ref, wo_ref, o_ref, *, ctx_tiles):
    is_ctx = pl.program_id(0) < ctx_tiles
    merged = jnp.zeros(x_ref.shape, F32)
    for j, (c_ref, l_ref, w_ref) in enumerate(((oac_ref, oal_ref, wa_ref), (obc_ref, obl_ref, wb_ref),
                                              (occ_ref, ocl_ref, wc_ref))):
        gate = pg_ref[:, j * D_MODEL:(j + 1) * D_MODEL].astype(F32)
        branch = jnp.where(is_ctx, c_ref[...], l_ref[...]).astype(BF16)
        merged = merged + gate * jnp.dot(branch, w_ref[...], preferred_element_type=F32)
    y = jnp.dot(merged.astype(BF16), wo_ref[...], preferred_element_type=F32)
    o_ref[...] = x_ref[...] + mod_ref[2] * y


def _merge(li, x, branches, pg, mod, wpa, wpb, wpc, wout):
    tm = TM
    ctx_tiles = N_TOK_CTX // tm
    row = lambda w: pl.BlockSpec((tm, w), lambda i: (i, 0))
    ctx_row = lambda w: pl.BlockSpec((tm, w), lambda i: (jnp.minimum(i, ctx_tiles - 1), 0))
    lat_row = lambda w: pl.BlockSpec((tm, w), lambda i: (jnp.maximum(i - ctx_tiles, 0), 0))
    branch_specs = []
    for w in (WIDTH_A, WIDTH_BV, WIDTH_C):
        branch_specs += [ctx_row(w), lat_row(w)]
    return pl.pallas_call(
        functools.partial(_merge_kernel, ctx_tiles=ctx_tiles),
        grid=(N_TOK // tm,),
        in_specs=[row(D_MODEL)] + branch_specs + [row(SEC_G), _mod_spec(li, tm),
                  _layer_resident(li, (WIDTH_A, D_MODEL)), _layer_resident(li, (WIDTH_BV, D_MODEL)),
                  _layer_resident(li, (WIDTH_C, D_MODEL)), _layer_resident(li, (D_MODEL, D_MODEL))],
        out_specs=row(D_MODEL),
        out_shape=jax.ShapeDtypeStruct((N_TOK, D_MODEL), F32),
        compiler_params=pltpu.CompilerParams(dimension_semantics=("arbitrary",), vmem_limit_bytes=VMEM_LIMIT),
        name="merge",
    )(x, *[b for pair in branches for b in pair], pg, mod, wpa, wpb, wpc, wout)


def _ffn_kernel(x_ref, g_ref, mod_ref, wg_ref, wu_ref, wo_ref, o_ref, h_scr, acc_scr):
    f = pl.program_id(1)

    @pl.when(f == 0)
    def _():
        h_scr[...] = _modnorm(x_ref[...], g_ref[...], mod_ref[3], mod_ref[4]).astype(BF16)
        acc_scr[...] = jnp.zeros_like(acc_scr)

    h = h_scr[...]
    gate = jnp.dot(h, wg_ref[...], preferred_element_type=F32)
    up = jnp.dot(h, wu_ref[...], preferred_element_type=F32)
    act = (gate * _sigmoid(gate) * up).astype(BF16)
    acc_scr[...] += jnp.dot(act, wo_ref[...], preferred_element_type=F32)

    @pl.when(f == pl.num_programs(1) - 1)
    def _():
        o_ref[...] = x_ref[...] + mod_ref[5] * acc_scr[...]


def _ffn(li, x, gain, mod, w_in, w_out):
    tm = TM_FFN
    nf = D_FF // TF
    return pl.pallas_call(
        _ffn_kernel,
        grid=(N_TOK // tm, nf),
        in_specs=[
            pl.BlockSpec((tm, D_MODEL), lambda i, f: (i, 0)),
            pl.BlockSpec((None, 1, D_MODEL), lambda i, f: (li, 0, 0)),
            _mod_spec(li, tm),
            pl.BlockSpec((D_MODEL, TF), lambda i, f: (0, f)),
            pl.BlockSpec((D_MODEL, TF), lambda i, f: (0, nf + f)),
            pl.BlockSpec((TF, D_MODEL), lambda i, f: (f, 0)),
        ],
        out_specs=pl.BlockSpec((tm, D_MODEL), lambda i, f: (i, 0)),
        out_shape=jax.ShapeDtypeStruct((N_TOK, D_MODEL), F32),
        scratch_shapes=[pltpu.VMEM((tm, D_MODEL), BF16), pltpu.VMEM((tm, D_MODEL), F32)],
        compiler_params=pltpu.CompilerParams(dimension_semantics=("arbitrary", "arbitrary"),
                                             vmem_limit_bytes=VMEM_LIMIT),
        name="ffn",
    )(x, gain, mod, w_in, w_in, w_out)


def _pad_rows(w, offset, total):
    pads = [(0, 0)] * w.ndim
    pads[-2] = (offset, total - offset - w.shape[-2])
    return jnp.pad(w, pads)


def kernel(x_prompt, x_sample, cache_k, cache_v, state_gla, state_rwkv, c, c_ctx, w_ada, b_ada, g_mix, g_ffn,
           w_in, q_gain, k_gain, gk_w2, gk_b, gla_gain, mu_c, w0, w2, a0, a2, g2, k_k, k_a, r_k, lnx_w, lnx_b,
           w_po_a, w_po_b, w_po_c, w_out, w_ffn_in, w_ffn_out):
    x = jnp.concatenate([x_prompt.reshape(N_TOK_CTX, D_MODEL), x_sample.reshape(N_TOK_LAT, D_MODEL)], axis=0)
    cond = jnp.concatenate([c_ctx[None, :], c, jnp.zeros((COND_ROWS - N_COND, D_MODEL), F32)], axis=0)
    mod = _modulation(cond, w_ada, b_ada)[:, :N_COND].reshape(DEPTH, N_COND, 6, 1, D_MODEL)

    w_ab = w_in[:, :, :SEC_A + SEC_B].astype(BF16)
    w_cg = w_in[:, :, OFF_C:].astype(BF16)
    gk_w2p = jnp.stack([_pad_rows(gk_w2[:, 0], 0, LANES), _pad_rows(gk_w2[:, 1], GK_RANK, LANES)], axis=1)
    w2p = jnp.stack([_pad_rows(w2[:, 0], 0, LANES), _pad_rows(w2[:, 1], W_LORA, LANES)], axis=1)
    a2p = jnp.stack([_pad_rows(a2[:, 0], 0, LANES), _pad_rows(a2[:, 1], A_LORA, LANES)], axis=1)
    wpa, wpb, wpc, wout = (w.astype(BF16) for w in (w_po_a, w_po_b, w_po_c, w_out))
    rope = _rope_tables()
    lat_blk = N_TOK_CTX // DEC_SEQ
    g_mix3 = g_mix.reshape(DEPTH, 1, D_MODEL)
    g_ffn3 = g_ffn.reshape(DEPTH, 1, D_MODEL)

    ks, vs, sgs, srs = [], [], [], []
    for li in range(DEPTH):
        vec = lambda a: a[li].reshape(1, -1)
        pa, pb, pc, pg, wfi, wfo = _inproj(li, x, g_mix3, mod, w_ab, w_cg, w_ffn_in, w_ffn_out)

        qg = jnp.tile(vec(q_gain), (1, LANES // HD_A))
        kg = jnp.tile(vec(k_gain), (1, LANES // HD_A))
        oa_c, k_l, v_l = _attention(pa, qg, kg, BATCH, SEQ, 0)
        (oa_l,) = _attention(pa, qg, kg, DEC_BATCH, DEC_SEQ, lat_blk,
                             cache=(cache_k[:, li].reshape(DEC_BATCH, PAST_LEN, KV_A),
                                    cache_v[:, li].reshape(DEC_BATCH, PAST_LEN, KV_A)), rope=rope)

        gkb = gk_b[li].reshape(2, 1, WIDTH_BK)
        ob_c, sg_l = _gla(pb, gk_w2p[li], gkb, vec(gla_gain), BATCH, SEQ, 0, want_state=True)
        (ob_l,) = _gla(pb, gk_w2p[li], gkb, vec(gla_gain), DEC_BATCH, DEC_SEQ, lat_blk, s0=state_gla[:, li])

        rw_args = (pc, vec(mu_c), w0[li].reshape(2, 1, WIDTH_C), w2p[li], a0[li].reshape(2, 1, WIDTH_C), a2p[li],
                   g2[li], vec(k_k), vec(k_a), r_k[li].reshape(1, WIDTH_C), vec(lnx_w), vec(lnx_b))
        oc_c, sr_l = _rwkv(*rw_args, BATCH, SEQ, 0, want_state=True)
        (oc_l,) = _rwkv(*rw_args, DEC_BATCH, DEC_SEQ, lat_blk, s0=state_rwkv[:, li])

        x = _merge(li, x, ((oa_c, oa_l), (ob_c, ob_l), (oc_c, oc_l)), pg, mod, wpa, wpb, wpc, wout)
        x = _ffn(li, x, g_ffn3, mod, wfi, wfo)

        ks.append(k_l.reshape(BATCH, SEQ, N_KV_A, HD_A))
        vs.append(v_l.reshape(BATCH, SEQ, N_KV_A, HD_A))
        sgs.append(sg_l)
        srs.append(sr_l)

    y_prompt = x[:N_TOK_CTX].reshape(BATCH, SEQ, D_MODEL)
    y_sample = x[N_TOK_CTX:].reshape(DEC_BATCH, DEC_SEQ, D_MODEL)
    return (y_prompt, y_sample, jnp.stack(ks, axis=1), jnp.stack(vs, axis=1),
            jnp.stack(sgs, axis=1), jnp.stack(srs, axis=1))
```
